```python
import math
import jax
import jax.numpy as jnp
from jax import lax
import numpy as np

D_MODEL = 1024
BATCH = 2
SEQ = 8192
DEPTH = 2

CTX_LEN = 256
GRID_W = 64
N_MIXERS = 2
N_HGRN_LAYERS = (DEPTH + N_MIXERS - 1) // N_MIXERS
N_DIFF_LAYERS = DEPTH // N_MIXERS
HGRN_HEADS = 8
HGRN_EXPAND = 128
HGRN_FDIM = HGRN_HEADS * HGRN_EXPAND
HGRN_HEAD_V = D_MODEL // HGRN_HEADS
CHUNK = 64
DIFF_HEADS = 8
DIFF_HEAD_DIM = D_MODEL // DIFF_HEADS // 2
ROPE_BASE = 10000.0
ROPE_PAIRS_PER_AXIS = DIFF_HEAD_DIM // 4
Q_BLOCK = 128
N_EXPERTS = 16
EC_CAPACITY_FACTOR = 2
D_EXPERT = 2816
NORM_EPS = 1e-6

kernel_name = "hybrid_hgrn2_diffattn_ecmoe_dit"


def _rmsnorm(x, g):
    xf = x.astype(jnp.float32)
    y = xf * lax.rsqrt(jnp.mean(xf * xf, axis=-1, keepdims=True) + NORM_EPS)
    return (y * g.astype(jnp.float32)).astype(x.dtype)


def _gla_step(S, inp):
    q, k, v, lf = inp
    b = jnp.cumsum(lf, axis=2)
    tril = jnp.tril(jnp.ones((CHUNK, CHUNK), dtype=bool))
    o_inter = jnp.einsum('bhtk,bhkv->bhtv', q * jnp.exp(b), S)
    rel = b[:, :, :, None, :] - b[:, :, None, :, :]
    decay = jnp.exp(jnp.where(tril[:, :, None], rel, -jnp.inf))
    a = jnp.einsum('bhtk,bhsk,bhtsk->bhts', q, k, decay)
    o = o_inter + jnp.einsum('bhts,bhsv->bhtv', a, v)
    b_last = b[:, :, -1:, :]
    S_new = jnp.exp(b_last[:, :, 0, :])[..., None] * S + jnp.einsum(
        'bhsk,bhsv->bhkv', k * jnp.exp(b_last - b), v)
    return S_new, o


def _gla_chunked(q, k, v, lf, S0):
    B, T, H, _ = q.shape
    V = v.shape[-1]
    n = T // CHUNK

    def to_blocks(t):
        return t.reshape(B, n, CHUNK, H, t.shape[-1]).transpose(1, 0, 3, 2, 4)

    S, o = lax.scan(_gla_step, S0, (to_blocks(q), to_blocks(k), to_blocks(v), to_blocks(lf)))
    return S, o.transpose(1, 0, 3, 2, 4).reshape(B, T, H, V)


def _hgrn2_mixer(h_ctx, h_lat, w_in, lb, norm_g, w_out):
    splits = [HGRN_FDIM, 2 * HGRN_FDIM, 3 * HGRN_FDIM, 3 * HGRN_FDIM + D_MODEL]

    def project(h):
        B, T, _ = h.shape
        q, f_fw, f_bw, v, g = jnp.split(h @ w_in, splits, axis=-1)
        hd = lambda t, d: t.reshape(B, T, HGRN_HEADS, d).astype(jnp.float32)
        return (jax.nn.silu(hd(q, HGRN_EXPAND)), hd(f_fw, HGRN_EXPAND), hd(f_bw, HGRN_EXPAND),
                hd(v, HGRN_HEAD_V), g)

    def gates(f_logit):
        f = lb + (1.0 - lb) * jax.nn.sigmoid(f_logit)
        return jnp.log(f), 1.0 - f

    qc, fc_fw, fc_bw, vc, gc = project(h_ctx)
    ql, fl_fw, fl_bw, vl, gl = project(h_lat)
    B = h_lat.shape[0]

    def direction(fc, fl, reverse):
        lfc, kc = gates(fc)
        lfl, kl = gates(fl)
        flip = (lambda t: jnp.flip(t, axis=1)) if reverse else (lambda t: t)
        S0 = jnp.zeros((B, HGRN_HEADS, HGRN_EXPAND, HGRN_HEAD_V), jnp.float32)
        S_ctx, oc = _gla_chunked(flip(qc), flip(kc), flip(vc), flip(lfc), S0)
        _, ol = _gla_chunked(flip(ql), flip(kl), flip(vl), flip(lfl), S_ctx)
        return flip(oc), flip(ol)

    oc_f, ol_f = direction(fc_fw, fl_fw, False)
    oc_b, ol_b = direction(fc_bw, fl_bw, True)

    def readout(o, g):
        Bo, T = o.shape[:2]
        o = _rmsnorm(o, norm_g) * jax.nn.silu(g.astype(jnp.float32)).reshape(
            Bo, T, HGRN_HEADS, HGRN_HEAD_V)
        return o.reshape(Bo, T, D_MODEL).astype(g.dtype) @ w_out

    return readout(oc_f + oc_b, gc), readout(ol_f + ol_b, gl)


def _axial_rope_tables(T):
    rows = T // GRID_W
    row = jnp.repeat(jnp.arange(rows, dtype=jnp.float32), GRID_W)
    col = jnp.tile(jnp.arange(GRID_W, dtype=jnp.float32), rows)
    freq = ROPE_BASE ** (-jnp.arange(ROPE_PAIRS_PER_AXIS, dtype=jnp.float32) / ROPE_PAIRS_PER_AXIS)
    ang = jnp.concatenate([row[:, None] * freq, col[:, None] * freq], axis=-1)
    return jnp.cos(ang), jnp.sin(ang)


def _apply_rope(x, cos, sin):
    c = cos[None, :, None, None, :]
    s = sin[None, :, None, None, :]
    x1, x2 = jnp.split(x, 2, axis=-1)
    return jnp.concatenate([x1 * c - x2 * s, x2 * c + x1 * s], axis=-1)


def _diff_scores(qb, keys, vals, lam):
    s = jnp.einsum('bqhcd,bkhcd->bhcqk', qb, keys) * (DIFF_HEAD_DIM ** -0.5)
    p = jax.nn.softmax(s, axis=-1)
    w = p[:, :, 0] - lam * p[:, :, 1]
    return jnp.einsum('bhqk,bkhv->bqhv', w, vals)


def _diff_attention(h_ctx, h_lat, w_in, lam_vecs, subln_g, w_out, lam_init, need_ctx):
    def project(h):
        B, T, _ = h.shape
        q, k, v = jnp.split(h @ w_in, 3, axis=-1)
        q = q.reshape(B, T, DIFF_HEADS, 2, DIFF_HEAD_DIM).astype(jnp.float32)
        k = k.reshape(B, T, DIFF_HEADS, 2, DIFF_HEAD_DIM).astype(jnp.float32)
        v = v.reshape(B, T, DIFF_HEADS, 2 * DIFF_HEAD_DIM).astype(jnp.float32)
        return q, k, v

    qc, kc, vc = project(h_ctx)
    ql, kl, vl = project(h_lat)
    B, T = h_lat.shape[:2]
    cos, sin = _axial_rope_tables(T)
    ql = _apply_rope(ql, cos, sin)
    kl = _apply_rope(kl, cos, sin)
    lv = lam_vecs.astype(jnp.float32)
    lam = jnp.exp(jnp.sum(lv[0] * lv[1])) - jnp.exp(jnp.sum(lv[2] * lv[3])) + lam_init

    keys = jnp.concatenate([kc, kl], axis=1)
    vals = jnp.concatenate([vc, vl], axis=1)
    nb = T // Q_BLOCK
    qblk = ql.reshape(B, nb, Q_BLOCK, DIFF_HEADS, 2, DIFF_HEAD_DIM).transpose(1, 0, 2, 3, 4, 5)
    ol = lax.map(lambda qb: _diff_scores(qb, keys, vals, lam), qblk)
    ol = ol.transpose(1, 0, 2, 3, 4).reshape(B, T, DIFF_HEADS, 2 * DIFF_HEAD_DIM)

    def readout(o):
        Bo, To = o.shape[:2]
        o = _rmsnorm(o, subln_g) * (1.0 - lam_init)
        return o.reshape(Bo, To, D_MODEL).astype(h_lat.dtype) @ w_out

    y_lat = readout(ol)
    y_ctx = readout(_diff_scores(qc, kc, vc, lam)) if need_ctx else None
    return y_ctx, y_lat


def _ec_moe(h, w_router, w_gate, w_up, w_down):
    B, T, _ = h.shape
    cap = EC_CAPACITY_FACTOR * T // N_EXPERTS
    aff = jax.nn.softmax((h @ w_router).astype(jnp.float32), axis=-1)
    g, idx = lax.top_k(jnp.swapaxes(aff, 1, 2), cap)
    xs = jax.vmap(lambda hb, ib: hb[ib])(h, idx)
    a = jnp.einsum('becd,edf->becf', xs, w_gate)
    u = jnp.einsum('becd,edf->becf', xs, w_up)
    y = jnp.einsum('becf,efd->becd', jax.nn.silu(a) * u, w_down) * g[..., None].astype(h.dtype)
    bidx = jnp.arange(B)[:, None, None]
    return jnp.zeros_like(h).at[bidx, idx].add(y.astype(h.dtype))


def setup_inputs(seed: int = 0) -> dict:
    key = jax.random.key(seed)
    ks = jax.random.split(key, 24)
    nrm = lambda k, shape, s: jax.random.normal(k, shape, jnp.float32) * s
    D = D_MODEL
    return {
        "x": nrm(ks[0], (BATCH, SEQ, D), 1.0),
        "c": nrm(ks[1], (BATCH, D), 1.0),
        "ctx": nrm(ks[2], (BATCH, CTX_LEN, D), 1.0),
        "c_ctx": nrm(ks[3], (D,), 1.0),
        "ada_w": nrm(ks[4], (DEPTH, D, 6 * D), 0.5 * D ** -0.5),
        "ada_b": nrm(ks[5], (DEPTH, 6 * D), 0.02),
        "norm_mix": 1.0 + nrm(ks[6], (DEPTH, D), 0.05),
        "norm_ffn": 1.0 + nrm(ks[7], (DEPTH, D), 0.05),
        "norm_final": 1.0 + nrm(ks[8], (D,), 0.05),
        "hgrn_w_in": nrm(ks[9], (N_HGRN_LAYERS, D, 3 * HGRN_FDIM + 2 * D), D ** -0.5),
        "hgrn_lb_logits": nrm(ks[10], (N_HGRN_LAYERS + 1, HGRN_FDIM), 0.5),
        "hgrn_norm": 1.0 + nrm(ks[11], (N_HGRN_LAYERS, HGRN_HEAD_V), 0.05),
        "hgrn_w_out": nrm(ks[12], (N_HGRN_LAYERS, D, D), D ** -0.5),
        "diff_w_in": nrm(ks[13], (N_DIFF_LAYERS, D, 3 * D), D ** -0.5),
        "diff_lambda": nrm(ks[14], (N_DIFF_LAYERS, 4, DIFF_HEAD_DIM), 0.1),
        "diff_subln": 1.0 + nrm(ks[15], (N_DIFF_LAYERS, 2 * DIFF_HEAD_DIM), 0.05),
        "diff_w_out": nrm(ks[16], (N_DIFF_LAYERS, D, D), D ** -0.5),
        "moe_router": nrm(ks[17], (DEPTH, D, N_EXPERTS), D ** -0.5),
        "moe_w_gate": nrm(ks[18], (DEPTH, N_EXPERTS, D, D_EXPERT), D ** -0.5),
        "moe_w_up": nrm(ks[19], (DEPTH, N_EXPERTS, D, D_EXPERT), D ** -0.5),
        "moe_w_down": nrm(ks[20], (DEPTH, N_EXPERTS, D_EXPERT, D), D_EXPERT ** -0.5),
    }


def reference(x, c, ctx, c_ctx, ada_w, ada_b, norm_mix, norm_ffn, norm_final,
              hgrn_w_in, hgrn_lb_logits, hgrn_norm, hgrn_w_out,
              diff_w_in, diff_lambda, diff_subln, diff_w_out,
              moe_router, moe_w_gate, moe_w_up, moe_w_down):
    lower_bounds = jnp.cumsum(jax.nn.softmax(hgrn_lb_logits.astype(jnp.float32), axis=0), axis=0)
    silu_c = jax.nn.silu(c)
    silu_cc = jax.nn.silu(c_ctx)
    x_lat, x_ctx = x, ctx
    for i in range(DEPTH):
        last = i == DEPTH - 1
        j = i // N_MIXERS
        mod_l = (silu_c @ ada_w[i] + ada_b[i])[:, None, :]
        mod_c = (silu_cc @ ada_w[i] + ada_b[i])[None, None, :]
        sh_m, sc_m, gt_m, sh_f, sc_f, gt_f = jnp.split(mod_l, 6, axis=-1)
        csh_m, csc_m, cgt_m, csh_f, csc_f, cgt_f = jnp.split(mod_c, 6, axis=-1)

        h_lat = _rmsnorm(x_lat, norm_mix[i]) * (1.0 + sc_m) + sh_m
        h_ctx = _rmsnorm(x_ctx, norm_mix[i]) * (1.0 + csc_m) + csh_m
        if i % N_MIXERS == 0:
            lb = lower_bounds[j].reshape(HGRN_HEADS, HGRN_EXPAND)
            y_ctx, y_lat = _hgrn2_mixer(h_ctx, h_lat, hgrn_w_in[j], lb, hgrn_norm[j], hgrn_w_out[j])
        else:
            lam_init = 0.8 - 0.6 * math.exp(-0.3 * i)
            y_ctx, y_lat = _diff_attention(h_ctx, h_lat, diff_w_in[j], diff_lambda[j], diff_subln[j],
                                           diff_w_out[j], lam_init, not last)
        x_lat = x_lat + gt_m * y_lat
        h_lat = _rmsnorm(x_lat, norm_ffn[i]) * (1.0 + sc_f) + sh_f
        x_lat = x_lat + gt_f * _ec_moe(h_lat, moe_router[i], moe_w_gate[i], moe_w_up[i], moe_w_down[i])
        if not last:
            x_ctx = x_ctx + cgt_m * y_ctx
            h_ctx = _rmsnorm(x_ctx, norm_ffn[i]) * (1.0 + csc_f) + csh_f
            x_ctx = x_ctx + cgt_f * _ec_moe(h_ctx, moe_router[i], moe_w_gate[i], moe_w_up[i], moe_w_down[i])
    return _rmsnorm(x_lat, norm_final)
```

```python
import functools
import math

import jax
import jax.numpy as jnp
import numpy as np
from jax import lax
from jax.experimental import pallas as pl
from jax.experimental.pallas import tpu as pltpu

F32 = jnp.float32
BF16 = jnp.bfloat16
I32 = jnp.int32

NORM_EPS = 1e-6
LANES = 128
HEAD_W = 128
N_HEADS = 8
GLA_CHUNK = 64
GLA_LEVELS = (32, 16, 8, 4, 2, 1)
ROPE_BASE = 10000.0
GRID_W = 64
TOKEN_BLOCK = 256
VMEM_LIMIT = 56 * 1024 * 1024

_NT = (((1,), (1,)), ((), ()))
_TN = (((0,), (0,)), ((), ()))


def _pick(n, cands):
    for c in cands:
        if n % c == 0:
            return c
    raise ValueError(f"no tile for {n} in {cands}")


def _params(sem):
    return pltpu.CompilerParams(dimension_semantics=sem, vmem_limit_bytes=VMEM_LIMIT)


def _sigmoid(x):
    return 1.0 / (1.0 + jnp.exp(-x))


def _norm_mod(x, g, scale2, shift2, row0, n_ctx):
    ms = jnp.mean(x * x, axis=-1, keepdims=True)
    y = x * lax.rsqrt(ms + NORM_EPS) * g
    rows = row0 + lax.broadcasted_iota(I32, (x.shape[0], 1), 0)
    is_ctx = rows < n_ctx
    sc = jnp.where(is_ctx, scale2[0:1], scale2[1:2])
    sh = jnp.where(is_ctx, shift2[0:1], shift2[1:2])
    return y * sc + sh


def _ada_kernel(c_ref, w_ref, b_ref, o_ref):
    c = c_ref[...]
    s = c * _sigmoid(c)
    o_ref[0] = jnp.dot(s, w_ref[0], precision=lax.Precision.HIGHEST,
                       preferred_element_type=F32) + b_ref[0]


def _ada(cvec, ada_w, ada_b):
    depth, d, n = ada_w.shape
    rows = cvec.shape[0]
    tn = _pick(n, (1024, 512, 256, 128))
    return pl.pallas_call(
        _ada_kernel,
        grid=(depth, n // tn),
        in_specs=[pl.BlockSpec((rows, d), lambda l, j: (0, 0)),
                  pl.BlockSpec((1, d, tn), lambda l, j: (l, 0, j)),
                  pl.BlockSpec((1, 1, tn), lambda l, j: (l, 0, j))],
        out_specs=pl.BlockSpec((1, rows, tn), lambda l, j: (l, 0, j)),
        out_shape=jax.ShapeDtypeStruct((depth, rows, n), F32),
        compiler_params=_params(("arbitrary", "arbitrary")),
        name="adaln",
    )(cvec, ada_w, ada_b.reshape(depth, 1, n))


def _inproj_kernel(x_ref, g_ref, sc_ref, sh_ref, w_ref, *rest, tm, n_ctx, rope):
    if rope:
        cos_ref, sin_ref, o_ref, h_ref = rest
    else:
        o_ref, h_ref = rest
    i = pl.program_id(1)
    n = pl.program_id(2)

    @pl.when(n == 0)
    def _():
        h = _norm_mod(x_ref[0], g_ref[...], sc_ref[0], sh_ref[0], i * tm, n_ctx)
        h_ref[...] = h.astype(BF16)

    acc = jnp.dot(h_ref[...], w_ref[...], preferred_element_type=F32)
    if not rope:
        o_ref[0] = acc.astype(o_ref.dtype)
    else:
        @pl.when(n < 2)
        def _():
            cos = cos_ref[0]
            sin = sin_ref[0]
            lane = lax.broadcasted_iota(I32, (1, HEAD_W), 1)
            first = (lane % 64) < 32
            for hd in range(acc.shape[1] // HEAD_W):
                a = acc[:, hd * HEAD_W:(hd + 1) * HEAD_W]
                rot = jnp.where(first, pltpu.roll(a, HEAD_W - 32, 1), pltpu.roll(a, 32, 1))
                o_ref[0, :, hd * HEAD_W:(hd + 1) * HEAD_W] = (a * cos + rot * sin).astype(o_ref.dtype)

        @pl.when(n >= 2)
        def _():
            o_ref[0] = acc.astype(o_ref.dtype)


def _inproj(x, g, scale2, shift2, w_bf16, n_ctx, out_dtype, rope_tabs=None):
    b, r, d = x.shape
    n = w_bf16.shape[1]
    tm = _pick(r, (768, 512, 384, 256, 128))
    tn = 1024
    rope = rope_tabs is not None
    in_specs = [pl.BlockSpec((1, tm, d), lambda bb, i, j: (bb, i, 0)),
                pl.BlockSpec((1, d), lambda bb, i, j: (0, 0)),
                pl.BlockSpec((1, 2, d), lambda bb, i, j: (bb, 0, 0)),
                pl.BlockSpec((1, 2, d), lambda bb, i, j: (bb, 0, 0)),
                pl.BlockSpec((d, tn), lambda bb, i, j: (0, j))]
    args = [x, g.reshape(1, d), scale2, shift2, w_bf16]
    if rope:
        cos_t, sin_t = rope_tabs
        spec = pl.BlockSpec((1, tm, HEAD_W), lambda bb, i, j: (jnp.minimum(j, 1), i, 0))
        in_specs += [spec, spec]
        args += [cos_t, sin_t]
    return pl.pallas_call(
        functools.partial(_inproj_kernel, tm=tm, n_ctx=n_ctx, rope=rope),
        grid=(b, r // tm, n // tn),
        in_specs=in_specs,
        out_specs=pl.BlockSpec((1, tm, tn), lambda bb, i, j: (bb, i, j)),
        out_shape=jax.ShapeDtypeStruct((b, r, n), out_dtype),
        scratch_shapes=[pltpu.VMEM((tm, d), BF16)],
        compiler_params=_params(("arbitrary", "arbitrary", "arbitrary")),
        name="inproj_rope" if rope else "inproj",
    )(*args)


def _gla_consts():
    c = GLA_CHUNK
    nl = len(GLA_LEVELS)
    t = np.arange(c)
    dall = np.zeros((2, (nl + 2) * c + 8, c), np.float32)
    rowsel = np.zeros((2, nl, c, HEAD_W), np.float32)
    masks = np.zeros((2, nl + 1, c, c), np.float32)
    for d in range(2):
        p = t if d == 0 else c - 1 - t
        for li, s in enumerate(GLA_LEVELS):
            blk = p // (2 * s)
            second = (p // s) % 2 == 1
            href = blk * 2 * s + s - 1
            pr = p[None, :]
            inc_q = (href[:, None] < pr) & (pr <= p[:, None])
            inc_k = (p[:, None] < pr) & (pr <= href[:, None])
            dall[d, li * c:(li + 1) * c] = np.where(second[:, None], inc_q, inc_k)
            rowsel[d, li] = second[:, None]
            masks[d, li] = (blk[:, None] == blk[None, :]) & second[:, None] & (~second)[None, :]
        dall[d, nl * c:(nl + 1) * c] = p[None, :] <= p[:, None]
        dall[d, (nl + 1) * c:(nl + 2) * c] = p[None, :] > p[:, None]
        dall[d, (nl + 2) * c:] = 1.0
        masks[d, nl] = np.eye(c)
    return dall, rowsel, masks


def _gla_kernel(q_ref, f_ref, v_ref, lb_ref, dall_ref, rowsel_ref, masks_ref, o_ref, st_ref, *, tb):
    c = GLA_CHUNK
    nl = len(GLA_LEVELS)
    nc = tb // c
    d = pl.program_id(1)
    n = pl.program_id(3)

    @pl.when(n == 0)
    def _():
        st_ref[...] = jnp.zeros_like(st_ref)

    lb = lb_ref[0]
    dall = dall_ref[0]
    for ci in range(nc):
        cc = jnp.where(d == 0, ci, nc - 1 - ci)
        r0 = pl.multiple_of(cc * c, c)
        q = q_ref[0, pl.ds(r0, c), :]
        qf = q * _sigmoid(q)
        f = lb + (1.0 - lb) * _sigmoid(f_ref[0, pl.ds(r0, c), :])
        lf = jnp.log(f)
        kk = 1.0 - f
        v = v_ref[0, pl.ds(r0, c), :]
        vb = v.astype(BF16)

        hi = lf.astype(BF16)
        r1 = lf - hi.astype(F32)
        mid = r1.astype(BF16)
        lo = (r1 - mid.astype(F32)).astype(BF16)
        lf3 = jnp.concatenate([hi, mid, lo], axis=1)
        ex3 = jnp.dot(dall, lf3, preferred_element_type=F32)
        ex = ex3[:, :HEAD_W] + ex3[:, HEAD_W:2 * HEAD_W] + ex3[:, 2 * HEAD_W:]
        e = jnp.exp(ex)

        st = st_ref[...]
        qd = (qf * e[nl * c:(nl + 1) * c]).astype(BF16)
        o = lax.dot_general(qd, st.astype(BF16), _NT, preferred_element_type=F32)

        qb = qf.astype(BF16)
        kb = kk.astype(BF16)
        a = masks_ref[0, nl] * lax.dot_general(qb, kb, _NT, preferred_element_type=F32)
        for li in range(nl):
            x = jnp.where(rowsel_ref[0, li] > 0, qf, kk) * e[li * c:(li + 1) * c]
            xb = x.astype(BF16)
            a = a + masks_ref[0, li] * lax.dot_general(xb, xb, _NT, preferred_element_type=F32)
        o = o + jnp.dot(a.astype(BF16), vb, preferred_element_type=F32)
        o_ref[0, 0, pl.ds(r0, c), :] = o

        kdec = (kk * e[(nl + 1) * c:(nl + 2) * c]).astype(BF16)
        upd = lax.dot_general(vb, kdec, _TN, preferred_element_type=F32)
        st_ref[...] = st * e[(nl + 2) * c:(nl + 2) * c + 1] + upd


def _gla(proj, lb, n_ctx):
    b, r, _ = proj.shape
    tb = TOKEN_BLOCK
    assert n_ctx % tb == 0 and r % tb == 0
    nblk = r // tb
    ncb = n_ctx // tb
    dall, rowsel, masks = _gla_consts()
    nrow = dall.shape[1]
    nl = len(GLA_LEVELS)
    c = GLA_CHUNK

    def blk(d, n):
        bwd = jnp.where(n < ncb, ncb - 1 - n, nblk - 1 - (n - ncb))
        return jnp.where(d == 0, n, bwd)

    return pl.pallas_call(
        functools.partial(_gla_kernel, tb=tb),
        grid=(b, 2, N_HEADS, nblk),
        in_specs=[pl.BlockSpec((1, tb, HEAD_W), lambda bb, d, h, n: (bb, blk(d, n), h)),
                  pl.BlockSpec((1, tb, HEAD_W), lambda bb, d, h, n: (bb, blk(d, n), N_HEADS * (1 + d) + h)),
                  pl.BlockSpec((1, tb, HEAD_W), lambda bb, d, h, n: (bb, blk(d, n), 3 * N_HEADS + h)),
                  pl.BlockSpec((1, 1, HEAD_W), lambda bb, d, h, n: (h, 0, 0)),
                  pl.BlockSpec((1, nrow, c), lambda bb, d, h, n: (d, 0, 0)),
                  pl.BlockSpec((1, nl, c, HEAD_W), lambda bb, d, h, n: (d, 0, 0, 0)),
                  pl.BlockSpec((1, nl + 1, c, c), lambda bb, d, h, n: (d, 0, 0, 0))],
        out_specs=pl.BlockSpec((1, 1, tb, HEAD_W), lambda bb, d, h, n: (d, bb, blk(d, n), h)),
        out_shape=jax.ShapeDtypeStruct((2, b, r, N_HEADS * HEAD_W), F32),
        scratch_shapes=[pltpu.VMEM((HEAD_W, HEAD_W), F32)],
        compiler_params=_params(("arbitrary", "arbitrary", "arbitrary", "arbitrary")),
        name="gla",
    )(proj, proj, proj, lb.reshape(N_HEADS, 1, HEAD_W), jnp.asarray(dall, BF16),
      jnp.asarray(rowsel), jnp.asarray(masks))


def _hgrn_out_kernel(of_ref, ob_ref, g_ref, ng_ref, w_ref, x_ref, gate_ref, o_ref, y_ref, *, tm, n_ctx):
    i = pl.program_id(1)
    o = of_ref[0, 0] + ob_ref[0, 0]
    for h in range(N_HEADS):
        sl = slice(h * HEAD_W, (h + 1) * HEAD_W)
        oh = o[:, sl]
        ms = jnp.mean(oh * oh, axis=-1, keepdims=True)
        g = g_ref[0, :, sl]
        y_ref[:, sl] = (oh * lax.rsqrt(ms + NORM_EPS) * ng_ref[:, sl] * (g * _sigmoid(g))).astype(BF16)
    y = jnp.dot(y_ref[...], w_ref[...], preferred_element_type=F32)
    rows = i * tm + lax.broadcasted_iota(I32, (tm, 1), 0)
    gate = jnp.where(rows < n_ctx, gate_ref[0, 0:1], gate_ref[0, 1:2])
    o_ref[0] = x_ref[0] + gate * y


def _hgrn_out(o2, proj, norm_g, w_bf16, x, gate2, n_ctx):
    b, r, d = x.shape
    tm = _pick(r, (384, 256, 128))
    return pl.pallas_call(
        functools.partial(_hgrn_out_kernel, tm=tm, n_ctx=n_ctx),
        grid=(b, r // tm),
        in_specs=[pl.BlockSpec((1, 1, tm, d), lambda bb, i: (0, bb, i, 0)),
                  pl.BlockSpec((1, 1, tm, d), lambda bb, i: (1, bb, i, 0)),
                  pl.BlockSpec((1, tm, d), lambda bb, i: (bb, i, 4)),
                  pl.BlockSpec((1, d), lambda bb, i: (0, 0)),
                  pl.BlockSpec((d, d), lambda bb, i: (0, 0)),
                  pl.BlockSpec((1, tm, d), lambda bb, i: (bb, i, 0)),
                  pl.BlockSpec((1, 2, d), lambda bb, i: (bb, 0, 0))],
        out_specs=pl.BlockSpec((1, tm, d), lambda bb, i: (bb, i, 0)),
        out_shape=jax.ShapeDtypeStruct((b, r, d), F32),
        scratch_shapes=[pltpu.VMEM((tm, d), BF16)],
        compiler_params=_params(("arbitrary", "arbitrary")),
        name="hgrn_out",
    )(o2, o2, proj, jnp.tile(norm_g, N_HEADS).reshape(1, d), w_bf16, x, gate2)


def _attn_kernel(q_ref, k_ref, v_ref, lam_ref, g_ref, o_ref, *, tq, tk, lam_init):
    nk = k_ref.shape[1] // tk
    q = q_ref[0]
    lane = lax.broadcasted_iota(I32, (1, HEAD_W), 1)
    zero = jnp.zeros_like(q)
    qs = jnp.concatenate([jnp.where(lane < 64, q, zero), jnp.where(lane >= 64, q, zero)], axis=0)

    def body(ki, carry):
        m, l, acc = carry
        r0 = pl.multiple_of(ki * tk, tk)
        k = k_ref[0, pl.ds(r0, tk), :]
        v = v_ref[0, pl.ds(r0, tk), :]
        s = lax.dot_general(qs, k, _NT, preferred_element_type=F32)
        m_new = jnp.maximum(m, jnp.max(s, axis=-1, keepdims=True))
        p = jnp.exp(s - m_new)
        alpha = jnp.exp(m - m_new)
        l = alpha * l + jnp.sum(p, axis=-1, keepdims=True)
        acc = alpha * acc + jnp.dot(p.astype(BF16), v, preferred_element_type=F32)
        return m_new, l, acc

    m0 = jnp.full((2 * tq, 1), -jnp.inf, F32)
    l0 = jnp.zeros((2 * tq, 1), F32)
    a0 = jnp.zeros((2 * tq, HEAD_W), F32)
    _, l, acc = lax.fori_loop(0, nk, body, (m0, l0, a0))
    out = acc / l

    lv = lam_ref[...]
    s01 = jnp.sum(lv[0:1] * lv[1:2], axis=-1, keepdims=True)
    s23 = jnp.sum(lv[2:3] * lv[3:4], axis=-1, keepdims=True)
    lam = jnp.exp(s01) - jnp.exp(s23) + lam_init
    o = out[:tq] - lam * out[tq:]
    ms = jnp.mean(o * o, axis=-1, keepdims=True)
    o_ref[0] = (o * lax.rsqrt(ms + NORM_EPS) * g_ref[...] * (1.0 - lam_init)).astype(o_ref.dtype)


def _diff_attn(qkv, lam_vecs, subln_g, n_ctx, lam_init):
    b, r, _ = qkv.shape
    t = r - n_ctx
    tq = TOKEN_BLOCK
    assert n_ctx % tq == 0 and t % tq == 0
    tk = _pick(r, (768, 512, 256))
    qoff = n_ctx // tq
    return pl.pallas_call(
        functools.partial(_attn_kernel, tq=tq, tk=tk, lam_init=lam_init),
        grid=(b, N_HEADS, t // tq),
        in_specs=[pl.BlockSpec((1, tq, HEAD_W), lambda bb, h, i: (bb, i + qoff, h)),
                  pl.BlockSpec((1, r, HEAD_W), lambda bb, h, i: (bb, 0, N_HEADS + h)),
                  pl.BlockSpec((1, r, HEAD_W), lambda bb, h, i: (bb, 0, 2 * N_HEADS + h)),
                  pl.BlockSpec(lam_vecs.shape, lambda bb, h, i: (0, 0)),
                  pl.BlockSpec((1, HEAD_W), lambda bb, h, i: (0, 0))],
        out_specs=pl.BlockSpec((1, tq, HEAD_W), lambda bb, h, i: (bb, i, h)),
        out_shape=jax.ShapeDtypeStruct((b, t, N_HEADS * HEAD_W), BF16),
        compiler_params=_params(("arbitrary", "arbitrary", "arbitrary")),
        name="diff_attn",
    )(qkv, qkv, qkv, lam_vecs.astype(F32), subln_g.reshape(1, HEAD_W).astype(F32))


def _outproj_kernel(a_ref, w_ref, x_ref, gate_ref, o_ref):
    y = jnp.dot(a_ref[0], w_ref[...], preferred_element_type=F32)
    o_ref[0] = x_ref[0] + gate_ref[0] * y


def _outproj(a, w_bf16, x_cat, gate, n_ctx):
    b, t, d = a.shape
    tm = _pick(math.gcd(t, n_ctx), (512, 256, 128))
    off = n_ctx // tm
    return pl.pallas_call(
        _outproj_kernel,
        grid=(b, t // tm),
        in_specs=[pl.BlockSpec((1, tm, d), lambda bb, i: (bb, i, 0)),
                  pl.BlockSpec((d, d), lambda bb, i: (0, 0)),
                  pl.BlockSpec((1, tm, d), lambda bb, i: (bb, i + off, 0)),
                  pl.BlockSpec((1, 1, d), lambda bb, i: (bb, 0, 0))],
        out_specs=pl.BlockSpec((1, tm, d), lambda bb, i: (bb, i, 0)),
        out_shape=jax.ShapeDtypeStruct((b, t, d), F32),
        compiler_params=_params(("arbitrary", "arbitrary")),
        name="outproj",
    )(a, w_bf16, x_cat, gate)


def _moe_pre_kernel(x_ref, g_ref, sc_ref, sh_ref, wr_ref, h_ref, aff_ref, *, tm, n_ctx):
    i = pl.program_id(1)
    h = _norm_mod(x_ref[0], g_ref[...], sc_ref[0], sh_ref[0], i * tm, n_ctx)
    h_ref[0] = h.astype(BF16)
    logits = lax.dot_general(wr_ref[...], h, _NT, precision=lax.Precision.HIGHEST,
                             preferred_element_type=F32)
    mx = jnp.max(logits, axis=0, keepdims=True)
    ex = jnp.exp(logits - mx)
    aff_ref[0] = ex / jnp.sum(ex, axis=0, keepdims=True)


def _moe_pre(x, g, scale2, shift2, w_router, n_ctx):
    b, r, d = x.shape
    ne = w_router.shape[1]
    tm = _pick(r, (768, 512, 384, 256, 128))
    return pl.pallas_call(
        functools.partial(_moe_pre_kernel, tm=tm, n_ctx=n_ctx),
        grid=(b, r // tm),
        in_specs=[pl.BlockSpec((1, tm, d), lambda bb, i: (bb, i, 0)),
                  pl.BlockSpec((1, d), lambda bb, i: (0, 0)),
                  pl.BlockSpec((1, 2, d), lambda bb, i: (bb, 0, 0)),
                  pl.BlockSpec((1, 2, d), lambda bb, i: (bb, 0, 0)),
                  pl.BlockSpec((ne, d), lambda bb, i: (0, 0))],
        out_specs=[pl.BlockSpec((1, tm, d), lambda bb, i: (bb, i, 0)),
                   pl.BlockSpec((1, ne, tm), lambda bb, i: (bb, 0, i))],
        out_shape=[jax.ShapeDtypeStruct((b, r, d), BF16),
                   jax.ShapeDtypeStruct((b, ne, r), F32)],
        compiler_params=_params(("arbitrary", "arbitrary")),
        name="moe_pre",
    )(x, g.reshape(1, d), scale2, shift2, w_router.T)


def _topk_kernel(aff_ref, tri_ref, pos_ref, gsel_ref, r0_ref, *, cap):
    ne, t = aff_ref.shape[1], aff_ref.shape[2]
    tbk = TOKEN_BLOCK
    nblk = t // tbk
    bits = pltpu.bitcast(aff_ref[0], I32)
    thr = jnp.zeros((ne, 1), I32)
    for bit in range(30, -1, -1):
        cand = thr | (1 << bit)
        cnt = jnp.sum(jnp.where(bits >= cand, 1.0, 0.0), axis=1, keepdims=True)
        thr = jnp.where(cnt >= cap, cand, thr)
    n_gt = jnp.sum(jnp.where(bits > thr, 1.0, 0.0), axis=1, keepdims=True)
    need = cap - n_gt
    tri = tri_ref[...]
    carry_eq = jnp.zeros((ne, 1), F32)
    carry_sel = jnp.zeros((ne, 1), F32)
    r0_ref[0] = jnp.full((ne, LANES), cap, I32)
    for j in range(nblk):
        sl = slice(j * tbk, (j + 1) * tbk)
        a = aff_ref[0, :, sl]
        bj = pltpu.bitcast(a, I32)
        eq = jnp.where(bj == thr, 1.0, 0.0)
        gt = jnp.where(bj > thr, 1.0, 0.0)
        pe = jnp.dot(eq.astype(BF16), tri, preferred_element_type=F32) + carry_eq
        sel = gt + eq * jnp.where(pe - eq < need, 1.0, 0.0)
        ps = jnp.dot(sel.astype(BF16), tri, preferred_element_type=F32) + carry_sel
        pos_ref[0, :, sl] = jnp.where(sel > 0, ps - 1.0, -1.0).astype(I32)
        gsel_ref[0, :, sl] = a * sel
        r0_ref[0, :, j:j + 1] = carry_sel.astype(I32)
        carry_eq = pe[:, tbk - 1:tbk]
        carry_sel = ps[:, tbk - 1:tbk]


def _topk(aff, cap):
    b, ne, t = aff.shape
    tbk = TOKEN_BLOCK
    assert t % tbk == 0 and t // tbk < LANES
    tri = jnp.asarray(np.triu(np.ones((tbk, tbk), np.float32)), BF16)
    return pl.pallas_call(
        functools.partial(_topk_kernel, cap=cap),
        grid=(b,),
        in_specs=[pl.BlockSpec((1, ne, t), lambda bb: (bb, 0, 0)),
                  pl.BlockSpec((tbk, tbk), lambda bb: (0, 0))],
        out_specs=[pl.BlockSpec((1, ne, t), lambda bb: (bb, 0, 0)),
                   pl.BlockSpec((1, ne, t), lambda bb: (bb, 0, 0)),
                   pl.BlockSpec((1, ne, LANES), lambda bb: (bb, 0, 0))],
        out_shape=[jax.ShapeDtypeStruct((b, ne, t), I32),
                   jax.ShapeDtypeStruct((b, ne, t), F32),
                   jax.ShapeDtypeStruct((b, ne, LANES), I32)],
        compiler_params=_params(("arbitrary",)),
        name="topk",
    )(aff, tri)


def _block_range(r0_ref, row, nblk, lo, hi):
    jlo = lax.fori_loop(0, nblk, lambda j, c: c + (r0_ref[row, j + 1] <= lo).astype(I32), jnp.int32(0))
    jhi = lax.fori_loop(0, nblk, lambda j, c: c + (r0_ref[row, j] < hi).astype(I32), jnp.int32(0))
    return jlo, jhi


def _gather_kernel(r0_ref, pos_ref, g_ref, h_ref, xs_ref, gs_ref, acc_ref, gacc_ref, *, rt, nblk, ne):
    b, e, i = pl.program_id(0), pl.program_id(1), pl.program_id(2)
    tbk = TOKEN_BLOCK
    lo = i * rt
    jlo, jhi = _block_range(r0_ref, b * ne + e, nblk, lo, lo + rt)
    acc_ref[...] = jnp.zeros_like(acc_ref)
    gacc_ref[...] = jnp.zeros_like(gacc_ref)
    rank = lo + lax.broadcasted_iota(I32, (rt, tbk), 0)

    def body(j, _):
        hit = pos_ref[0, 0, pl.ds(j, 1), :] == rank
        oh = jnp.where(hit, 1.0, 0.0).astype(BF16)
        r0 = pl.multiple_of(j * tbk, tbk)
        acc_ref[...] += jnp.dot(oh, h_ref[0, pl.ds(r0, tbk), :], preferred_element_type=F32)
        gacc_ref[...] += jnp.sum(jnp.where(hit, g_ref[0, 0, pl.ds(j, 1), :], 0.0), axis=1, keepdims=True)
        return 0

    lax.fori_loop(jlo, jhi, body, 0)
    xs_ref[0, 0] = acc_ref[...].astype(BF16)
    gs_ref[0, 0] = jnp.broadcast_to(gacc_ref[...], (rt, LANES))


def _gather(r0, pos, gate, hb, cap):
    b, ne, t = pos.shape
    d = hb.shape[2]
    tbk = TOKEN_BLOCK
    nblk = t // tbk
    rt = min(cap, 256)
    pos4 = pos.reshape(b, ne, nblk, tbk)
    g4 = gate.reshape(b, ne, nblk, tbk)
    grid_spec = pltpu.PrefetchScalarGridSpec(
        num_scalar_prefetch=1,
        grid=(b, ne, cap // rt),
        in_specs=[pl.BlockSpec((1, 1, nblk, tbk), lambda bb, e, i, r: (bb, e, 0, 0)),
                  pl.BlockSpec((1, 1, nblk, tbk), lambda bb, e, i, r: (bb, e, 0, 0)),
                  pl.BlockSpec((1, t, d), lambda bb, e, i, r: (bb, 0, 0))],
        out_specs=[pl.BlockSpec((1, 1, rt, d), lambda bb, e, i, r: (e, bb, i, 0)),
                   pl.BlockSpec((1, 1, rt, LANES), lambda bb, e, i, r: (e, bb, i, 0))],
        scratch_shapes=[pltpu.VMEM((rt, d), F32), pltpu.VMEM((rt, 1), F32)],
    )
    return pl.pallas_call(
        functools.partial(_gather_kernel, rt=rt, nblk=nblk, ne=ne),
        grid_spec=grid_spec,
        out_shape=[jax.ShapeDtypeStruct((ne, b, cap, d), BF16),
                   jax.ShapeDtypeStruct((ne, b, cap, LANES), F32)],
        compiler_params=_params(("arbitrary", "arbitrary", "arbitrary")),
        name="moe_gather",
    )(r0.reshape(b * ne, LANES), pos4, g4, hb)


def _ffn_kernel(xs_ref, gs_ref, gt_ref, wg_ref, wu_ref, wd_ref, y_ref, acc_ref, *, nb, cap, rc):
    f = pl.program_id(1)
    nf = pl.num_programs(1)

    @pl.when(f == 0)
    def _():
        acc_ref[...] = jnp.zeros_like(acc_ref)

    wg = wg_ref[0].astype(BF16)
    wu = wu_ref[0].astype(BF16)
    wd = wd_ref[0].astype(BF16)
    rows = nb * cap
    for c0 in range(0, rows, rc):
        x = xs_ref[0, c0:c0 + rc, :]
        a = jnp.dot(x, wg, preferred_element_type=F32)
        u = jnp.dot(x, wu, preferred_element_type=F32)
        hm = (a * _sigmoid(a) * u).astype(BF16)
        acc_ref[c0:c0 + rc, :] += jnp.dot(hm, wd, preferred_element_type=F32)

    @pl.when(f == nf - 1)
    def _():
        for bb in range(nb):
            sl = slice(bb * cap, (bb + 1) * cap)
            y_ref[0, sl, :] = (acc_ref[sl, :] * gs_ref[0, sl, 0:1] * gt_ref[bb]).astype(BF16)


def _ffn(xs, gs, gate_f, w_gate, w_up, w_down):
    ne, b, cap, d = xs.shape
    fdim = w_gate.shape[2]
    tf = _pick(fdim, (256, 128))
    rows = b * cap
    rc = _pick(rows, (512, 256, 128, 64))
    y = pl.pallas_call(
        functools.partial(_ffn_kernel, nb=b, cap=cap, rc=rc),
        grid=(ne, fdim // tf),
        in_specs=[pl.BlockSpec((1, rows, d), lambda e, f: (e, 0, 0)),
                  pl.BlockSpec((1, rows, LANES), lambda e, f: (e, 0, 0)),
                  pl.BlockSpec((b, 1, d), lambda e, f: (0, 0, 0)),
                  pl.BlockSpec((1, d, tf), lambda e, f: (e, 0, f)),
                  pl.BlockSpec((1, d, tf), lambda e, f: (e, 0, f)),
                  pl.BlockSpec((1, tf, d), lambda e, f: (e, f, 0))],
        out_specs=pl.BlockSpec((1, rows, d), lambda e, f: (e, 0, 0)),
        out_shape=jax.ShapeDtypeStruct((ne, rows, d), BF16),
        scratch_shapes=[pltpu.VMEM((rows, d), F32)],
        compiler_params=_params(("arbitrary", "arbitrary")),
        name="moe_ffn",
    )(xs.reshape(ne, rows, d), gs.reshape(ne, rows, LANES), gate_f, w_gate, w_up, w_down)
    return y.reshape(ne, b, cap, d)


def _combine_kernel(r0_ref, pos_ref, y_ref, o_ref, *, rt, nblk, ne):
    b, e, i = pl.program_id(0), pl.program_id(2), pl.program_id(3)
    tbk = TOKEN_BLOCK
    lo = i * rt

    @pl.when((e == 0) & (i == 0))
    def _():
        o_ref[...] = jnp.zeros_like(o_ref)

    jlo, jhi = _block_range(r0_ref, b * ne + e, nblk, lo, lo + rt)
    rank = lo + lax.broadcasted_iota(I32, (rt, tbk), 0)
    y = y_ref[0, 0]

    def body(j, _):
        hit = pos_ref[0, 0, pl.ds(j, 1), :] == rank
        oh = jnp.where(hit, 1.0, 0.0).astype(BF16)
        r0 = pl.multiple_of(j * tbk, tbk)
        o_ref[0, pl.ds(r0, tbk), :] += lax.dot_general(oh, y, _TN, preferred_element_type=F32)
        return 0

    lax.fori_loop(jlo, jhi, body, 0)


def _combine(r0, pos, y):
    ne, b, cap, d = y.shape
    t = pos.shape[2]
    tbk = TOKEN_BLOCK
    nblk = t // tbk
    rt = min(cap, 256)
    dh = d // 2
    pos4 = pos.reshape(b, ne, nblk, tbk)
    grid_spec = pltpu.PrefetchScalarGridSpec(
        num_scalar_prefetch=1,
        grid=(b, 2, ne, cap // rt),
        in_specs=[pl.BlockSpec((1, 1, nblk, tbk), lambda bb, c, e, i, r: (bb, e, 0, 0)),
                  pl.BlockSpec((1, 1, rt, dh), lambda bb, c, e, i, r: (e, bb, i, c))],
        out_specs=pl.BlockSpec((1, t, dh), lambda bb, c, e, i, r: (bb, 0, c)),
    )
    return pl.pallas_call(
        functools.partial(_combine_kernel, rt=rt, nblk=nblk, ne=ne),
        grid_spec=grid_spec,
        out_shape=jax.ShapeDtypeStruct((b, t, d), F32),
        compiler_params=_params(("arbitrary", "arbitrary", "arbitrary", "arbitrary")),
        name="moe_combine",
    )(r0.reshape(b * ne, LANES), pos4, y)


def _add_cat_kernel(x_ref, ml_ref, mc_ref, o_ref, *, ncb):
    i = pl.program_id(1)

    @pl.when(i < ncb)
    def _():
        o_ref[0] = x_ref[0] + mc_ref[0]

    @pl.when(i >= ncb)
    def _():
        o_ref[0] = x_ref[0] + ml_ref[0]


def _add_cat(x_cat, moe_lat, moe_ctx):
    b, r, d = x_cat.shape
    n_ctx = moe_ctx.shape[1]
    tm = TOKEN_BLOCK
    ncb = n_ctx // tm
    return pl.pallas_call(
        functools.partial(_add_cat_kernel, ncb=ncb),
        grid=(b, r // tm),
        in_specs=[pl.BlockSpec((1, tm, d), lambda bb, i: (bb, i, 0)),
                  pl.BlockSpec((1, tm, d), lambda bb, i: (bb, jnp.maximum(i - ncb, 0), 0)),
                  pl.BlockSpec((1, tm, d), lambda bb, i: (bb, jnp.minimum(i, ncb - 1), 0))],
        out_specs=pl.BlockSpec((1, tm, d), lambda bb, i: (bb, i, 0)),
        out_shape=jax.ShapeDtypeStruct((b, r, d), F32),
        compiler_params=_params(("arbitrary", "arbitrary")),
        name="moe_residual",
    )(x_cat, moe_lat, moe_ctx)


def _final_kernel(x_ref, m_ref, g_ref, o_ref):
    x = x_ref[0] + m_ref[0]
    ms = jnp.mean(x * x, axis=-1, keepdims=True)
    o_ref[0] = x * lax.rsqrt(ms + NORM_EPS) * g_ref[...]


def _final(x, moe, g):
    b, t, d = x.shape
    tm = _pick(t, (512, 256, 128))
    return pl.pallas_call(
        _final_kernel,
        grid=(b, t // tm),
        in_specs=[pl.BlockSpec((1, tm, d), lambda bb, i: (bb, i, 0)),
                  pl.BlockSpec((1, tm, d), lambda bb, i: (bb, i, 0)),
                  pl.BlockSpec((1, d), lambda bb, i: (0, 0))],
        out_specs=pl.BlockSpec((1, tm, d), lambda bb, i: (bb, i, 0)),
        out_shape=jax.ShapeDtypeStruct((b, t, d), F32),
        compiler_params=_params(("arbitrary", "arbitrary")),
        name="final_norm",
    )(x, moe, g.reshape(1, d))


def _moe_route(aff, hb, gate_f, w_gate, w_up, w_down, cap):
    pos, gsel, r0 = _topk(aff, cap)
    xs, gs = _gather(r0, pos, gsel, hb, cap)
    y = _ffn(xs, gs, gate_f, w_gate, w_up, w_down)
    return _combine(r0, pos, y)


def _rope_tables(t, n_ctx):
    rows = t // GRID_W
    row = jnp.repeat(jnp.arange(rows, dtype=F32), GRID_W)
    col = jnp.tile(jnp.arange(GRID_W, dtype=F32), rows)
    pairs = HEAD_W // 8
    freq = ROPE_BASE ** (-jnp.arange(pairs, dtype=F32) / pairs)
    ang = jnp.concatenate([row[:, None] * freq, col[:, None] * freq], axis=-1)
    cos, sin = jnp.cos(ang), jnp.sin(ang)
    cos = jnp.tile(cos, (1, 4))
    sin = jnp.tile(jnp.concatenate([-sin, sin], axis=-1), (1, 2))
    cos = jnp.concatenate([jnp.ones((n_ctx, HEAD_W), F32), cos], axis=0)
    sin = jnp.concatenate([jnp.zeros((n_ctx, HEAD_W), F32), sin], axis=0)
    qs = (HEAD_W // 2) ** -0.5
    return jnp.stack([cos * qs, cos]), jnp.stack([sin * qs, sin])


def kernel(x, c, ctx, c_ctx, ada_w, ada_b, norm_mix, norm_ffn, norm_final, hgrn_w_in, hgrn_lb_logits, hgrn_norm, hgrn_w_out, diff_w_in, diff_lambda, diff_subln, diff_w_out, moe_router, moe_w_gate, moe_w_up, moe_w_down):
    b, t, d = x.shape
    n_ctx = ctx.shape[1]
    depth = ada_w.shape[0]
    ne = moe_router.shape[2]
    assert depth == 2 and d == N_HEADS * HEAD_W

    cvec = jnp.concatenate([c, c_ctx[None, :], jnp.zeros((8 - b - 1, d), F32)], axis=0)
    mod = _ada(cvec, ada_w, ada_b)
    lower_bounds = jnp.cumsum(jax.nn.softmax(hgrn_lb_logits.astype(F32), axis=0), axis=0)

    def kinds(layer, k, plus_one=False):
        m = mod[layer, :, k * d:(k + 1) * d]
        v = jnp.stack([jnp.broadcast_to(m[b], (b, d)), m[:b]], axis=1)
        return 1.0 + v if plus_one else v

    x_cat = jnp.concatenate([ctx, x], axis=1)

    proj = _inproj(x_cat, norm_mix[0], kinds(0, 1, True), kinds(0, 0), hgrn_w_in[0].astype(BF16), n_ctx, F32)
    o2 = _gla(proj, lower_bounds[0], n_ctx)
    x_cat = _hgrn_out(o2, proj, hgrn_norm[0], hgrn_w_out[0].astype(BF16), x_cat, kinds(0, 2), n_ctx)
    hb, aff = _moe_pre(x_cat, norm_ffn[0], kinds(0, 4, True), kinds(0, 3), moe_router[0], n_ctx)
    gate_f = kinds(0, 5)
    moe_lat = _moe_route(aff[:, :, n_ctx:], hb[:, n_ctx:], gate_f[:, 1:2], moe_w_gate[0], moe_w_up[0],
                         moe_w_down[0], 2 * t // ne)
    moe_ctx = _moe_route(aff[:, :, :n_ctx], hb[:, :n_ctx], gate_f[:, 0:1], moe_w_gate[0], moe_w_up[0],
                         moe_w_down[0], 2 * n_ctx // ne)
    x_cat = _add_cat(x_cat, moe_lat, moe_ctx)

    lam_init = 0.8 - 0.6 * math.exp(-0.3 * 1)
    qkv = _inproj(x_cat, norm_mix[1], kinds(1, 1, True), kinds(1, 0), diff_w_in[0].astype(BF16), n_ctx, BF16,
                  rope_tabs=_rope_tables(t, n_ctx))
    att = _diff_attn(qkv, diff_lambda[0], diff_subln[0], n_ctx, lam_init)
    x_lat = _outproj(att, diff_w_out[0].astype(BF16), x_cat, kinds(1, 2)[:, 1:2], n_ctx)
    hb, aff = _moe_pre(x_lat, norm_ffn[1], kinds(1, 4, True), kinds(1, 3), moe_router[1], 0)
    moe_lat = _moe_route(aff, hb, kinds(1, 5)[:, 1:2], moe_w_gate[1], moe_w_up[1], moe_w_down[1], 2 * t // ne)
    return _final(x_lat, moe_lat, norm_final)
```

```python
import functools
import math

import jax
import jax.numpy as jnp
import numpy as np
from jax import lax
from jax.experimental import pallas as pl
from jax.experimental.pallas import tpu as pltpu

F32 = jnp.float32
BF16 = jnp.bfloat16
I32 = jnp.int32

NORM_EPS = 1e-6
LANES = 128
HEAD_W = 128
N_HEADS = 8
GLA_CHUNK = 64
GLA_LEVELS = (32, 16, 8, 4, 2, 1)
ROPE_BASE = 10000.0
GRID_W = 64
TOKEN_BLOCK = 256
VMEM_LIMIT = 56 * 1024 * 1024

_NT = (((1,), (1,)), ((), ()))
_TN = (((0,), (0,)), ((), ()))


def _pick(n, cands):
    for c in cands:
        if n % c == 0:
            return c
    raise ValueError(f"no tile for {n} in {cands}")


def _params(sem):
    return pltpu.CompilerParams(dimension_semantics=sem, vmem_limit_bytes=VMEM_LIMIT)


def _sigmoid(x):
    return 1.0 / (1.0 + jnp.exp(-x))


def _norm_mod(x, g, scale2, shift2, row0, n_ctx):
    ms = jnp.mean(x * x, axis=-1, keepdims=True)
    y = x * lax.rsqrt(ms + NORM_EPS) * g
    rows = row0 + lax.broadcasted_iota(I32, (x.shape[0], 1), 0)
    is_ctx = rows < n_ctx
    sc = jnp.where(is_ctx, scale2[0:1], scale2[1:2])
    sh = jnp.where(is_ctx, shift2[0:1], shift2[1:2])
    return y * sc + sh


def _ada_kernel(c_ref, w_ref, b_ref, o_ref):
    c = c_ref[...]
    s = c * _sigmoid(c)
    o_ref[0] = jnp.dot(s, w_ref[0], precision=lax.Precision.HIGHEST,
                       preferred_element_type=F32) + b_ref[0]


def _ada(cvec, ada_w, ada_b):
    depth, d, n = ada_w.shape
    rows = cvec.shape[0]
    tn = _pick(n, (1024, 512, 256, 128))
    return pl.pallas_call(
        _ada_kernel,
        grid=(depth, n // tn),
        in_specs=[pl.BlockSpec((rows, d), lambda l, j: (0, 0)),
                  pl.BlockSpec((1, d, tn), lambda l, j: (l, 0, j)),
                  pl.BlockSpec((1, 1, tn), lambda l, j: (l, 0, j))],
        out_specs=pl.BlockSpec((1, rows, tn), lambda l, j: (l, 0, j)),
        out_shape=jax.ShapeDtypeStruct((depth, rows, n), F32),
        compiler_params=_params(("arbitrary", "arbitrary")),
        name="adaln",
    )(cvec, ada_w, ada_b.reshape(depth, 1, n))


def _inproj_kernel(x_ref, g_ref, sc_ref, sh_ref, w_ref, *rest, tm, n_ctx, rope):
    if rope:
        cos_ref, sin_ref, o_ref, h_ref = rest
    else:
        o_ref, h_ref = rest
    i = pl.program_id(1)
    n = pl.program_id(2)

    @pl.when(n == 0)
    def _():
        h = _norm_mod(x_ref[0], g_ref[...], sc_ref[0], sh_ref[0], i * tm, n_ctx)
        h_ref[...] = h.astype(BF16)

    acc = jnp.dot(h_ref[...], w_ref[...], preferred_element_type=F32)
    if not rope:
        o_ref[0] = acc.astype(o_ref.dtype)
    else:
        @pl.when(n < 2)
        def _():
            cos = cos_ref[0]
            sin = sin_ref[0]
            lane = lax.broadcasted_iota(I32, (1, HEAD_W), 1)
            first = (lane % 64) < 32
            for hd in range(acc.shape[1] // HEAD_W):
                a = acc[:, hd * HEAD_W:(hd + 1) * HEAD_W]
                rot = jnp.where(first, pltpu.roll(a, HEAD_W - 32, 1), pltpu.roll(a, 32, 1))
                o_ref[0, :, hd * HEAD_W:(hd + 1) * HEAD_W] = (a * cos + rot * sin).astype(o_ref.dtype)

        @pl.when(n >= 2)
        def _():
            col = lax.broadcasted_iota(I32, (1, acc.shape[1]), 1)
            o_ref[0] = jnp.where(col % (2 * HEAD_W) >= HEAD_W, 1.0, acc).astype(o_ref.dtype)


def _inproj(x, g, scale2, shift2, w_bf16, n_ctx, out_dtype, rope_tabs=None):
    b, r, d = x.shape
    n = w_bf16.shape[1]
    tm = _pick(r, (768, 512, 384, 256, 128))
    tn = 1024
    rope = rope_tabs is not None
    in_specs = [pl.BlockSpec((1, tm, d), lambda bb, i, j: (bb, i, 0)),
                pl.BlockSpec((1, d), lambda bb, i, j: (0, 0)),
                pl.BlockSpec((1, 2, d), lambda bb, i, j: (bb, 0, 0)),
                pl.BlockSpec((1, 2, d), lambda bb, i, j: (bb, 0, 0)),
                pl.BlockSpec((d, tn), lambda bb, i, j: (0, j))]
    args = [x, g.reshape(1, d), scale2, shift2, w_bf16]
    if rope:
        cos_t, sin_t = rope_tabs
        spec = pl.BlockSpec((1, tm, HEAD_W), lambda bb, i, j: (jnp.minimum(j, 1), i, 0))
        in_specs += [spec, spec]
        args += [cos_t, sin_t]
    return pl.pallas_call(
        functools.partial(_inproj_kernel, tm=tm, n_ctx=n_ctx, rope=rope),
        grid=(b, r // tm, n // tn),
        in_specs=in_specs,
        out_specs=pl.BlockSpec((1, tm, tn), lambda bb, i, j: (bb, i, j)),
        out_shape=jax.ShapeDtypeStruct((b, r, n), out_dtype),
        scratch_shapes=[pltpu.VMEM((tm, d), BF16)],
        compiler_params=_params(("arbitrary", "arbitrary", "arbitrary")),
        name="inproj_rope" if rope else "inproj",
    )(*args)


def _gla_consts():
    c = GLA_CHUNK
    nl = len(GLA_LEVELS)
    t = np.arange(c)
    dall = np.zeros((2, (nl + 2) * c + 8, c), np.float32)
    rowsel = np.zeros((2, nl, c, HEAD_W), np.float32)
    masks = np.zeros((2, nl + 1, c, c), np.float32)
    for d in range(2):
        p = t if d == 0 else c - 1 - t
        for li, s in enumerate(GLA_LEVELS):
            blk = p // (2 * s)
            second = (p // s) % 2 == 1
            href = blk * 2 * s + s - 1
            pr = p[None, :]
            inc_q = (href[:, None] < pr) & (pr <= p[:, None])
            inc_k = (p[:, None] < pr) & (pr <= href[:, None])
            dall[d, li * c:(li + 1) * c] = np.where(second[:, None], inc_q, inc_k)
            rowsel[d, li] = second[:, None]
            masks[d, li] = (blk[:, None] == blk[None, :]) & second[:, None] & (~second)[None, :]
        dall[d, nl * c:(nl + 1) * c] = p[None, :] <= p[:, None]
        dall[d, (nl + 1) * c:(nl + 2) * c] = p[None, :] > p[:, None]
        dall[d, (nl + 2) * c:] = 1.0
        masks[d, nl] = np.eye(c)
    return dall, rowsel, masks


def _gla_kernel(q_ref, f_ref, v_ref, lb_ref, dall_ref, rowsel_ref, masks_ref, o_ref, st_ref, *, tb):
    c = GLA_CHUNK
    nl = len(GLA_LEVELS)
    nc = tb // c
    d = pl.program_id(1)
    n = pl.program_id(3)

    @pl.when(n == 0)
    def _():
        st_ref[...] = jnp.zeros_like(st_ref)

    lb = lb_ref[0]
    dall = dall_ref[0]
    for ci in range(nc):
        cc = jnp.where(d == 0, ci, nc - 1 - ci)
        r0 = pl.multiple_of(cc * c, c)
        q = q_ref[0, pl.ds(r0, c), :]
        qf = q * _sigmoid(q)
        f = lb + (1.0 - lb) * _sigmoid(f_ref[0, pl.ds(r0, c), :])
        lf = jnp.log(f)
        kk = 1.0 - f
        v = v_ref[0, pl.ds(r0, c), :]
        vb = v.astype(BF16)

        hi = lf.astype(BF16)
        r1 = lf - hi.astype(F32)
        mid = r1.astype(BF16)
        lo = (r1 - mid.astype(F32)).astype(BF16)
        lf3 = jnp.concatenate([hi, mid, lo], axis=1)
        ex3 = jnp.dot(dall, lf3, preferred_element_type=F32)
        ex = ex3[:, :HEAD_W] + ex3[:, HEAD_W:2 * HEAD_W] + ex3[:, 2 * HEAD_W:]
        e = jnp.exp(ex)

        st = st_ref[...]
        qd = (qf * e[nl * c:(nl + 1) * c]).astype(BF16)
        o = lax.dot_general(qd, st.astype(BF16), _NT, preferred_element_type=F32)

        qb = qf.astype(BF16)
        kb = kk.astype(BF16)
        a = masks_ref[0, nl] * lax.dot_general(qb, kb, _NT, preferred_element_type=F32)
        for li in range(nl):
            x = jnp.where(rowsel_ref[0, li] > 0, qf, kk) * e[li * c:(li + 1) * c]
            xb = x.astype(BF16)
            a = a + masks_ref[0, li] * lax.dot_general(xb, xb, _NT, preferred_element_type=F32)
        o = o + jnp.dot(a.astype(BF16), vb, preferred_element_type=F32)
        o_ref[0, 0, pl.ds(r0, c), :] = o

        kdec = (kk * e[(nl + 1) * c:(nl + 2) * c]).astype(BF16)
        upd = lax.dot_general(vb, kdec, _TN, preferred_element_type=F32)
        st_ref[...] = st * e[(nl + 2) * c:(nl + 2) * c + 1] + upd


def _gla(proj, lb, n_ctx):
    b, r, _ = proj.shape
    tb = TOKEN_BLOCK
    assert n_ctx % tb == 0 and r % tb == 0
    nblk = r // tb
    ncb = n_ctx // tb
    dall, rowsel, masks = _gla_consts()
    nrow = dall.shape[1]
    nl = len(GLA_LEVELS)
    c = GLA_CHUNK

    def blk(d, n):
        bwd = jnp.where(n < ncb, ncb - 1 - n, nblk - 1 - (n - ncb))
        return jnp.where(d == 0, n, bwd)

    return pl.pallas_call(
        functools.partial(_gla_kernel, tb=tb),
        grid=(b, 2, N_HEADS, nblk),
        in_specs=[pl.BlockSpec((1, tb, HEAD_W), lambda bb, d, h, n: (bb, blk(d, n), h)),
                  pl.BlockSpec((1, tb, HEAD_W), lambda bb, d, h, n: (bb, blk(d, n), N_HEADS * (1 + d) + h)),
                  pl.BlockSpec((1, tb, HEAD_W), lambda bb, d, h, n: (bb, blk(d, n), 3 * N_HEADS + h)),
                  pl.BlockSpec((1, 1, HEAD_W), lambda bb, d, h, n: (h, 0, 0)),
                  pl.BlockSpec((1, nrow, c), lambda bb, d, h, n: (d, 0, 0)),
                  pl.BlockSpec((1, nl, c, HEAD_W), lambda bb, d, h, n: (d, 0, 0, 0)),
                  pl.BlockSpec((1, nl + 1, c, c), lambda bb, d, h, n: (d, 0, 0, 0))],
        out_specs=pl.BlockSpec((1, 1, tb, HEAD_W), lambda bb, d, h, n: (d, bb, blk(d, n), h)),
        out_shape=jax.ShapeDtypeStruct((2, b, r, N_HEADS * HEAD_W), F32),
        scratch_shapes=[pltpu.VMEM((HEAD_W, HEAD_W), F32)],
        compiler_params=_params(("arbitrary", "arbitrary", "arbitrary", "arbitrary")),
        name="gla",
    )(proj, proj, proj, lb.reshape(N_HEADS, 1, HEAD_W), jnp.asarray(dall, BF16),
      jnp.asarray(rowsel), jnp.asarray(masks))


def _hgrn_out_kernel(of_ref, ob_ref, g_ref, ng_ref, w_ref, x_ref, gate_ref, o_ref, y_ref, *, tm, n_ctx):
    i = pl.program_id(1)
    o = of_ref[0, 0] + ob_ref[0, 0]
    for h in range(N_HEADS):
        sl = slice(h * HEAD_W, (h + 1) * HEAD_W)
        oh = o[:, sl]
        ms = jnp.mean(oh * oh, axis=-1, keepdims=True)
        g = g_ref[0, :, sl]
        y_ref[:, sl] = (oh * lax.rsqrt(ms + NORM_EPS) * ng_ref[:, sl] * (g * _sigmoid(g))).astype(BF16)
    y = jnp.dot(y_ref[...], w_ref[...], preferred_element_type=F32)
    rows = i * tm + lax.broadcasted_iota(I32, (tm, 1), 0)
    gate = jnp.where(rows < n_ctx, gate_ref[0, 0:1], gate_ref[0, 1:2])
    o_ref[0] = x_ref[0] + gate * y


def _hgrn_out(o2, proj, norm_g, w_bf16, x, gate2, n_ctx):
    b, r, d = x.shape
    tm = _pick(r, (384, 256, 128))
    return pl.pallas_call(
        functools.partial(_hgrn_out_kernel, tm=tm, n_ctx=n_ctx),
        grid=(b, r // tm),
        in_specs=[pl.BlockSpec((1, 1, tm, d), lambda bb, i: (0, bb, i, 0)),
                  pl.BlockSpec((1, 1, tm, d), lambda bb, i: (1, bb, i, 0)),
                  pl.BlockSpec((1, tm, d), lambda bb, i: (bb, i, 4)),
                  pl.BlockSpec((1, d), lambda bb, i: (0, 0)),
                  pl.BlockSpec((d, d), lambda bb, i: (0, 0)),
                  pl.BlockSpec((1, tm, d), lambda bb, i: (bb, i, 0)),
                  pl.BlockSpec((1, 2, d), lambda bb, i: (bb, 0, 0))],
        out_specs=pl.BlockSpec((1, tm, d), lambda bb, i: (bb, i, 0)),
        out_shape=jax.ShapeDtypeStruct((b, r, d), F32),
        scratch_shapes=[pltpu.VMEM((tm, d), BF16)],
        compiler_params=_params(("arbitrary", "arbitrary")),
        name="hgrn_out",
    )(o2, o2, proj, jnp.tile(norm_g, N_HEADS).reshape(1, d), w_bf16, x, gate2)


def _attn_kernel(q_ref, k_ref, v_ref, lam_ref, g_ref, o_ref, *, tq, tk, lam_init):
    nk = k_ref.shape[1] // tk
    q = q_ref[0]
    lane = lax.broadcasted_iota(I32, (1, HEAD_W), 1)
    zero = jnp.zeros_like(q)
    qs = jnp.concatenate([jnp.where(lane < 64, q, zero), jnp.where(lane >= 64, q, zero)], axis=0)

    m = jnp.full((2 * tq, LANES), -jnp.inf, F32)
    acc = jnp.zeros((2 * tq, 2 * HEAD_W), F32)
    for ki in range(nk):
        k = k_ref[0, ki * tk:(ki + 1) * tk, :]
        v1 = v_ref[0, ki * tk:(ki + 1) * tk, :]
        s = lax.dot_general(qs, k, _NT, preferred_element_type=F32)
        m_new = jnp.maximum(m, jnp.max(s, axis=-1, keepdims=True))
        p = jnp.exp2(s - jnp.tile(m_new, (1, tk // LANES)))
        alpha = jnp.exp2(m - m_new)
        acc = jnp.tile(alpha, (1, 2)) * acc + jnp.dot(p.astype(BF16), v1, preferred_element_type=F32)
        m = m_new
    out = acc[:, :HEAD_W] / acc[:, HEAD_W:]

    lv = lam_ref[...]
    s01 = jnp.sum(lv[0:1] * lv[1:2], axis=-1, keepdims=True)
    s23 = jnp.sum(lv[2:3] * lv[3:4], axis=-1, keepdims=True)
    lam = jnp.exp(s01) - jnp.exp(s23) + lam_init
    o = out[:tq] - lam * out[tq:]
    ms = jnp.mean(o * o, axis=-1, keepdims=True)
    o_ref[0] = (o * lax.rsqrt(ms + NORM_EPS) * g_ref[...] * (1.0 - lam_init)).astype(o_ref.dtype)


def _diff_attn(qkv, lam_vecs, subln_g, n_ctx, lam_init):
    b, r, _ = qkv.shape
    t = r - n_ctx
    tq = TOKEN_BLOCK
    assert n_ctx % tq == 0 and t % tq == 0
    tk = _pick(r, (2816, 1408, 1280, 768, 640, 512, 256))
    qoff = n_ctx // tq
    return pl.pallas_call(
        functools.partial(_attn_kernel, tq=tq, tk=tk, lam_init=lam_init),
        grid=(b, N_HEADS, t // tq),
        in_specs=[pl.BlockSpec((1, tq, HEAD_W), lambda bb, h, i: (bb, i + qoff, h)),
                  pl.BlockSpec((1, r, HEAD_W), lambda bb, h, i: (bb, 0, N_HEADS + h)),
                  pl.BlockSpec((1, r, 2 * HEAD_W), lambda bb, h, i: (bb, 0, N_HEADS + h)),
                  pl.BlockSpec(lam_vecs.shape, lambda bb, h, i: (0, 0)),
                  pl.BlockSpec((1, HEAD_W), lambda bb, h, i: (0, 0))],
        out_specs=pl.BlockSpec((1, tq, HEAD_W), lambda bb, h, i: (bb, i, h)),
        out_shape=jax.ShapeDtypeStruct((b, t, N_HEADS * HEAD_W), BF16),
        compiler_params=_params(("arbitrary", "arbitrary", "arbitrary")),
        name="diff_attn",
    )(qkv, qkv, qkv, lam_vecs.astype(F32), subln_g.reshape(1, HEAD_W).astype(F32))


def _outproj_kernel(a_ref, w_ref, x_ref, gate_ref, o_ref):
    y = jnp.dot(a_ref[0], w_ref[...], preferred_element_type=F32)
    o_ref[0] = x_ref[0] + gate_ref[0] * y


def _outproj(a, w_bf16, x_cat, gate, n_ctx):
    b, t, d = a.shape
    tm = _pick(math.gcd(t, n_ctx), (512, 256, 128))
    off = n_ctx // tm
    return pl.pallas_call(
        _outproj_kernel,
        grid=(b, t // tm),
        in_specs=[pl.BlockSpec((1, tm, d), lambda bb, i: (bb, i, 0)),
                  pl.BlockSpec((d, d), lambda bb, i: (0, 0)),
                  pl.BlockSpec((1, tm, d), lambda bb, i: (bb, i + off, 0)),
                  pl.BlockSpec((1, 1, d), lambda bb, i: (bb, 0, 0))],
        out_specs=pl.BlockSpec((1, tm, d), lambda bb, i: (bb, i, 0)),
        out_shape=jax.ShapeDtypeStruct((b, t, d), F32),
        compiler_params=_params(("arbitrary", "arbitrary")),
        name="outproj",
    )(a, w_bf16, x_cat, gate)


def _moe_pre_kernel(x_ref, g_ref, sc_ref, sh_ref, wr_ref, h_ref, aff_ref, *, tm, n_ctx):
    i = pl.program_id(1)
    h = _norm_mod(x_ref[0], g_ref[...], sc_ref[0], sh_ref[0], i * tm, n_ctx)
    h_ref[0] = h.astype(BF16)
    logits = lax.dot_general(wr_ref[...], h, _NT, precision=lax.Precision.HIGHEST,
                             preferred_element_type=F32)
    mx = jnp.max(logits, axis=0, keepdims=True)
    ex = jnp.exp(logits - mx)
    aff_ref[0] = ex / jnp.sum(ex, axis=0, keepdims=True)


def _moe_pre(x, g, scale2, shift2, w_router, n_ctx):
    b, r, d = x.shape
    ne = w_router.shape[1]
    tm = _pick(r, (768, 512, 384, 256, 128))
    return pl.pallas_call(
        functools.partial(_moe_pre_kernel, tm=tm, n_ctx=n_ctx),
        grid=(b, r // tm),
        in_specs=[pl.BlockSpec((1, tm, d), lambda bb, i: (bb, i, 0)),
                  pl.BlockSpec((1, d), lambda bb, i: (0, 0)),
                  pl.BlockSpec((1, 2, d), lambda bb, i: (bb, 0, 0)),
                  pl.BlockSpec((1, 2, d), lambda bb, i: (bb, 0, 0)),
                  pl.BlockSpec((ne, d), lambda bb, i: (0, 0))],
        out_specs=[pl.BlockSpec((1, tm, d), lambda bb, i: (bb, i, 0)),
                   pl.BlockSpec((1, ne, tm), lambda bb, i: (bb, 0, i))],
        out_shape=[jax.ShapeDtypeStruct((b, r, d), BF16),
                   jax.ShapeDtypeStruct((b, ne, r), F32)],
        compiler_params=_params(("arbitrary", "arbitrary")),
        name="moe_pre",
    )(x, g.reshape(1, d), scale2, shift2, w_router.T)


def _topk_kernel(aff_ref, tri_ref, pos_ref, gsel_ref, r0_ref, *, cap):
    ne, t = aff_ref.shape[1], aff_ref.shape[2]
    tbk = TOKEN_BLOCK
    nblk = t // tbk
    bits = pltpu.bitcast(aff_ref[0], I32)
    thr = jnp.zeros((ne, 1), I32)
    for bit in range(30, -1, -1):
        cand = thr | (1 << bit)
        cnt = jnp.sum(jnp.where(bits >= cand, 1.0, 0.0), axis=1, keepdims=True)
        thr = jnp.where(cnt >= cap, cand, thr)
    n_gt = jnp.sum(jnp.where(bits > thr, 1.0, 0.0), axis=1, keepdims=True)
    need = cap - n_gt
    tri = tri_ref[...]
    carry_eq = jnp.zeros((ne, 1), F32)
    carry_sel = jnp.zeros((ne, 1), F32)
    r0_ref[0] = jnp.full((ne, LANES), cap, I32)
    for j in range(nblk):
        sl = slice(j * tbk, (j + 1) * tbk)
        a = aff_ref[0, :, sl]
        bj = pltpu.bitcast(a, I32)
        eq = jnp.where(bj == thr, 1.0, 0.0)
        gt = jnp.where(bj > thr, 1.0, 0.0)
        pe = jnp.dot(eq.astype(BF16), tri, preferred_element_type=F32) + carry_eq
        sel = gt + eq * jnp.where(pe - eq < need, 1.0, 0.0)
        ps = jnp.dot(sel.astype(BF16), tri, preferred_element_type=F32) + carry_sel
        pos_ref[0, :, sl] = jnp.where(sel > 0, ps - 1.0, -1.0).astype(I32)
        gsel_ref[0, :, sl] = a * sel
        r0_ref[0, :, j:j + 1] = carry_sel.astype(I32)
        carry_eq = pe[:, tbk - 1:tbk]
        carry_sel = ps[:, tbk - 1:tbk]


def _topk(aff, cap):
    b, ne, t = aff.shape
    tbk = TOKEN_BLOCK
    assert t % tbk == 0 and t // tbk < LANES
    tri = jnp.asarray(np.triu(np.ones((tbk, tbk), np.float32)), BF16)
    return pl.pallas_call(
        functools.partial(_topk_kernel, cap=cap),
        grid=(b,),
        in_specs=[pl.BlockSpec((1, ne, t), lambda bb: (bb, 0, 0)),
                  pl.BlockSpec((tbk, tbk), lambda bb: (0, 0))],
        out_specs=[pl.BlockSpec((1, ne, t), lambda bb: (bb, 0, 0)),
                   pl.BlockSpec((1, ne, t), lambda bb: (bb, 0, 0)),
                   pl.BlockSpec((1, ne, LANES), lambda bb: (bb, 0, 0))],
        out_shape=[jax.ShapeDtypeStruct((b, ne, t), I32),
                   jax.ShapeDtypeStruct((b, ne, t), F32),
                   jax.ShapeDtypeStruct((b, ne, LANES), I32)],
        compiler_params=_params(("arbitrary",)),
        name="topk",
    )(aff, tri)


def _block_range(r0_ref, row, nblk, lo, hi):
    jlo = lax.fori_loop(0, nblk, lambda j, c: c + (r0_ref[row, j + 1] <= lo).astype(I32), jnp.int32(0))
    jhi = lax.fori_loop(0, nblk, lambda j, c: c + (r0_ref[row, j] < hi).astype(I32), jnp.int32(0))
    return jlo, jhi


def _gather_kernel(r0_ref, pos_ref, g_ref, h_ref, xs_ref, gs_ref, acc_ref, gacc_ref, *, rt, nblk, ne):
    b, e, i = pl.program_id(0), pl.program_id(1), pl.program_id(2)
    tbk = TOKEN_BLOCK
    lo = i * rt
    jlo, jhi = _block_range(r0_ref, b * ne + e, nblk, lo, lo + rt)
    acc_ref[...] = jnp.zeros_like(acc_ref)
    gacc_ref[...] = jnp.zeros_like(gacc_ref)
    rank = lo + lax.broadcasted_iota(I32, (rt, tbk), 0)

    def body(j, _):
        hit = pos_ref[0, 0, pl.ds(j, 1), :] == rank
        oh = jnp.where(hit, 1.0, 0.0).astype(BF16)
        r0 = pl.multiple_of(j * tbk, tbk)
        acc_ref[...] += jnp.dot(oh, h_ref[0, pl.ds(r0, tbk), :], preferred_element_type=F32)
        gacc_ref[...] += jnp.sum(jnp.where(hit, g_ref[0, 0, pl.ds(j, 1), :], 0.0), axis=1, keepdims=True)
        return 0

    lax.fori_loop(jlo, jhi, body, 0)
    xs_ref[0, 0] = acc_ref[...].astype(BF16)
    gs_ref[0, 0] = jnp.broadcast_to(gacc_ref[...], (rt, LANES))


def _gather(r0, pos, gate, hb, cap):
    b, ne, t = pos.shape
    d = hb.shape[2]
    tbk = TOKEN_BLOCK
    nblk = t // tbk
    rt = min(cap, 256)
    pos4 = pos.reshape(b, ne, nblk, tbk)
    g4 = gate.reshape(b, ne, nblk, tbk)
    grid_spec = pltpu.PrefetchScalarGridSpec(
        num_scalar_prefetch=1,
        grid=(b, ne, cap // rt),
        in_specs=[pl.BlockSpec((1, 1, nblk, tbk), lambda bb, e, i, r: (bb, e, 0, 0)),
                  pl.BlockSpec((1, 1, nblk, tbk), lambda bb, e, i, r: (bb, e, 0, 0)),
                  pl.BlockSpec((1, t, d), lambda bb, e, i, r: (bb, 0, 0))],
        out_specs=[pl.BlockSpec((1, 1, rt, d), lambda bb, e, i, r: (e, bb, i, 0)),
                   pl.BlockSpec((1, 1, rt, LANES), lambda bb, e, i, r: (e, bb, i, 0))],
        scratch_shapes=[pltpu.VMEM((rt, d), F32), pltpu.VMEM((rt, 1), F32)],
    )
    return pl.pallas_call(
        functools.partial(_gather_kernel, rt=rt, nblk=nblk, ne=ne),
        grid_spec=grid_spec,
        out_shape=[jax.ShapeDtypeStruct((ne, b, cap, d), BF16),
                   jax.ShapeDtypeStruct((ne, b, cap, LANES), F32)],
        compiler_params=_params(("arbitrary", "arbitrary", "arbitrary")),
        name="moe_gather",
    )(r0.reshape(b * ne, LANES), pos4, g4, hb)


def _ffn_kernel(xs_ref, gs_ref, gt_ref, wg_ref, wu_ref, wd_ref, y_ref, acc_ref, *, nb, cap, rc):
    f = pl.program_id(1)
    nf = pl.num_programs(1)

    @pl.when(f == 0)
    def _():
        acc_ref[...] = jnp.zeros_like(acc_ref)

    wg = wg_ref[0, 0].astype(BF16)
    wu = wu_ref[0, 0].astype(BF16)
    wd = wd_ref[0, 0].astype(BF16)
    rows = nb * cap
    for c0 in range(0, rows, rc):
        x = xs_ref[0, c0:c0 + rc, :]
        a = jnp.dot(x, wg, preferred_element_type=F32)
        u = jnp.dot(x, wu, preferred_element_type=F32)
        hm = (a * _sigmoid(a) * u).astype(BF16)
        acc_ref[c0:c0 + rc, :] += jnp.dot(hm, wd, preferred_element_type=F32)

    @pl.when(f == nf - 1)
    def _():
        for bb in range(nb):
            sl = slice(bb * cap, (bb + 1) * cap)
            y_ref[0, sl, :] = (acc_ref[sl, :] * gs_ref[0, sl, 0:1] * gt_ref[bb]).astype(BF16)


def _ffn(xs, gs, gate_f, w_gate, w_up, w_down, layer):
    ne, b, cap, d = xs.shape
    fdim = w_gate.shape[3]
    tf = _pick(fdim, (256, 128))
    rows = b * cap
    rc = _pick(rows, (512, 256, 128, 64))
    y = pl.pallas_call(
        functools.partial(_ffn_kernel, nb=b, cap=cap, rc=rc),
        grid=(ne, fdim // tf),
        in_specs=[pl.BlockSpec((1, rows, d), lambda e, f: (e, 0, 0)),
                  pl.BlockSpec((1, rows, LANES), lambda e, f: (e, 0, 0)),
                  pl.BlockSpec((b, 1, d), lambda e, f: (0, 0, 0)),
                  pl.BlockSpec((1, 1, d, tf), lambda e, f: (layer, e, 0, f)),
                  pl.BlockSpec((1, 1, d, tf), lambda e, f: (layer, e, 0, f)),
                  pl.BlockSpec((1, 1, tf, d), lambda e, f: (layer, e, f, 0))],
        out_specs=pl.BlockSpec((1, rows, d), lambda e, f: (e, 0, 0)),
        out_shape=jax.ShapeDtypeStruct((ne, rows, d), BF16),
        scratch_shapes=[pltpu.VMEM((rows, d), F32)],
        compiler_params=_params(("arbitrary", "arbitrary")),
        name="moe_ffn",
    )(xs.reshape(ne, rows, d), gs.reshape(ne, rows, LANES), gate_f, w_gate, w_up, w_down)
    return y.reshape(ne, b, cap, d)


def _combine_kernel(r0_ref, pos_ref, y_ref, o_ref, *, rt, nblk, ne):
    b, e, i = pl.program_id(0), pl.program_id(2), pl.program_id(3)
    tbk = TOKEN_BLOCK
    lo = i * rt

    @pl.when((e == 0) & (i == 0))
    def _():
        o_ref[...] = jnp.zeros_like(o_ref)

    jlo, jhi = _block_range(r0_ref, b * ne + e, nblk, lo, lo + rt)
    rank = lo + lax.broadcasted_iota(I32, (rt, tbk), 0)
    y = y_ref[0, 0]

    def body(j, _):
        hit = pos_ref[0, 0, pl.ds(j, 1), :] == rank
        oh = jnp.where(hit, 1.0, 0.0).astype(BF16)
        r0 = pl.multiple_of(j * tbk, tbk)
        o_ref[0, pl.ds(r0, tbk), :] += lax.dot_general(oh, y, _TN, preferred_element_type=F32)
        return 0

    lax.fori_loop(jlo, jhi, body, 0)


def _combine(r0, pos, y):
    ne, b, cap, d = y.shape
    t = pos.shape[2]
    tbk = TOKEN_BLOCK
    nblk = t // tbk
    rt = min(cap, 256)
    dh = d // 2
    pos4 = pos.reshape(b, ne, nblk, tbk)
    grid_spec = pltpu.PrefetchScalarGridSpec(
        num_scalar_prefetch=1,
        grid=(b, 2, ne, cap // rt),
        in_specs=[pl.BlockSpec((1, 1, nblk, tbk), lambda bb, c, e, i, r: (bb, e, 0, 0)),
                  pl.BlockSpec((1, 1, rt, dh), lambda bb, c, e, i, r: (e, bb, i, c))],
        out_specs=pl.BlockSpec((1, t, dh), lambda bb, c, e, i, r: (bb, 0, c)),
    )
    return pl.pallas_call(
        functools.partial(_combine_kernel, rt=rt, nblk=nblk, ne=ne),
        grid_spec=grid_spec,
        out_shape=jax.ShapeDtypeStruct((b, t, d), F32),
        compiler_params=_params(("arbitrary", "arbitrary", "arbitrary", "arbitrary")),
        name="moe_combine",
    )(r0.reshape(b * ne, LANES), pos4, y)


def _add_cat_kernel(x_ref, ml_ref, mc_ref, o_ref, *, ncb):
    i = pl.program_id(1)

    @pl.when(i < ncb)
    def _():
        o_ref[0] = x_ref[0] + mc_ref[0]

    @pl.when(i >= ncb)
    def _():
        o_ref[0] = x_ref[0] + ml_ref[0]


def _add_cat(x_cat, moe_lat, moe_ctx):
    b, r, d = x_cat.shape
    n_ctx = moe_ctx.shape[1]
    tm = TOKEN_BLOCK
    ncb = n_ctx // tm
    return pl.pallas_call(
        functools.partial(_add_cat_kernel, ncb=ncb),
        grid=(b, r // tm),
        in_specs=[pl.BlockSpec((1, tm, d), lambda bb, i: (bb, i, 0)),
                  pl.BlockSpec((1, tm, d), lambda bb, i: (bb, jnp.maximum(i - ncb, 0), 0)),
                  pl.BlockSpec((1, tm, d), lambda bb, i: (bb, jnp.minimum(i, ncb - 1), 0))],
        out_specs=pl.BlockSpec((1, tm, d), lambda bb, i: (bb, i, 0)),
        out_shape=jax.ShapeDtypeStruct((b, r, d), F32),
        compiler_params=_params(("arbitrary", "arbitrary")),
        name="moe_residual",
    )(x_cat, moe_lat, moe_ctx)


def _final_kernel(x_ref, m_ref, g_ref, o_ref):
    x = x_ref[0] + m_ref[0]
    ms = jnp.mean(x * x, axis=-1, keepdims=True)
    o_ref[0] = x * lax.rsqrt(ms + NORM_EPS) * g_ref[...]


def _final(x, moe, g):
    b, t, d = x.shape
    tm = _pick(t, (512, 256, 128))
    return pl.pallas_call(
        _final_kernel,
        grid=(b, t // tm),
        in_specs=[pl.BlockSpec((1, tm, d), lambda bb, i: (bb, i, 0)),
                  pl.BlockSpec((1, tm, d), lambda bb, i: (bb, i, 0)),
                  pl.BlockSpec((1, d), lambda bb, i: (0, 0))],
        out_specs=pl.BlockSpec((1, tm, d), lambda bb, i: (bb, i, 0)),
        out_shape=jax.ShapeDtypeStruct((b, t, d), F32),
        compiler_params=_params(("arbitrary", "arbitrary")),
        name="final_norm",
    )(x, moe, g.reshape(1, d))


def _moe_route(aff, hb, gate_f, w_gate, w_up, w_down, layer, cap):
    pos, gsel, r0 = _topk(aff, cap)
    xs, gs = _gather(r0, pos, gsel, hb, cap)
    y = _ffn(xs, gs, gate_f, w_gate, w_up, w_down, layer)
    return _combine(r0, pos, y)


def _rope_tables(t, n_ctx):
    rows = t // GRID_W
    row = jnp.repeat(jnp.arange(rows, dtype=F32), GRID_W)
    col = jnp.tile(jnp.arange(GRID_W, dtype=F32), rows)
    pairs = HEAD_W // 8
    freq = ROPE_BASE ** (-jnp.arange(pairs, dtype=F32) / pairs)
    ang = jnp.concatenate([row[:, None] * freq, col[:, None] * freq], axis=-1)
    cos, sin = jnp.cos(ang), jnp.sin(ang)
    cos = jnp.tile(cos, (1, 4))
    sin = jnp.tile(jnp.concatenate([-sin, sin], axis=-1), (1, 2))
    cos = jnp.concatenate([jnp.ones((n_ctx, HEAD_W), F32), cos], axis=0)
    sin = jnp.concatenate([jnp.zeros((n_ctx, HEAD_W), F32), sin], axis=0)
    qs = (HEAD_W // 2) ** -0.5 * math.log2(math.e)
    return jnp.stack([cos * qs, cos]), jnp.stack([sin * qs, sin])


def _diff_w_ext(w):
    d = w.shape[0]
    wv = w[:, 2 * d:].reshape(d, N_HEADS, HEAD_W)
    wv = jnp.concatenate([wv, jnp.zeros_like(wv)], axis=-1).reshape(d, 2 * d)
    return jnp.concatenate([w[:, :2 * d], wv], axis=1).astype(BF16)


def kernel(x, c, ctx, c_ctx, ada_w, ada_b, norm_mix, norm_ffn, norm_final, hgrn_w_in, hgrn_lb_logits, hgrn_norm, hgrn_w_out, diff_w_in, diff_lambda, diff_subln, diff_w_out, moe_router, moe_w_gate, moe_w_up, moe_w_down):
    b, t, d = x.shape
    n_ctx = ctx.shape[1]
    depth = ada_w.shape[0]
    ne = moe_router.shape[2]
    assert depth == 2 and d == N_HEADS * HEAD_W

    cvec = jnp.concatenate([c, c_ctx[None, :], jnp.zeros((8 - b - 1, d), F32)], axis=0)
    mod = _ada(cvec, ada_w, ada_b)
    lower_bounds = jnp.cumsum(jax.nn.softmax(hgrn_lb_logits.astype(F32), axis=0), axis=0)

    def kinds(layer, k, plus_one=False):
        m = mod[layer, :, k * d:(k + 1) * d]
        v = jnp.stack([jnp.broadcast_to(m[b], (b, d)), m[:b]], axis=1)
        return 1.0 + v if plus_one else v

    x_cat = jnp.concatenate([ctx, x], axis=1)

    proj = _inproj(x_cat, norm_mix[0], kinds(0, 1, True), kinds(0, 0), hgrn_w_in[0].astype(BF16), n_ctx, F32)
    o2 = _gla(proj, lower_bounds[0], n_ctx)
    x_cat = _hgrn_out(o2, proj, hgrn_norm[0], hgrn_w_out[0].astype(BF16), x_cat, kinds(0, 2), n_ctx)
    hb, aff = _moe_pre(x_cat, norm_ffn[0], kinds(0, 4, True), kinds(0, 3), moe_router[0], n_ctx)
    gate_f = kinds(0, 5)
    moe_lat = _moe_route(aff[:, :, n_ctx:], hb[:, n_ctx:], gate_f[:, 1:2], moe_w_gate, moe_w_up,
                         moe_w_down, 0, 2 * t // ne)
    moe_ctx = _moe_route(aff[:, :, :n_ctx], hb[:, :n_ctx], gate_f[:, 0:1], moe_w_gate, moe_w_up,
                         moe_w_down, 0, 2 * n_ctx // ne)
    x_cat = _add_cat(x_cat, moe_lat, moe_ctx)

    lam_init = 0.8 - 0.6 * math.exp(-0.3 * 1)
    qkv = _inproj(x_cat, norm_mix[1], kinds(1, 1, True), kinds(1, 0), _diff_w_ext(diff_w_in[0]), n_ctx, BF16,
                  rope_tabs=_rope_tables(t, n_ctx))
    att = _diff_attn(qkv, diff_lambda[0], diff_subln[0], n_ctx, lam_init)
    x_lat = _outproj(att, diff_w_out[0].astype(BF16), x_cat, kinds(1, 2)[:, 1:2], n_ctx)
    hb, aff = _moe_pre(x_lat, norm_ffn[1], kinds(1, 4, True), kinds(1, 3), moe_router[1], 0)
    moe_lat = _moe_route(aff, hb, kinds(1, 5)[:, 1:2], moe_w_gate, moe_w_up, moe_w_down, 1, 2 * t // ne)
    return _final(x_lat, moe_lat, norm_final)
```

```python
import functools
import math

import jax
import jax.numpy as jnp
import numpy as np
from jax import lax
from jax.experimental import pallas as pl
from jax.experimental.pallas import tpu as pltpu

F32 = jnp.float32
BF16 = jnp.bfloat16
I32 = jnp.int32

NORM_EPS = 1e-6
LANES = 128
HEAD_W = 128
N_HEADS = 8
GLA_CHUNK = 64
GLA_LEVELS = (32, 16, 8, 4, 2, 1)
ROPE_BASE = 10000.0
GRID_W = 64
TOKEN_BLOCK = 256
VMEM_LIMIT = 56 * 1024 * 1024

_NT = (((1,), (1,)), ((), ()))
_TN = (((0,), (0,)), ((), ()))


def _pick(n, cands):
    for c in cands:
        if n % c == 0:
            return c
    raise ValueError(f"no tile for {n} in {cands}")


def _params(sem):
    return pltpu.CompilerParams(dimension_semantics=sem, vmem_limit_bytes=VMEM_LIMIT)


def _sigmoid(x):
    return 1.0 / (1.0 + jnp.exp(-x))


def _norm_mod(x, g, scale2, shift2, row0, n_ctx):
    ms = jnp.mean(x * x, axis=-1, keepdims=True)
    y = x * lax.rsqrt(ms + NORM_EPS) * g
    rows = row0 + lax.broadcasted_iota(I32, (x.shape[0], 1), 0)
    is_ctx = rows < n_ctx
    sc = jnp.where(is_ctx, scale2[0:1], scale2[1:2])
    sh = jnp.where(is_ctx, shift2[0:1], shift2[1:2])
    return y * sc + sh


def _ada_kernel(c_ref, w_ref, b_ref, o_ref):
    c = c_ref[...]
    s = c * _sigmoid(c)
    o_ref[0] = jnp.dot(s, w_ref[0], precision=lax.Precision.HIGHEST,
                       preferred_element_type=F32) + b_ref[0]


def _ada(cvec, ada_w, ada_b):
    depth, d, n = ada_w.shape
    rows = cvec.shape[0]
    tn = _pick(n, (1024, 512, 256, 128))
    return pl.pallas_call(
        _ada_kernel,
        grid=(depth, n // tn),
        in_specs=[pl.BlockSpec((rows, d), lambda l, j: (0, 0)),
                  pl.BlockSpec((1, d, tn), lambda l, j: (l, 0, j)),
                  pl.BlockSpec((1, 1, tn), lambda l, j: (l, 0, j))],
        out_specs=pl.BlockSpec((1, rows, tn), lambda l, j: (l, 0, j)),
        out_shape=jax.ShapeDtypeStruct((depth, rows, n), F32),
        compiler_params=_params(("arbitrary", "arbitrary")),
        name="adaln",
    )(cvec, ada_w, ada_b.reshape(depth, 1, n))


def _inproj_kernel(x_ref, g_ref, sc_ref, sh_ref, w_ref, *rest, tm, n_ctx, rope):
    if rope:
        cos_ref, sin_ref, o_ref, h_ref = rest
    else:
        o_ref, h_ref = rest
    i = pl.program_id(1)
    n = pl.program_id(2)

    @pl.when(n == 0)
    def _():
        h = _norm_mod(x_ref[0], g_ref[...], sc_ref[0], sh_ref[0], i * tm, n_ctx)
        h_ref[...] = h.astype(BF16)

    acc = jnp.dot(h_ref[...], w_ref[...], preferred_element_type=F32)
    if not rope:
        o_ref[0] = acc.astype(o_ref.dtype)
    else:
        @pl.when(n < 2)
        def _():
            cos = cos_ref[0]
            sin = sin_ref[0]
            lane = lax.broadcasted_iota(I32, (1, HEAD_W), 1)
            first = (lane % 64) < 32
            for hd in range(acc.shape[1] // HEAD_W):
                a = acc[:, hd * HEAD_W:(hd + 1) * HEAD_W]
                rot = jnp.where(first, pltpu.roll(a, HEAD_W - 32, 1), pltpu.roll(a, 32, 1))
                o_ref[0, :, hd * HEAD_W:(hd + 1) * HEAD_W] = (a * cos + rot * sin).astype(o_ref.dtype)

        @pl.when(n >= 2)
        def _():
            col = lax.broadcasted_iota(I32, (1, acc.shape[1]), 1)
            o_ref[0] = jnp.where(col % (2 * HEAD_W) >= HEAD_W, 1.0, acc).astype(o_ref.dtype)


def _inproj(x, g, scale2, shift2, w_bf16, n_ctx, out_dtype, rope_tabs=None):
    b, r, d = x.shape
    n = w_bf16.shape[1]
    tm = _pick(r, (768, 512, 384, 256, 128))
    tn = 1024
    rope = rope_tabs is not None
    in_specs = [pl.BlockSpec((1, tm, d), lambda bb, i, j: (bb, i, 0)),
                pl.BlockSpec((1, d), lambda bb, i, j: (0, 0)),
                pl.BlockSpec((1, 2, d), lambda bb, i, j: (bb, 0, 0)),
                pl.BlockSpec((1, 2, d), lambda bb, i, j: (bb, 0, 0)),
                pl.BlockSpec((d, tn), lambda bb, i, j: (0, j))]
    args = [x, g.reshape(1, d), scale2, shift2, w_bf16]
    if rope:
        cos_t, sin_t = rope_tabs
        spec = pl.BlockSpec((1, tm, HEAD_W), lambda bb, i, j: (jnp.minimum(j, 1), i, 0))
        in_specs += [spec, spec]
        args += [cos_t, sin_t]
    return pl.pallas_call(
        functools.partial(_inproj_kernel, tm=tm, n_ctx=n_ctx, rope=rope),
        grid=(b, r // tm, n // tn),
        in_specs=in_specs,
        out_specs=pl.BlockSpec((1, tm, tn), lambda bb, i, j: (bb, i, j)),
        out_shape=jax.ShapeDtypeStruct((b, r, n), out_dtype),
        scratch_shapes=[pltpu.VMEM((tm, d), BF16)],
        compiler_params=_params(("arbitrary", "arbitrary", "arbitrary")),
        name="inproj_rope" if rope else "inproj",
    )(*args)


def _gla_consts(reverse):
    c = GLA_CHUNK
    nl = len(GLA_LEVELS)
    t = np.arange(c)
    p = c - 1 - t if reverse else t
    tri = (p[None, :] <= p[:, None]).astype(np.float32)
    rowsel = np.zeros((nl, c, HEAD_W), np.float32)
    masks = np.zeros((nl + 1, c, c), np.float32)
    for li, s in enumerate(GLA_LEVELS):
        blk = p // (2 * s)
        second = (p // s) % 2 == 1
        rowsel[li] = second[:, None]
        masks[li] = (blk[:, None] == blk[None, :]) & second[:, None] & (~second)[None, :]
    masks[nl] = np.eye(c)
    return tri, rowsel, masks


def _level_ref(b, s, reverse):
    c = b.shape[0]
    at = s if reverse else s - 1
    if 2 * s >= 8:
        pieces = [jnp.broadcast_to(b[blk * 2 * s + at:blk * 2 * s + at + 1], (2 * s, HEAD_W))
                  for blk in range(c // (2 * s))]
        return pieces[0] if len(pieces) == 1 else jnp.concatenate(pieces, axis=0)
    sub = lax.broadcasted_iota(I32, (8, HEAD_W), 0)
    outs = []
    for v in range(c // 8):
        acc = None
        for blk in range(8 // (2 * s)):
            row = 8 * v + blk * 2 * s + at
            cand = jnp.broadcast_to(b[row:row + 1], (8, HEAD_W))
            acc = cand if acc is None else jnp.where(sub >= blk * 2 * s, cand, acc)
        outs.append(acc)
    return jnp.concatenate(outs, axis=0)


def _gla_kernel(q_ref, f_ref, v_ref, lb_ref, tri_ref, rowsel_ref, masks_ref, o_ref, st_ref, *, tb, reverse):
    c = GLA_CHUNK
    nl = len(GLA_LEVELS)
    nc = tb // c

    @pl.when(pl.program_id(2) == 0)
    def _():
        st_ref[...] = jnp.zeros_like(st_ref)

    lb = lb_ref[0]
    tri = tri_ref[...]
    last = 0 if reverse else c - 1
    starts = [(nc - 1 - ci if reverse else ci) * c for ci in range(nc)]

    qf, kk, vb, b3 = [], [], [], []
    for r0 in starts:
        q = q_ref[0, r0:r0 + c, :]
        qf.append(q * _sigmoid(q))
        f = lb + (1.0 - lb) * _sigmoid(f_ref[0, r0:r0 + c, :])
        lf = jnp.log(f)
        kk.append(1.0 - f)
        vb.append(v_ref[0, r0:r0 + c, :].astype(BF16))
        hi = lf.astype(BF16)
        r1 = lf - hi.astype(F32)
        mid = r1.astype(BF16)
        lo = (r1 - mid.astype(F32)).astype(BF16)
        b3.append(jnp.dot(tri, jnp.concatenate([hi, mid, lo], axis=1), preferred_element_type=F32))
    bc = [x[:, :HEAD_W] + x[:, HEAD_W:2 * HEAD_W] + x[:, 2 * HEAD_W:] for x in b3]

    a = []
    for i in range(nc):
        ai = masks_ref[nl] * lax.dot_general(qf[i].astype(BF16), kk[i].astype(BF16), _NT,
                                             preferred_element_type=F32)
        for li, s in enumerate(GLA_LEVELS):
            e = jnp.exp(-jnp.abs(bc[i] - _level_ref(bc[i], s, reverse)))
            xb = (jnp.where(rowsel_ref[li] > 0, qf[i], kk[i]) * e).astype(BF16)
            ai = ai + masks_ref[li] * lax.dot_general(xb, xb, _NT, preferred_element_type=F32)
        a.append(ai.astype(BF16))

    o_intra = [jnp.dot(a[i], vb[i], preferred_element_type=F32) for i in range(nc)]
    upd = []
    for i in range(nc):
        kdec = (kk[i] * jnp.exp(bc[i][last:last + 1] - bc[i])).astype(BF16)
        upd.append(lax.dot_general(vb[i], kdec, _TN, preferred_element_type=F32))

    st = st_ref[...]
    for i, r0 in enumerate(starts):
        qd = (qf[i] * jnp.exp(bc[i])).astype(BF16)
        o_ref[0, r0:r0 + c, :] = o_intra[i] + lax.dot_general(qd, st.astype(BF16), _NT,
                                                             preferred_element_type=F32)
        st = st * jnp.exp(bc[i][last:last + 1]) + upd[i]
    st_ref[...] = st


def _gla(proj, lb, n_ctx, reverse):
    b, r, _ = proj.shape
    tb = TOKEN_BLOCK
    assert n_ctx % tb == 0 and r % tb == 0
    nblk = r // tb
    ncb = n_ctx // tb
    tri, rowsel, masks = _gla_consts(reverse)
    nl = len(GLA_LEVELS)
    c = GLA_CHUNK
    fcol = N_HEADS * (2 if reverse else 1)

    def blk(n):
        if not reverse:
            return n
        return jnp.where(n < ncb, ncb - 1 - n, nblk - 1 - (n - ncb))

    return pl.pallas_call(
        functools.partial(_gla_kernel, tb=tb, reverse=reverse),
        grid=(b, N_HEADS, nblk),
        in_specs=[pl.BlockSpec((1, tb, HEAD_W), lambda bb, h, n: (bb, blk(n), h)),
                  pl.BlockSpec((1, tb, HEAD_W), lambda bb, h, n: (bb, blk(n), fcol + h)),
                  pl.BlockSpec((1, tb, HEAD_W), lambda bb, h, n: (bb, blk(n), 3 * N_HEADS + h)),
                  pl.BlockSpec((1, 1, HEAD_W), lambda bb, h, n: (h, 0, 0)),
                  pl.BlockSpec((c, c), lambda bb, h, n: (0, 0)),
                  pl.BlockSpec((nl, c, HEAD_W), lambda bb, h, n: (0, 0, 0)),
                  pl.BlockSpec((nl + 1, c, c), lambda bb, h, n: (0, 0, 0))],
        out_specs=pl.BlockSpec((1, tb, HEAD_W), lambda bb, h, n: (bb, blk(n), h)),
        out_shape=jax.ShapeDtypeStruct((b, r, N_HEADS * HEAD_W), F32),
        scratch_shapes=[pltpu.VMEM((HEAD_W, HEAD_W), F32)],
        compiler_params=_params(("arbitrary", "arbitrary", "arbitrary")),
        name="gla_bwd" if reverse else "gla_fwd",
    )(proj, proj, proj, lb.reshape(N_HEADS, 1, HEAD_W), jnp.asarray(tri, BF16),
      jnp.asarray(rowsel), jnp.asarray(masks))


def _hgrn_out_kernel(of_ref, ob_ref, g_ref, ng_ref, w_ref, x_ref, gate_ref, o_ref, y_ref, *, tm, n_ctx):
    i = pl.program_id(1)
    o = of_ref[0] + ob_ref[0]
    for h in range(N_HEADS):
        sl = slice(h * HEAD_W, (h + 1) * HEAD_W)
        oh = o[:, sl]
        ms = jnp.mean(oh * oh, axis=-1, keepdims=True)
        g = g_ref[0, :, sl]
        y_ref[:, sl] = (oh * lax.rsqrt(ms + NORM_EPS) * ng_ref[:, sl] * (g * _sigmoid(g))).astype(BF16)
    y = jnp.dot(y_ref[...], w_ref[...], preferred_element_type=F32)
    rows = i * tm + lax.broadcasted_iota(I32, (tm, 1), 0)
    gate = jnp.where(rows < n_ctx, gate_ref[0, 0:1], gate_ref[0, 1:2])
    o_ref[0] = x_ref[0] + gate * y


def _hgrn_out(o_f, o_b, proj, norm_g, w_bf16, x, gate2, n_ctx):
    b, r, d = x.shape
    tm = _pick(r, (384, 256, 128))
    return pl.pallas_call(
        functools.partial(_hgrn_out_kernel, tm=tm, n_ctx=n_ctx),
        grid=(b, r // tm),
        in_specs=[pl.BlockSpec((1, tm, d), lambda bb, i: (bb, i, 0)),
                  pl.BlockSpec((1, tm, d), lambda bb, i: (bb, i, 0)),
                  pl.BlockSpec((1, tm, d), lambda bb, i: (bb, i, 4)),
                  pl.BlockSpec((1, d), lambda bb, i: (0, 0)),
                  pl.BlockSpec((d, d), lambda bb, i: (0, 0)),
                  pl.BlockSpec((1, tm, d), lambda bb, i: (bb, i, 0)),
                  pl.BlockSpec((1, 2, d), lambda bb, i: (bb, 0, 0))],
        out_specs=pl.BlockSpec((1, tm, d), lambda bb, i: (bb, i, 0)),
        out_shape=jax.ShapeDtypeStruct((b, r, d), F32),
        scratch_shapes=[pltpu.VMEM((tm, d), BF16)],
        compiler_params=_params(("arbitrary", "arbitrary")),
        name="hgrn_out",
    )(o_f, o_b, proj, jnp.tile(norm_g, N_HEADS).reshape(1, d), w_bf16, x, gate2)


def _attn_kernel(q_ref, k_ref, v_ref, lam_ref, g_ref, o_ref, *, tq, tk, lam_init):
    nk = k_ref.shape[1] // tk
    q = q_ref[0]
    lane = lax.broadcasted_iota(I32, (1, HEAD_W), 1)
    zero = jnp.zeros_like(q)
    qc = [jnp.where(lane < 64, q, zero), jnp.where(lane >= 64, q, zero)]

    def scores(u):
        comp, ki = u
        return lax.dot_general(qc[comp], k_ref[0, ki * tk:(ki + 1) * tk, :], _NT, preferred_element_type=F32)

    units = [(comp, ki) for ki in range(nk) for comp in range(2)]
    m = [jnp.full((tq, LANES), -jnp.inf, F32)] * 2
    acc = [jnp.zeros((tq, 2 * HEAD_W), F32)] * 2
    s_next = scores(units[0])
    for idx, (comp, ki) in enumerate(units):
        s = s_next
        if idx + 1 < len(units):
            s_next = scores(units[idx + 1])
        m_new = jnp.maximum(m[comp], jnp.max(s, axis=-1, keepdims=True))
        p = jnp.exp2(s - jnp.tile(m_new, (1, tk // LANES)))
        alpha = jnp.exp2(m[comp] - m_new)
        v1 = v_ref[0, ki * tk:(ki + 1) * tk, :]
        acc[comp] = jnp.tile(alpha, (1, 2)) * acc[comp] + jnp.dot(p.astype(BF16), v1, preferred_element_type=F32)
        m[comp] = m_new
    out = [a[:, :HEAD_W] / a[:, HEAD_W:] for a in acc]

    lv = lam_ref[...]
    s01 = jnp.sum(lv[0:1] * lv[1:2], axis=-1, keepdims=True)
    s23 = jnp.sum(lv[2:3] * lv[3:4], axis=-1, keepdims=True)
    lam = jnp.exp(s01) - jnp.exp(s23) + lam_init
    o = out[0] - lam * out[1]
    ms = jnp.mean(o * o, axis=-1, keepdims=True)
    o_ref[0] = (o * lax.rsqrt(ms + NORM_EPS) * g_ref[...] * (1.0 - lam_init)).astype(o_ref.dtype)


def _diff_attn(qkv, lam_vecs, subln_g, n_ctx, lam_init):
    b, r, _ = qkv.shape
    t = r - n_ctx
    tq = TOKEN_BLOCK
    assert n_ctx % tq == 0 and t % tq == 0
    tk = _pick(r, (2816, 1408, 1280, 768, 640, 512, 256))
    qoff = n_ctx // tq
    return pl.pallas_call(
        functools.partial(_attn_kernel, tq=tq, tk=tk, lam_init=lam_init),
        grid=(b, N_HEADS, t // tq),
        in_specs=[pl.BlockSpec((1, tq, HEAD_W), lambda bb, h, i: (bb, i + qoff, h)),
                  pl.BlockSpec((1, r, HEAD_W), lambda bb, h, i: (bb, 0, N_HEADS + h)),
                  pl.BlockSpec((1, r, 2 * HEAD_W), lambda bb, h, i: (bb, 0, N_HEADS + h)),
                  pl.BlockSpec(lam_vecs.shape, lambda bb, h, i: (0, 0)),
                  pl.BlockSpec((1, HEAD_W), lambda bb, h, i: (0, 0))],
        out_specs=pl.BlockSpec((1, tq, HEAD_W), lambda bb, h, i: (bb, i, h)),
        out_shape=jax.ShapeDtypeStruct((b, t, N_HEADS * HEAD_W), BF16),
        compiler_params=_params(("arbitrary", "arbitrary", "arbitrary")),
        name="diff_attn",
    )(qkv, qkv, qkv, lam_vecs.astype(F32), subln_g.reshape(1, HEAD_W).astype(F32))


def _outproj_kernel(a_ref, w_ref, x_ref, gate_ref, o_ref):
    y = jnp.dot(a_ref[0], w_ref[...], preferred_element_type=F32)
    o_ref[0] = x_ref[0] + gate_ref[0] * y


def _outproj(a, w_bf16, x_cat, gate, n_ctx):
    b, t, d = a.shape
    tm = _pick(math.gcd(t, n_ctx), (512, 256, 128))
    off = n_ctx // tm
    return pl.pallas_call(
        _outproj_kernel,
        grid=(b, t // tm),
        in_specs=[pl.BlockSpec((1, tm, d), lambda bb, i: (bb, i, 0)),
                  pl.BlockSpec((d, d), lambda bb, i: (0, 0)),
                  pl.BlockSpec((1, tm, d), lambda bb, i: (bb, i + off, 0)),
                  pl.BlockSpec((1, 1, d), lambda bb, i: (bb, 0, 0))],
        out_specs=pl.BlockSpec((1, tm, d), lambda bb, i: (bb, i, 0)),
        out_shape=jax.ShapeDtypeStruct((b, t, d), F32),
        compiler_params=_params(("arbitrary", "arbitrary")),
        name="outproj",
    )(a, w_bf16, x_cat, gate)


def _moe_pre_kernel(x_ref, g_ref, sc_ref, sh_ref, wr_ref, h_ref, aff_ref, *, tm, n_ctx):
    i = pl.program_id(1)
    h = _norm_mod(x_ref[0], g_ref[...], sc_ref[0], sh_ref[0], i * tm, n_ctx)
    h_ref[0] = h.astype(BF16)
    logits = lax.dot_general(wr_ref[...], h, _NT, precision=lax.Precision.HIGHEST,
                             preferred_element_type=F32)
    mx = jnp.max(logits, axis=0, keepdims=True)
    ex = jnp.exp(logits - mx)
    aff_ref[0] = ex / jnp.sum(ex, axis=0, keepdims=True)


def _moe_pre(x, g, scale2, shift2, w_router, n_ctx):
    b, r, d = x.shape
    ne = w_router.shape[1]
    tm = _pick(r, (768, 512, 384, 256, 128))
    return pl.pallas_call(
        functools.partial(_moe_pre_kernel, tm=tm, n_ctx=n_ctx),
        grid=(b, r // tm),
        in_specs=[pl.BlockSpec((1, tm, d), lambda bb, i: (bb, i, 0)),
                  pl.BlockSpec((1, d), lambda bb, i: (0, 0)),
                  pl.BlockSpec((1, 2, d), lambda bb, i: (bb, 0, 0)),
                  pl.BlockSpec((1, 2, d), lambda bb, i: (bb, 0, 0)),
                  pl.BlockSpec((ne, d), lambda bb, i: (0, 0))],
        out_specs=[pl.BlockSpec((1, tm, d), lambda bb, i: (bb, i, 0)),
                   pl.BlockSpec((1, ne, tm), lambda bb, i: (bb, 0, i))],
        out_shape=[jax.ShapeDtypeStruct((b, r, d), BF16),
                   jax.ShapeDtypeStruct((b, ne, r), F32)],
        compiler_params=_params(("arbitrary", "arbitrary")),
        name="moe_pre",
    )(x, g.reshape(1, d), scale2, shift2, w_router.T)


def _topk_kernel(aff_ref, tri_ref, pos_ref, gsel_ref, r0_ref, *, cap):
    ne, t = aff_ref.shape[1], aff_ref.shape[2]
    tbk = TOKEN_BLOCK
    nblk = t // tbk
    bits = pltpu.bitcast(aff_ref[0], I32)
    thr = jnp.zeros((ne, 1), I32)
    for bit in range(30, -1, -1):
        cand = thr | (1 << bit)
        cnt = jnp.sum(jnp.where(bits >= cand, 1.0, 0.0), axis=1, keepdims=True)
        thr = jnp.where(cnt >= cap, cand, thr)
    n_gt = jnp.sum(jnp.where(bits > thr, 1.0, 0.0), axis=1, keepdims=True)
    need = cap - n_gt
    tri = tri_ref[...]
    carry_eq = jnp.zeros((ne, 1), F32)
    carry_sel = jnp.zeros((ne, 1), F32)
    r0_ref[0] = jnp.full((ne, LANES), cap, I32)
    for j in range(nblk):
        sl = slice(j * tbk, (j + 1) * tbk)
        a = aff_ref[0, :, sl]
        bj = pltpu.bitcast(a, I32)
        eq = jnp.where(bj == thr, 1.0, 0.0)
        gt = jnp.where(bj > thr, 1.0, 0.0)
        pe = jnp.dot(eq.astype(BF16), tri, preferred_element_type=F32) + carry_eq
        sel = gt + eq * jnp.where(pe - eq < need, 1.0, 0.0)
        ps = jnp.dot(sel.astype(BF16), tri, preferred_element_type=F32) + carry_sel
        pos_ref[0, :, sl] = jnp.where(sel > 0, ps - 1.0, -1.0).astype(I32)
        gsel_ref[0, :, sl] = a * sel
        r0_ref[0, :, j:j + 1] = carry_sel.astype(I32)
        carry_eq = pe[:, tbk - 1:tbk]
        carry_sel = ps[:, tbk - 1:tbk]


def _topk(aff, cap):
    b, ne, t = aff.shape
    tbk = TOKEN_BLOCK
    assert t % tbk == 0 and t // tbk < LANES
    tri = jnp.asarray(np.triu(np.ones((tbk, tbk), np.float32)), BF16)
    return pl.pallas_call(
        functools.partial(_topk_kernel, cap=cap),
        grid=(b,),
        in_specs=[pl.BlockSpec((1, ne, t), lambda bb: (bb, 0, 0)),
                  pl.BlockSpec((tbk, tbk), lambda bb: (0, 0))],
        out_specs=[pl.BlockSpec((1, ne, t), lambda bb: (bb, 0, 0)),
                   pl.BlockSpec((1, ne, t), lambda bb: (bb, 0, 0)),
                   pl.BlockSpec((1, ne, LANES), lambda bb: (bb, 0, 0))],
        out_shape=[jax.ShapeDtypeStruct((b, ne, t), I32),
                   jax.ShapeDtypeStruct((b, ne, t), F32),
                   jax.ShapeDtypeStruct((b, ne, LANES), I32)],
        compiler_params=_params(("arbitrary",)),
        name="topk",
    )(aff, tri)


def _windows(r0_ref, row, j, w, cap, align):
    lo = r0_ref[row, j]
    hi = r0_ref[row, j + 1]
    off = pl.multiple_of(jnp.minimum((lo // align) * align, cap - w), align)
    return off, (jnp.maximum(hi - (off + w), 0) + w - 1) // w


def _gather_kernel(r0_ref, pos_ref, g_ref, h_ref, xs_ref, gs_ref, acc_ref, gacc_ref, *, w, nblk, ne, cap):
    b, e = pl.program_id(0), pl.program_id(1)
    row = b * ne + e
    tbk = TOKEN_BLOCK
    acc_ref[...] = jnp.zeros_like(acc_ref)
    gacc_ref[...] = jnp.zeros_like(gacc_ref)
    sub = lax.broadcasted_iota(I32, (w, tbk), 0)

    def window(j, off, first_rank):
        rank = off + sub
        if first_rank is not None:
            rank = jnp.where(rank >= first_rank, rank, -2)
        hit = pos_ref[0, 0, j:j + 1, :] == rank
        oh = jnp.where(hit, 1.0, 0.0).astype(BF16)
        acc_ref[pl.ds(off, w), :] += jnp.dot(oh, h_ref[0, j * tbk:(j + 1) * tbk, :], preferred_element_type=F32)
        gsum = jnp.sum(jnp.where(hit, g_ref[0, 0, j:j + 1, :], 0.0), axis=1, keepdims=True)
        gacc_ref[pl.ds(off, w), :] += jnp.broadcast_to(gsum, (w, LANES))

    for j in range(nblk):
        off, n_more = _windows(r0_ref, row, j, w, cap, 8)
        window(j, off, None)

        def more(i, _, j=j, off=off):
            start = off + i * w
            window(j, pl.multiple_of(jnp.minimum(start, cap - w), 8), start)
            return 0

        lax.fori_loop(1, n_more + 1, more, 0)
    xs_ref[0, 0] = acc_ref[...].astype(BF16)
    gs_ref[0, 0] = gacc_ref[...]


def _gather(r0, pos, gate, hb, cap):
    b, ne, t = pos.shape
    d = hb.shape[2]
    tbk = TOKEN_BLOCK
    nblk = t // tbk
    w = min(cap, 128)
    pos4 = pos.reshape(b, ne, nblk, tbk)
    g4 = gate.reshape(b, ne, nblk, tbk)
    grid_spec = pltpu.PrefetchScalarGridSpec(
        num_scalar_prefetch=1,
        grid=(b, ne),
        in_specs=[pl.BlockSpec((1, 1, nblk, tbk), lambda bb, e, r: (bb, e, 0, 0)),
                  pl.BlockSpec((1, 1, nblk, tbk), lambda bb, e, r: (bb, e, 0, 0)),
                  pl.BlockSpec((1, t, d), lambda bb, e, r: (bb, 0, 0))],
        out_specs=[pl.BlockSpec((1, 1, cap, d), lambda bb, e, r: (e, bb, 0, 0)),
                   pl.BlockSpec((1, 1, cap, LANES), lambda bb, e, r: (e, bb, 0, 0))],
        scratch_shapes=[pltpu.VMEM((cap, d), F32), pltpu.VMEM((cap, LANES), F32)],
    )
    return pl.pallas_call(
        functools.partial(_gather_kernel, w=w, nblk=nblk, ne=ne, cap=cap),
        grid_spec=grid_spec,
        out_shape=[jax.ShapeDtypeStruct((ne, b, cap, d), BF16),
                   jax.ShapeDtypeStruct((ne, b, cap, LANES), F32)],
        compiler_params=_params(("arbitrary", "arbitrary")),
        name="moe_gather",
    )(r0.reshape(b * ne, LANES), pos4, g4, hb)


def _ffn_kernel(xs_ref, gs_ref, gt_ref, wg_ref, wu_ref, wd_ref, y_ref, acc_ref, *, nb, cap, rc):
    f = pl.program_id(1)
    nf = pl.num_programs(1)

    @pl.when(f == 0)
    def _():
        acc_ref[...] = jnp.zeros_like(acc_ref)

    wg = wg_ref[0, 0].astype(BF16)
    wu = wu_ref[0, 0].astype(BF16)
    wd = wd_ref[0, 0].astype(BF16)
    rows = nb * cap

    def gate_up(c0):
        x = xs_ref[0, c0:c0 + rc, :]
        return jnp.dot(x, wg, preferred_element_type=F32), jnp.dot(x, wu, preferred_element_type=F32)

    starts = list(range(0, rows, rc))
    nxt = gate_up(starts[0])
    for i, c0 in enumerate(starts):
        a, u = nxt
        if i + 1 < len(starts):
            nxt = gate_up(starts[i + 1])
        hm = (a * _sigmoid(a) * u).astype(BF16)
        acc_ref[c0:c0 + rc, :] += jnp.dot(hm, wd, preferred_element_type=F32)

    @pl.when(f == nf - 1)
    def _():
        for bb in range(nb):
            sl = slice(bb * cap, (bb + 1) * cap)
            y_ref[0, sl, :] = (acc_ref[sl, :] * gs_ref[0, sl, 0:1] * gt_ref[bb]).astype(BF16)


def _ffn(xs, gs, gate_f, w_gate, w_up, w_down, layer):
    ne, b, cap, d = xs.shape
    fdim = w_gate.shape[3]
    tf = _pick(fdim, (256, 128))
    rows = b * cap
    rc = _pick(rows, (512, 256, 128, 64))
    y = pl.pallas_call(
        functools.partial(_ffn_kernel, nb=b, cap=cap, rc=rc),
        grid=(ne, fdim // tf),
        in_specs=[pl.BlockSpec((1, rows, d), lambda e, f: (e, 0, 0)),
                  pl.BlockSpec((1, rows, LANES), lambda e, f: (e, 0, 0)),
                  pl.BlockSpec((b, 1, d), lambda e, f: (0, 0, 0)),
                  pl.BlockSpec((1, 1, d, tf), lambda e, f: (layer, e, 0, f)),
                  pl.BlockSpec((1, 1, d, tf), lambda e, f: (layer, e, 0, f)),
                  pl.BlockSpec((1, 1, tf, d), lambda e, f: (layer, e, f, 0))],
        out_specs=pl.BlockSpec((1, rows, d), lambda e, f: (e, 0, 0)),
        out_shape=jax.ShapeDtypeStruct((ne, rows, d), BF16),
        scratch_shapes=[pltpu.VMEM((rows, d), F32)],
        compiler_params=_params(("arbitrary", "arbitrary")),
        name="moe_ffn",
    )(xs.reshape(ne, rows, d), gs.reshape(ne, rows, LANES), gate_f, w_gate, w_up, w_down)
    return y.reshape(ne, b, cap, d)


def _combine_kernel(r0_ref, pos_ref, y_ref, x_ref, o_ref, acc_ref, *, w, ne, cap):
    b, j = pl.program_id(0), pl.program_id(2)
    tbk = TOKEN_BLOCK
    sub = lax.broadcasted_iota(I32, (w, tbk), 0)
    align = 16

    def window(e, off, first_rank):
        rank = off + sub
        if first_rank is not None:
            rank = jnp.where(rank >= first_rank, rank, -2)
        oh = jnp.where(pos_ref[0, 0, e:e + 1, :] == rank, 1.0, 0.0).astype(BF16)
        return lax.dot_general(oh, y_ref[e, 0, pl.ds(off, w), :], _TN, preferred_element_type=F32)

    acc = x_ref[0]
    more = []
    for e in range(ne):
        off, n_more = _windows(r0_ref, b * ne + e, j, w, cap, align)
        acc = acc + window(e, off, None)
        more.append((off, n_more))
    acc_ref[...] = acc
    for e, (off, n_more) in enumerate(more):
        def extra(i, _, e=e, off=off):
            start = off + i * w
            acc_ref[...] += window(e, pl.multiple_of(jnp.minimum(start, cap - w), align), start)
            return 0

        lax.fori_loop(1, n_more + 1, extra, 0)
    o_ref[0] = acc_ref[...]


def _combine(r0, pos, y, x, row_off):
    ne, b, cap, d = y.shape
    t = pos.shape[2]
    tbk = TOKEN_BLOCK
    nblk = t // tbk
    assert row_off % tbk == 0
    boff = row_off // tbk
    w = min(cap, 256)
    dh = d // 2
    pos4 = pos.reshape(b, ne, nblk, tbk).transpose(0, 2, 1, 3)
    grid_spec = pltpu.PrefetchScalarGridSpec(
        num_scalar_prefetch=1,
        grid=(b, 2, nblk),
        in_specs=[pl.BlockSpec((1, 1, ne, tbk), lambda bb, c, j, r: (bb, j, 0, 0)),
                  pl.BlockSpec((ne, 1, cap, dh), lambda bb, c, j, r: (0, bb, 0, c)),
                  pl.BlockSpec((1, tbk, dh), lambda bb, c, j, r: (bb, j + boff, c))],
        out_specs=pl.BlockSpec((1, tbk, dh), lambda bb, c, j, r: (bb, j, c)),
        scratch_shapes=[pltpu.VMEM((tbk, dh), F32)],
    )
    return pl.pallas_call(
        functools.partial(_combine_kernel, w=w, ne=ne, cap=cap),
        grid_spec=grid_spec,
        out_shape=jax.ShapeDtypeStruct((b, t, d), F32),
        compiler_params=_params(("arbitrary", "arbitrary", "arbitrary")),
        name="moe_combine",
    )(r0.reshape(b * ne, LANES), pos4, y, x)


def _final_kernel(x_ref, g_ref, o_ref):
    x = x_ref[0]
    ms = jnp.mean(x * x, axis=-1, keepdims=True)
    o_ref[0] = x * lax.rsqrt(ms + NORM_EPS) * g_ref[...]


def _final(x, g):
    b, t, d = x.shape
    tm = _pick(t, (512, 256, 128))
    return pl.pallas_call(
        _final_kernel,
        grid=(b, t // tm),
        in_specs=[pl.BlockSpec((1, tm, d), lambda bb, i: (bb, i, 0)),
                  pl.BlockSpec((1, d), lambda bb, i: (0, 0))],
        out_specs=pl.BlockSpec((1, tm, d), lambda bb, i: (bb, i, 0)),
        out_shape=jax.ShapeDtypeStruct((b, t, d), F32),
        compiler_params=_params(("arbitrary", "arbitrary")),
        name="final_norm",
    )(x, g.reshape(1, d))


def _moe_route(aff, hb, gate_f, w_gate, w_up, w_down, layer, cap, x, row_off):
    pos, gsel, r0 = _topk(aff, cap)
    xs, gs = _gather(r0, pos, gsel, hb, cap)
    y = _ffn(xs, gs, gate_f, w_gate, w_up, w_down, layer)
    return _combine(r0, pos, y, x, row_off)


def _rope_tables(t, n_ctx):
    rows = t // GRID_W
    row = jnp.repeat(jnp.arange(rows, dtype=F32), GRID_W)
    col = jnp.tile(jnp.arange(GRID_W, dtype=F32), rows)
    pairs = HEAD_W // 8
    freq = ROPE_BASE ** (-jnp.arange(pairs, dtype=F32) / pairs)
    ang = jnp.concatenate([row[:, None] * freq, col[:, None] * freq], axis=-1)
    cos, sin = jnp.cos(ang), jnp.sin(ang)
    cos = jnp.tile(cos, (1, 4))
    sin = jnp.tile(jnp.concatenate([-sin, sin], axis=-1), (1, 2))
    cos = jnp.concatenate([jnp.ones((n_ctx, HEAD_W), F32), cos], axis=0)
    sin = jnp.concatenate([jnp.zeros((n_ctx, HEAD_W), F32), sin], axis=0)
    qs = (HEAD_W // 2) ** -0.5 * math.log2(math.e)
    return jnp.stack([cos * qs, cos]), jnp.stack([sin * qs, sin])


def _diff_w_ext(w):
    d = w.shape[0]
    wv = w[:, 2 * d:].reshape(d, N_HEADS, HEAD_W)
    wv = jnp.concatenate([wv, jnp.zeros_like(wv)], axis=-1).reshape(d, 2 * d)
    return jnp.concatenate([w[:, :2 * d], wv], axis=1).astype(BF16)


def kernel(x, c, ctx, c_ctx, ada_w, ada_b, norm_mix, norm_ffn, norm_final, hgrn_w_in, hgrn_lb_logits, hgrn_norm, hgrn_w_out, diff_w_in, diff_lambda, diff_subln, diff_w_out, moe_router, moe_w_gate, moe_w_up, moe_w_down):
    b, t, d = x.shape
    n_ctx = ctx.shape[1]
    depth = ada_w.shape[0]
    ne = moe_router.shape[2]
    assert depth == 2 and d == N_HEADS * HEAD_W

    cvec = jnp.concatenate([c, c_ctx[None, :], jnp.zeros((8 - b - 1, d), F32)], axis=0)
    mod = _ada(cvec, ada_w, ada_b)
    lower_bounds = jnp.cumsum(jax.nn.softmax(hgrn_lb_logits.astype(F32), axis=0), axis=0)

    def kinds(layer, k, plus_one=False):
        m = mod[layer, :, k * d:(k + 1) * d]
        v = jnp.stack([jnp.broadcast_to(m[b], (b, d)), m[:b]], axis=1)
        return 1.0 + v if plus_one else v

    x_cat = jnp.concatenate([ctx, x], axis=1)

    proj = _inproj(x_cat, norm_mix[0], kinds(0, 1, True), kinds(0, 0), hgrn_w_in[0].astype(BF16), n_ctx, F32)
    o_f = _gla(proj, lower_bounds[0], n_ctx, False)
    o_b = _gla(proj, lower_bounds[0], n_ctx, True)
    x_cat = _hgrn_out(o_f, o_b, proj, hgrn_norm[0], hgrn_w_out[0].astype(BF16), x_cat, kinds(0, 2), n_ctx)
    hb, aff = _moe_pre(x_cat, norm_ffn[0], kinds(0, 4, True), kinds(0, 3), moe_router[0], n_ctx)
    gate_f = kinds(0, 5)
    x_lat = _moe_route(aff[:, :, n_ctx:], hb[:, n_ctx:], gate_f[:, 1:2], moe_w_gate, moe_w_up,
                       moe_w_down, 0, 2 * t // ne, x_cat, n_ctx)
    x_ctx = _moe_route(aff[:, :, :n_ctx], hb[:, :n_ctx], gate_f[:, 0:1], moe_w_gate, moe_w_up,
                       moe_w_down, 0, 2 * n_ctx // ne, x_cat, 0)
    x_cat = jnp.concatenate([x_ctx, x_lat], axis=1)

    lam_init = 0.8 - 0.6 * math.exp(-0.3 * 1)
    qkv = _inproj(x_cat, norm_mix[1], kinds(1, 1, True), kinds(1, 0), _diff_w_ext(diff_w_in[0]), n_ctx, BF16,
                  rope_tabs=_rope_tables(t, n_ctx))
    att = _diff_attn(qkv, diff_lambda[0], diff_subln[0], n_ctx, lam_init)
    x_lat = _outproj(att, diff_w_out[0].astype(BF16), x_cat, kinds(1, 2)[:, 1:2], n_ctx)
    hb, aff = _moe_pre(x_lat, norm_ffn[1], kinds(1, 4, True), kinds(1, 3), moe_router[1], 0)
    x_lat = _moe_route(aff, hb, kinds(1, 5)[:, 1:2], moe_w_gate, moe_w_up, moe_w_down, 1, 2 * t // ne, x_lat, 0)
    return _final(x_lat, norm_final)
```

```python
import functools
import math

import jax
import jax.numpy as jnp
import numpy as np
from jax import lax
from jax.experimental import pallas as pl
from jax.experimental.pallas import tpu as pltpu

F32 = jnp.float32
BF16 = jnp.bfloat16
I32 = jnp.int32

NORM_EPS = 1e-6
LANES = 128
HEAD_W = 128
N_HEADS = 8
GLA_CHUNK = 64
GLA_LEVELS = (32, 16, 8, 4, 2, 1)
ROPE_BASE = 10000.0
GRID_W = 64
TOKEN_BLOCK = 256
VMEM_LIMIT = 56 * 1024 * 1024

_NT = (((1,), (1,)), ((), ()))
_TN = (((0,), (0,)), ((), ()))


def _pick(n, cands):
    for c in cands:
        if n % c == 0:
            return c
    raise ValueError(f"no tile for {n} in {cands}")


def _params(sem):
    return pltpu.CompilerParams(dimension_semantics=sem, vmem_limit_bytes=VMEM_LIMIT)


def _sigmoid(x):
    return 1.0 / (1.0 + jnp.exp(-x))


def _norm_mod(x, g, scale2, shift2, row0, n_lat):
    ms = jnp.mean(x * x, axis=-1, keepdims=True)
    y = x * lax.rsqrt(ms + NORM_EPS) * g
    rows = row0 + lax.broadcasted_iota(I32, (x.shape[0], 1), 0)
    is_ctx = rows >= n_lat
    sc = jnp.where(is_ctx, scale2[0:1], scale2[1:2])
    sh = jnp.where(is_ctx, shift2[0:1], shift2[1:2])
    return y * sc + sh


def _ada_kernel(c_ref, w_ref, b_ref, o_ref):
    c = c_ref[...]
    s = c * _sigmoid(c)
    o_ref[0] = jnp.dot(s, w_ref[0], precision=lax.Precision.HIGHEST,
                       preferred_element_type=F32) + b_ref[0]


def _ada(cvec, ada_w, ada_b):
    depth, d, n = ada_w.shape
    rows = cvec.shape[0]
    tn = _pick(n, (1024, 512, 256, 128))
    return pl.pallas_call(
        _ada_kernel,
        grid=(depth, n // tn),
        in_specs=[pl.BlockSpec((rows, d), lambda l, j: (0, 0)),
                  pl.BlockSpec((1, d, tn), lambda l, j: (l, 0, j)),
                  pl.BlockSpec((1, 1, tn), lambda l, j: (l, 0, j))],
        out_specs=pl.BlockSpec((1, rows, tn), lambda l, j: (l, 0, j)),
        out_shape=jax.ShapeDtypeStruct((depth, rows, n), F32),
        compiler_params=_params(("arbitrary", "arbitrary")),
        name="adaln",
    )(cvec, ada_w, ada_b.reshape(depth, 1, n))


def _inproj_kernel(x_ref, g_ref, sc_ref, sh_ref, w_ref, *rest, tm, n_lat, rope):
    if rope:
        cos_ref, sin_ref, o_ref, h_ref = rest
    else:
        o_ref, h_ref = rest
    i = pl.program_id(1)
    n = pl.program_id(2)

    @pl.when(n == 0)
    def _():
        h = _norm_mod(x_ref[0], g_ref[...], sc_ref[0], sh_ref[0], i * tm, n_lat)
        h_ref[...] = h.astype(BF16)

    acc = jnp.dot(h_ref[...], w_ref[...], preferred_element_type=F32)
    if not rope:
        o_ref[0] = acc.astype(o_ref.dtype)
    else:
        @pl.when(n < 2)
        def _():
            cos = cos_ref[0]
            sin = sin_ref[0]
            lane = lax.broadcasted_iota(I32, (1, HEAD_W), 1)
            first = (lane % 64) < 32
            for hd in range(acc.shape[1] // HEAD_W):
                a = acc[:, hd * HEAD_W:(hd + 1) * HEAD_W]
                rot = jnp.where(first, pltpu.roll(a, HEAD_W - 32, 1), pltpu.roll(a, 32, 1))
                o_ref[0, :, hd * HEAD_W:(hd + 1) * HEAD_W] = (a * cos + rot * sin).astype(o_ref.dtype)

        @pl.when(n >= 2)
        def _():
            col = lax.broadcasted_iota(I32, (1, acc.shape[1]), 1)
            o_ref[0] = jnp.where(col % (2 * HEAD_W) >= HEAD_W, 1.0, acc).astype(o_ref.dtype)


def _inproj(x, g, scale2, shift2, w_bf16, n_lat, out_dtype, rope_tabs=None):
    b, r, d = x.shape
    n = w_bf16.shape[1]
    tm = _pick(r, (768, 512, 384, 256, 128))
    tn = 1024
    rope = rope_tabs is not None
    in_specs = [pl.BlockSpec((1, tm, d), lambda bb, i, j: (bb, i, 0)),
                pl.BlockSpec((1, d), lambda bb, i, j: (0, 0)),
                pl.BlockSpec((1, 2, d), lambda bb, i, j: (bb, 0, 0)),
                pl.BlockSpec((1, 2, d), lambda bb, i, j: (bb, 0, 0)),
                pl.BlockSpec((d, tn), lambda bb, i, j: (0, j))]
    args = [x, g.reshape(1, d), scale2, shift2, w_bf16]
    if rope:
        cos_t, sin_t = rope_tabs
        spec = pl.BlockSpec((1, tm, HEAD_W), lambda bb, i, j: (jnp.minimum(j, 1), i, 0))
        in_specs += [spec, spec]
        args += [cos_t, sin_t]
    return pl.pallas_call(
        functools.partial(_inproj_kernel, tm=tm, n_lat=n_lat, rope=rope),
        grid=(b, r // tm, n // tn),
        in_specs=in_specs,
        out_specs=pl.BlockSpec((1, tm, tn), lambda bb, i, j: (bb, i, j)),
        out_shape=jax.ShapeDtypeStruct((b, r, n), out_dtype),
        scratch_shapes=[pltpu.VMEM((tm, d), BF16)],
        compiler_params=_params(("arbitrary", "arbitrary", "arbitrary")),
        name="inproj_rope" if rope else "inproj",
    )(*args)


def _gla_consts(reverse):
    c = GLA_CHUNK
    nl = len(GLA_LEVELS)
    t = np.arange(c)
    p = c - 1 - t if reverse else t
    tri = (p[None, :] <= p[:, None]).astype(np.float32)
    rowsel = np.zeros((nl, c, HEAD_W), np.float32)
    masks = np.zeros((nl + 1, c, c), np.float32)
    for li, s in enumerate(GLA_LEVELS):
        blk = p // (2 * s)
        second = (p // s) % 2 == 1
        rowsel[li] = second[:, None]
        masks[li] = (blk[:, None] == blk[None, :]) & second[:, None] & (~second)[None, :]
    masks[nl] = np.eye(c)
    return tri, rowsel, masks


def _level_ref(b, s, reverse):
    c = b.shape[0]
    at = s if reverse else s - 1
    if 2 * s >= 8:
        pieces = [jnp.broadcast_to(b[blk * 2 * s + at:blk * 2 * s + at + 1], (2 * s, HEAD_W))
                  for blk in range(c // (2 * s))]
        return pieces[0] if len(pieces) == 1 else jnp.concatenate(pieces, axis=0)
    sub = lax.broadcasted_iota(I32, (8, HEAD_W), 0)
    outs = []
    for v in range(c // 8):
        acc = None
        for blk in range(8 // (2 * s)):
            row = 8 * v + blk * 2 * s + at
            cand = jnp.broadcast_to(b[row:row + 1], (8, HEAD_W))
            acc = cand if acc is None else jnp.where(sub >= blk * 2 * s, cand, acc)
        outs.append(acc)
    return jnp.concatenate(outs, axis=0)


def _gla_kernel(q_ref, f_ref, v_ref, lb_ref, tri_ref, rowsel_ref, masks_ref, o_ref, st_ref, *, tb, hg, reverse):
    c = GLA_CHUNK
    nl = len(GLA_LEVELS)
    nc = tb // c

    @pl.when(pl.program_id(2) == 0)
    def _():
        st_ref[...] = jnp.zeros_like(st_ref)

    tri = tri_ref[...]
    last = 0 if reverse else c - 1
    starts = [(nc - 1 - ci if reverse else ci) * c for ci in range(nc)]
    units = [(hd, r0) for hd in range(hg) for r0 in starts]

    qf, kk, vb, b3 = [], [], [], []
    for hd, r0 in units:
        lanes = slice(hd * HEAD_W, (hd + 1) * HEAD_W)
        lb = lb_ref[0, :, lanes]
        q = q_ref[0, r0:r0 + c, lanes]
        qf.append(q * _sigmoid(q))
        f = lb + (1.0 - lb) * _sigmoid(f_ref[0, r0:r0 + c, lanes])
        lf = jnp.log(f)
        kk.append(1.0 - f)
        vb.append(v_ref[0, r0:r0 + c, lanes].astype(BF16))
        hi = lf.astype(BF16)
        r1 = lf - hi.astype(F32)
        mid = r1.astype(BF16)
        lo = (r1 - mid.astype(F32)).astype(BF16)
        b3.append(jnp.dot(tri, jnp.concatenate([hi, mid, lo], axis=1), preferred_element_type=F32))
    bc = [x[:, :HEAD_W] + x[:, HEAD_W:2 * HEAD_W] + x[:, 2 * HEAD_W:] for x in b3]

    a = []
    for i in range(len(units)):
        ai = masks_ref[nl] * lax.dot_general(qf[i].astype(BF16), kk[i].astype(BF16), _NT,
                                             preferred_element_type=F32)
        for li, s in enumerate(GLA_LEVELS):
            e = jnp.exp(-jnp.abs(bc[i] - _level_ref(bc[i], s, reverse)))
            xb = (jnp.where(rowsel_ref[li] > 0, qf[i], kk[i]) * e).astype(BF16)
            ai = ai + masks_ref[li] * lax.dot_general(xb, xb, _NT, preferred_element_type=F32)
        a.append(ai.astype(BF16))

    o_intra = [jnp.dot(a[i], vb[i], preferred_element_type=F32) for i in range(len(units))]
    upd = []
    for i in range(len(units)):
        kdec = (kk[i] * jnp.exp(bc[i][last:last + 1] - bc[i])).astype(BF16)
        upd.append(lax.dot_general(vb[i], kdec, _TN, preferred_element_type=F32))

    st = [st_ref[hd] for hd in range(hg)]
    for i, (hd, r0) in enumerate(units):
        qd = (qf[i] * jnp.exp(bc[i])).astype(BF16)
        o_ref[0, r0:r0 + c, hd * HEAD_W:(hd + 1) * HEAD_W] = o_intra[i] + lax.dot_general(
            qd, st[hd].astype(BF16), _NT, preferred_element_type=F32)
        st[hd] = st[hd] * jnp.exp(bc[i][last:last + 1]) + upd[i]
    for hd in range(hg):
        st_ref[hd] = st[hd]


def _gla(proj, lb, n_lat, reverse):
    b, r, _ = proj.shape
    tb = TOKEN_BLOCK
    assert n_lat % tb == 0 and r % tb == 0
    nblk = r // tb
    nlb = n_lat // tb
    ncb = nblk - nlb
    tri, rowsel, masks = _gla_consts(reverse)
    nl = len(GLA_LEVELS)
    c = GLA_CHUNK
    hg = 4
    wblk = hg * HEAD_W
    fcol = N_HEADS // hg * (2 if reverse else 1)

    def blk(n):
        if reverse:
            return jnp.where(n < ncb, nblk - 1 - n, nlb - 1 - (n - ncb))
        return jnp.where(n < ncb, nlb + n, n - ncb)

    return pl.pallas_call(
        functools.partial(_gla_kernel, tb=tb, hg=hg, reverse=reverse),
        grid=(b, N_HEADS // hg, nblk),
        in_specs=[pl.BlockSpec((1, tb, wblk), lambda bb, h, n: (bb, blk(n), h)),
                  pl.BlockSpec((1, tb, wblk), lambda bb, h, n: (bb, blk(n), fcol + h)),
                  pl.BlockSpec((1, tb, wblk), lambda bb, h, n: (bb, blk(n), 3 * N_HEADS // hg + h)),
                  pl.BlockSpec((1, 1, wblk), lambda bb, h, n: (h, 0, 0)),
                  pl.BlockSpec((c, c), lambda bb, h, n: (0, 0)),
                  pl.BlockSpec((nl, c, HEAD_W), lambda bb, h, n: (0, 0, 0)),
                  pl.BlockSpec((nl + 1, c, c), lambda bb, h, n: (0, 0, 0))],
        out_specs=pl.BlockSpec((1, tb, wblk), lambda bb, h, n: (bb, blk(n), h)),
        out_shape=jax.ShapeDtypeStruct((b, r, N_HEADS * HEAD_W), F32),
        scratch_shapes=[pltpu.VMEM((hg, HEAD_W, HEAD_W), F32)],
        compiler_params=_params(("arbitrary", "arbitrary", "arbitrary")),
        name="gla_bwd" if reverse else "gla_fwd",
    )(proj, proj, proj, lb.reshape(N_HEADS // hg, 1, wblk), jnp.asarray(tri, BF16),
      jnp.asarray(rowsel), jnp.asarray(masks))


def _hgrn_out_kernel(of_ref, ob_ref, g_ref, ng_ref, w_ref, x_ref, gate_ref, o_ref, y_ref, *, tm, n_lat):
    i = pl.program_id(1)
    o = of_ref[0] + ob_ref[0]
    for h in range(N_HEADS):
        sl = slice(h * HEAD_W, (h + 1) * HEAD_W)
        oh = o[:, sl]
        ms = jnp.mean(oh * oh, axis=-1, keepdims=True)
        g = g_ref[0, :, sl]
        y_ref[:, sl] = (oh * lax.rsqrt(ms + NORM_EPS) * ng_ref[:, sl] * (g * _sigmoid(g))).astype(BF16)
    y = jnp.dot(y_ref[...], w_ref[...], preferred_element_type=F32)
    rows = i * tm + lax.broadcasted_iota(I32, (tm, 1), 0)
    gate = jnp.where(rows >= n_lat, gate_ref[0, 0:1], gate_ref[0, 1:2])
    o_ref[0] = x_ref[0] + gate * y


def _hgrn_out(o_f, o_b, proj, norm_g, w_bf16, x, gate2, n_lat):
    b, r, d = x.shape
    tm = _pick(r, (384, 256, 128))
    return pl.pallas_call(
        functools.partial(_hgrn_out_kernel, tm=tm, n_lat=n_lat),
        grid=(b, r // tm),
        in_specs=[pl.BlockSpec((1, tm, d), lambda bb, i: (bb, i, 0)),
                  pl.BlockSpec((1, tm, d), lambda bb, i: (bb, i, 0)),
                  pl.BlockSpec((1, tm, d), lambda bb, i: (bb, i, 4)),
                  pl.BlockSpec((1, d), lambda bb, i: (0, 0)),
                  pl.BlockSpec((d, d), lambda bb, i: (0, 0)),
                  pl.BlockSpec((1, tm, d), lambda bb, i: (bb, i, 0)),
                  pl.BlockSpec((1, 2, d), lambda bb, i: (bb, 0, 0))],
        out_specs=pl.BlockSpec((1, tm, d), lambda bb, i: (bb, i, 0)),
        out_shape=jax.ShapeDtypeStruct((b, r, d), F32),
        scratch_shapes=[pltpu.VMEM((tm, d), BF16)],
        compiler_params=_params(("arbitrary", "arbitrary")),
        name="hgrn_out",
    )(o_f, o_b, proj, jnp.tile(norm_g, N_HEADS).reshape(1, d), w_bf16, x, gate2)


def _attn_kernel(q_ref, k_ref, v_ref, lam_ref, g_ref, o_ref, *, tq, tk, lam_init):
    nk = k_ref.shape[1] // tk
    q = q_ref[0]
    lane = lax.broadcasted_iota(I32, (1, HEAD_W), 1)
    zero = jnp.zeros_like(q)
    qc = [jnp.where(lane < 64, q, zero), jnp.where(lane >= 64, q, zero)]

    def scores(u):
        comp, ki = u
        return lax.dot_general(qc[comp], k_ref[0, ki * tk:(ki + 1) * tk, :], _NT, preferred_element_type=F32)

    units = [(comp, ki) for ki in range(nk) for comp in range(2)]
    m = [jnp.full((tq, LANES), -jnp.inf, F32)] * 2
    acc = [jnp.zeros((tq, 2 * HEAD_W), F32)] * 2
    s_next = scores(units[0])
    for idx, (comp, ki) in enumerate(units):
        s = s_next
        if idx + 1 < len(units):
            s_next = scores(units[idx + 1])
        m_new = jnp.maximum(m[comp], jnp.max(s, axis=-1, keepdims=True))
        p = jnp.exp2(s - jnp.tile(m_new, (1, tk // LANES)))
        alpha = jnp.exp2(m[comp] - m_new)
        v1 = v_ref[0, ki * tk:(ki + 1) * tk, :]
        acc[comp] = jnp.tile(alpha, (1, 2)) * acc[comp] + jnp.dot(p.astype(BF16), v1, preferred_element_type=F32)
        m[comp] = m_new
    out = [a[:, :HEAD_W] / a[:, HEAD_W:] for a in acc]

    lv = lam_ref[...]
    s01 = jnp.sum(lv[0:1] * lv[1:2], axis=-1, keepdims=True)
    s23 = jnp.sum(lv[2:3] * lv[3:4], axis=-1, keepdims=True)
    lam = jnp.exp(s01) - jnp.exp(s23) + lam_init
    o = out[0] - lam * out[1]
    ms = jnp.mean(o * o, axis=-1, keepdims=True)
    o_ref[0] = (o * lax.rsqrt(ms + NORM_EPS) * g_ref[...] * (1.0 - lam_init)).astype(o_ref.dtype)


def _diff_attn(qkv, lam_vecs, subln_g, t, lam_init):
    b, r, _ = qkv.shape
    tq = _pick(t, (512, 256, 128))
    tk = _pick(r, (768, 1280, 512, 256))
    return pl.pallas_call(
        functools.partial(_attn_kernel, tq=tq, tk=tk, lam_init=lam_init),
        grid=(b, N_HEADS, t // tq),
        in_specs=[pl.BlockSpec((1, tq, HEAD_W), lambda bb, h, i: (bb, i, h)),
                  pl.BlockSpec((1, r, HEAD_W), lambda bb, h, i: (bb, 0, N_HEADS + h)),
                  pl.BlockSpec((1, r, 2 * HEAD_W), lambda bb, h, i: (bb, 0, N_HEADS + h)),
                  pl.BlockSpec(lam_vecs.shape, lambda bb, h, i: (0, 0)),
                  pl.BlockSpec((1, HEAD_W), lambda bb, h, i: (0, 0))],
        out_specs=pl.BlockSpec((1, tq, HEAD_W), lambda bb, h, i: (bb, i, h)),
        out_shape=jax.ShapeDtypeStruct((b, t, N_HEADS * HEAD_W), BF16),
        compiler_params=_params(("arbitrary", "arbitrary", "arbitrary")),
        name="diff_attn",
    )(qkv, qkv, qkv, lam_vecs.astype(F32), subln_g.reshape(1, HEAD_W).astype(F32))


def _outproj_kernel(a_ref, w_ref, x_ref, gate_ref, o_ref):
    y = jnp.dot(a_ref[0], w_ref[...], preferred_element_type=F32)
    o_ref[0] = x_ref[0] + gate_ref[0] * y


def _outproj(a, w_bf16, x_cat, gate):
    b, t, d = a.shape
    tm = _pick(t, (512, 256, 128))
    return pl.pallas_call(
        _outproj_kernel,
        grid=(b, t // tm),
        in_specs=[pl.BlockSpec((1, tm, d), lambda bb, i: (bb, i, 0)),
                  pl.BlockSpec((d, d), lambda bb, i: (0, 0)),
                  pl.BlockSpec((1, tm, d), lambda bb, i: (bb, i, 0)),
                  pl.BlockSpec((1, 1, d), lambda bb, i: (bb, 0, 0))],
        out_specs=pl.BlockSpec((1, tm, d), lambda bb, i: (bb, i, 0)),
        out_shape=jax.ShapeDtypeStruct((b, t, d), F32),
        compiler_params=_params(("arbitrary", "arbitrary")),
        name="outproj",
    )(a, w_bf16, x_cat, gate)


def _moe_pre_kernel(x_ref, g_ref, sc_ref, sh_ref, wr_ref, h_ref, aff_ref, *, tm, n_lat):
    i = pl.program_id(1)
    h = _norm_mod(x_ref[0], g_ref[...], sc_ref[0], sh_ref[0], i * tm, n_lat)
    h_ref[0] = h.astype(BF16)
    logits = lax.dot_general(wr_ref[...], h, _NT, precision=lax.Precision.HIGHEST,
                             preferred_element_type=F32)
    mx = jnp.max(logits, axis=0, keepdims=True)
    ex = jnp.exp(logits - mx)
    aff_ref[0] = ex / jnp.sum(ex, axis=0, keepdims=True)


def _moe_pre(x, g, scale2, shift2, w_router, n_lat):
    b, r, d = x.shape
    ne = w_router.shape[1]
    tm = _pick(r, (768, 512, 384, 256, 128))
    return pl.pallas_call(
        functools.partial(_moe_pre_kernel, tm=tm, n_lat=n_lat),
        grid=(b, r // tm),
        in_specs=[pl.BlockSpec((1, tm, d), lambda bb, i: (bb, i, 0)),
                  pl.BlockSpec((1, d), lambda bb, i: (0, 0)),
                  pl.BlockSpec((1, 2, d), lambda bb, i: (bb, 0, 0)),
                  pl.BlockSpec((1, 2, d), lambda bb, i: (bb, 0, 0)),
                  pl.BlockSpec((ne, d), lambda bb, i: (0, 0))],
        out_specs=[pl.BlockSpec((1, tm, d), lambda bb, i: (bb, i, 0)),
                   pl.BlockSpec((1, ne, tm), lambda bb, i: (bb, 0, i))],
        out_shape=[jax.ShapeDtypeStruct((b, r, d), BF16),
                   jax.ShapeDtypeStruct((b, ne, r), F32)],
        compiler_params=_params(("arbitrary", "arbitrary")),
        name="moe_pre",
    )(x, g.reshape(1, d), scale2, shift2, w_router.T)


def _topk_kernel(aff_ref, tri_ref, pos_ref, gsel_ref, r0_ref, *, cap):
    ne, t = aff_ref.shape[1], aff_ref.shape[2]
    tbk = TOKEN_BLOCK
    nblk = t // tbk
    bits = pltpu.bitcast(aff_ref[0], I32)
    thr = jnp.zeros((ne, 1), I32)
    for bit in range(30, -1, -1):
        cand = thr | (1 << bit)
        cnt = jnp.sum(jnp.where(bits >= cand, 1.0, 0.0), axis=1, keepdims=True)
        thr = jnp.where(cnt >= cap, cand, thr)
    n_gt = jnp.sum(jnp.where(bits > thr, 1.0, 0.0), axis=1, keepdims=True)
    need = cap - n_gt
    tri = tri_ref[...]
    carry_eq = jnp.zeros((ne, 1), F32)
    carry_sel = jnp.zeros((ne, 1), F32)
    r0_ref[0] = jnp.full((ne, LANES), cap, I32)
    for j in range(nblk):
        sl = slice(j * tbk, (j + 1) * tbk)
        a = aff_ref[0, :, sl]
        bj = pltpu.bitcast(a, I32)
        eq = jnp.where(bj == thr, 1.0, 0.0)
        gt = jnp.where(bj > thr, 1.0, 0.0)
        pe = jnp.dot(eq.astype(BF16), tri, preferred_element_type=F32) + carry_eq
        sel = gt + eq * jnp.where(pe - eq < need, 1.0, 0.0)
        ps = jnp.dot(sel.astype(BF16), tri, preferred_element_type=F32) + carry_sel
        pos_ref[0, :, sl] = jnp.where(sel > 0, ps - 1.0, -1.0).astype(I32)
        gsel_ref[0, :, sl] = a * sel
        r0_ref[0, :, j:j + 1] = carry_sel.astype(I32)
        carry_eq = pe[:, tbk - 1:tbk]
        carry_sel = ps[:, tbk - 1:tbk]


def _topk(aff, cap, t, tok_off):
    b, ne, _ = aff.shape
    tbk = TOKEN_BLOCK
    assert t % tbk == 0 and t // tbk < LANES and tok_off % t == 0
    tri = jnp.asarray(np.triu(np.ones((tbk, tbk), np.float32)), BF16)
    return pl.pallas_call(
        functools.partial(_topk_kernel, cap=cap),
        grid=(b,),
        in_specs=[pl.BlockSpec((1, ne, t), lambda bb: (bb, 0, tok_off // t)),
                  pl.BlockSpec((tbk, tbk), lambda bb: (0, 0))],
        out_specs=[pl.BlockSpec((1, ne, t), lambda bb: (bb, 0, 0)),
                   pl.BlockSpec((1, ne, t), lambda bb: (bb, 0, 0)),
                   pl.BlockSpec((1, ne, LANES), lambda bb: (bb, 0, 0))],
        out_shape=[jax.ShapeDtypeStruct((b, ne, t), I32),
                   jax.ShapeDtypeStruct((b, ne, t), F32),
                   jax.ShapeDtypeStruct((b, ne, LANES), I32)],
        compiler_params=_params(("arbitrary",)),
        name="topk",
    )(aff, tri)


def _windows(r0_ref, row, j, w, cap, align):
    lo = r0_ref[row, j]
    hi = r0_ref[row, j + 1]
    off = pl.multiple_of(jnp.minimum((lo // align) * align, cap - w), align)
    return off, (jnp.maximum(hi - (off + w), 0) + w - 1) // w


def _gather_kernel(r0_ref, pos_ref, g_ref, h_ref, xs_ref, gs_ref, acc_ref, gacc_ref, *, w, nblk, ne, cap):
    b, e = pl.program_id(0), pl.program_id(1)
    row = b * ne + e
    tbk = TOKEN_BLOCK
    acc_ref[...] = jnp.zeros_like(acc_ref)
    gacc_ref[...] = jnp.zeros_like(gacc_ref)
    sub = lax.broadcasted_iota(I32, (w, tbk), 0)

    def window(j, off, first_rank):
        rank = off + sub
        if first_rank is not None:
            rank = jnp.where(rank >= first_rank, rank, -2)
        hit = pos_ref[0, 0, j:j + 1, :] == rank
        oh = jnp.where(hit, 1.0, 0.0).astype(BF16)
        acc_ref[pl.ds(off, w), :] += jnp.dot(oh, h_ref[0, j * tbk:(j + 1) * tbk, :], preferred_element_type=F32)
        gsum = jnp.sum(jnp.where(hit, g_ref[0, 0, j:j + 1, :], 0.0), axis=1, keepdims=True)
        gacc_ref[pl.ds(off, w), :] += jnp.broadcast_to(gsum, (w, LANES))

    for j in range(nblk):
        off, n_more = _windows(r0_ref, row, j, w, cap, 8)
        window(j, off, None)

        def more(i, _, j=j, off=off):
            start = off + i * w
            window(j, pl.multiple_of(jnp.minimum(start, cap - w), 8), start)
            return 0

        lax.fori_loop(1, n_more + 1, more, 0)
    xs_ref[0, 0] = acc_ref[...].astype(BF16)
    gs_ref[0, 0] = gacc_ref[...]


def _gather(r0, pos, gate, hb, cap, tok_off):
    b, ne, t = pos.shape
    d = hb.shape[2]
    assert tok_off % t == 0
    tbk = TOKEN_BLOCK
    nblk = t // tbk
    w = min(cap, 128)
    pos4 = pos.reshape(b, ne, nblk, tbk)
    g4 = gate.reshape(b, ne, nblk, tbk)
    grid_spec = pltpu.PrefetchScalarGridSpec(
        num_scalar_prefetch=1,
        grid=(b, ne),
        in_specs=[pl.BlockSpec((1, 1, nblk, tbk), lambda bb, e, r: (bb, e, 0, 0)),
                  pl.BlockSpec((1, 1, nblk, tbk), lambda bb, e, r: (bb, e, 0, 0)),
                  pl.BlockSpec((1, t, d), lambda bb, e, r: (bb, tok_off // t, 0))],
        out_specs=[pl.BlockSpec((1, 1, cap, d), lambda bb, e, r: (e, bb, 0, 0)),
                   pl.BlockSpec((1, 1, cap, LANES), lambda bb, e, r: (e, bb, 0, 0))],
        scratch_shapes=[pltpu.VMEM((cap, d), F32), pltpu.VMEM((cap, LANES), F32)],
    )
    return pl.pallas_call(
        functools.partial(_gather_kernel, w=w, nblk=nblk, ne=ne, cap=cap),
        grid_spec=grid_spec,
        out_shape=[jax.ShapeDtypeStruct((ne, b, cap, d), BF16),
                   jax.ShapeDtypeStruct((ne, b, cap, LANES), F32)],
        compiler_params=_params(("arbitrary", "arbitrary")),
        name="moe_gather",
    )(r0.reshape(b * ne, LANES), pos4, g4, hb)


def _ffn_kernel(*refs, nb, caps, rcs):
    ng = len(caps)
    ins, (wg_ref, wu_ref, wd_ref) = refs[:3 * ng], refs[3 * ng:3 * ng + 3]
    y_refs, acc_refs = refs[3 * ng + 3:4 * ng + 3], refs[4 * ng + 3:]
    f = pl.program_id(1)
    nf = pl.num_programs(1)

    @pl.when(f == 0)
    def _():
        for acc_ref in acc_refs:
            acc_ref[...] = jnp.zeros_like(acc_ref)

    wg = wg_ref[0, 0].astype(BF16)
    wu = wu_ref[0, 0].astype(BF16)
    wd = wd_ref[0, 0].astype(BF16)
    chunks = [(g, c0, rcs[g]) for g in range(ng) for c0 in range(0, nb * caps[g], rcs[g])]

    def gate_up(chunk):
        g, c0, rc = chunk
        x = ins[3 * g][0, c0:c0 + rc, :]
        return jnp.dot(x, wg, preferred_element_type=F32), jnp.dot(x, wu, preferred_element_type=F32)

    nxt = gate_up(chunks[0])
    for i, (g, c0, rc) in enumerate(chunks):
        a, u = nxt
        if i + 1 < len(chunks):
            nxt = gate_up(chunks[i + 1])
        hm = (a * _sigmoid(a) * u).astype(BF16)
        acc_refs[g][c0:c0 + rc, :] += jnp.dot(hm, wd, preferred_element_type=F32)

    @pl.when(f == nf - 1)
    def _():
        for g in range(ng):
            gs_ref, gt_ref = ins[3 * g + 1], ins[3 * g + 2]
            for bb in range(nb):
                sl = slice(bb * caps[g], (bb + 1) * caps[g])
                y_refs[g][0, sl, :] = (acc_refs[g][sl, :] * gs_ref[0, sl, 0:1] * gt_ref[bb]).astype(BF16)


def _ffn(groups, w_gate, w_up, w_down, layer):
    ne, b, _, d = groups[0][0].shape
    fdim = w_gate.shape[3]
    tf = _pick(fdim, (256, 128))
    caps = [xs.shape[2] for xs, _, _ in groups]
    rcs = [_pick(b * cap, (1024, 512, 256, 128, 64)) for cap in caps]
    in_specs, args = [], []
    for (xs, gs, gate_f), cap in zip(groups, caps):
        rows = b * cap
        in_specs += [pl.BlockSpec((1, rows, d), lambda e, f: (e, 0, 0)),
                     pl.BlockSpec((1, rows, LANES), lambda e, f: (e, 0, 0)),
                     pl.BlockSpec((b, 1, d), lambda e, f: (0, 0, 0))]
        args += [xs.reshape(ne, rows, d), gs.reshape(ne, rows, LANES), gate_f]
    in_specs += [pl.BlockSpec((1, 1, d, tf), lambda e, f: (layer, e, 0, f)),
                 pl.BlockSpec((1, 1, d, tf), lambda e, f: (layer, e, 0, f)),
                 pl.BlockSpec((1, 1, tf, d), lambda e, f: (layer, e, f, 0))]
    ys = pl.pallas_call(
        functools.partial(_ffn_kernel, nb=b, caps=caps, rcs=rcs),
        grid=(ne, fdim // tf),
        in_specs=in_specs,
        out_specs=[pl.BlockSpec((1, b * cap, d), lambda e, f: (e, 0, 0)) for cap in caps],
        out_shape=[jax.ShapeDtypeStruct((ne, b * cap, d), BF16) for cap in caps],
        scratch_shapes=[pltpu.VMEM((b * cap, d), F32) for cap in caps],
        compiler_params=_params(("arbitrary", "arbitrary")),
        name="moe_ffn",
    )(*args, w_gate, w_up, w_down)
    return [y.reshape(ne, b, cap, d) for y, cap in zip(ys, caps)]


def _combine_kernel(r0_ref, pos_ref, y_ref, x_ref, o_ref, acc_ref, *, w, ne, cap):
    b, j = pl.program_id(0), pl.program_id(2)
    tbk = TOKEN_BLOCK
    sub = lax.broadcasted_iota(I32, (w, tbk), 0)
    align = 16

    def window(e, off, first_rank):
        rank = off + sub
        if first_rank is not None:
            rank = jnp.where(rank >= first_rank, rank, -2)
        oh = jnp.where(pos_ref[0, 0, e:e + 1, :] == rank, 1.0, 0.0).astype(BF16)
        return lax.dot_general(oh, y_ref[e, 0, pl.ds(off, w), :], _TN, preferred_element_type=F32)

    acc = x_ref[0]
    more = []
    for e in range(ne):
        off, n_more = _windows(r0_ref, b * ne + e, j, w, cap, align)
        acc = acc + window(e, off, None)
        more.append((off, n_more))
    acc_ref[...] = acc
    for e, (off, n_more) in enumerate(more):
        def extra(i, _, e=e, off=off):
            start = off + i * w
            acc_ref[...] += window(e, pl.multiple_of(jnp.minimum(start, cap - w), align), start)
            return 0

        lax.fori_loop(1, n_more + 1, extra, 0)
    o_ref[0] = acc_ref[...]


def _combine(r0, pos, y, x, row_off):
    ne, b, cap, d = y.shape
    t = pos.shape[2]
    tbk = TOKEN_BLOCK
    nblk = t // tbk
    assert row_off % tbk == 0
    boff = row_off // tbk
    w = min(cap, 256)
    dh = d // 2
    pos4 = pos.reshape(b, ne, nblk, tbk).transpose(0, 2, 1, 3)
    grid_spec = pltpu.PrefetchScalarGridSpec(
        num_scalar_prefetch=1,
        grid=(b, 2, nblk),
        in_specs=[pl.BlockSpec((1, 1, ne, tbk), lambda bb, c, j, r: (bb, j, 0, 0)),
                  pl.BlockSpec((ne, 1, cap, dh), lambda bb, c, j, r: (0, bb, 0, c)),
                  pl.BlockSpec((1, tbk, dh), lambda bb, c, j, r: (bb, j + boff, c))],
        out_specs=pl.BlockSpec((1, tbk, dh), lambda bb, c, j, r: (bb, j, c)),
        scratch_shapes=[pltpu.VMEM((tbk, dh), F32)],
    )
    return pl.pallas_call(
        functools.partial(_combine_kernel, w=w, ne=ne, cap=cap),
        grid_spec=grid_spec,
        out_shape=jax.ShapeDtypeStruct((b, t, d), F32),
        compiler_params=_params(("arbitrary", "arbitrary", "arbitrary")),
        name="moe_combine",
    )(r0.reshape(b * ne, LANES), pos4, y, x)


def _final_kernel(x_ref, g_ref, o_ref):
    x = x_ref[0]
    ms = jnp.mean(x * x, axis=-1, keepdims=True)
    o_ref[0] = x * lax.rsqrt(ms + NORM_EPS) * g_ref[...]


def _final(x, g):
    b, t, d = x.shape
    tm = _pick(t, (512, 256, 128))
    return pl.pallas_call(
        _final_kernel,
        grid=(b, t // tm),
        in_specs=[pl.BlockSpec((1, tm, d), lambda bb, i: (bb, i, 0)),
                  pl.BlockSpec((1, d), lambda bb, i: (0, 0))],
        out_specs=pl.BlockSpec((1, tm, d), lambda bb, i: (bb, i, 0)),
        out_shape=jax.ShapeDtypeStruct((b, t, d), F32),
        compiler_params=_params(("arbitrary", "arbitrary")),
        name="final_norm",
    )(x, g.reshape(1, d))


def _moe(aff, hb, x, sets, w_gate, w_up, w_down, layer):
    ne = aff.shape[1]
    routed, groups = [], []
    for off, t, gate_f in sets:
        cap = 2 * t // ne
        pos, gsel, r0 = _topk(aff, cap, t, off)
        xs, gs = _gather(r0, pos, gsel, hb, cap, off)
        routed.append((r0, pos, off))
        groups.append((xs, gs, gate_f))
    ys = _ffn(groups, w_gate, w_up, w_down, layer)
    return [_combine(r0, pos, y, x, off) for (r0, pos, off), y in zip(routed, ys)]


def _rope_tables(t, n_ctx):
    rows = t // GRID_W
    row = jnp.repeat(jnp.arange(rows, dtype=F32), GRID_W)
    col = jnp.tile(jnp.arange(GRID_W, dtype=F32), rows)
    pairs = HEAD_W // 8
    freq = ROPE_BASE ** (-jnp.arange(pairs, dtype=F32) / pairs)
    ang = jnp.concatenate([row[:, None] * freq, col[:, None] * freq], axis=-1)
    cos, sin = jnp.cos(ang), jnp.sin(ang)
    cos = jnp.tile(cos, (1, 4))
    sin = jnp.tile(jnp.concatenate([-sin, sin], axis=-1), (1, 2))
    cos = jnp.concatenate([cos, jnp.ones((n_ctx, HEAD_W), F32)], axis=0)
    sin = jnp.concatenate([sin, jnp.zeros((n_ctx, HEAD_W), F32)], axis=0)
    qs = (HEAD_W // 2) ** -0.5 * math.log2(math.e)
    return jnp.stack([cos * qs, cos]), jnp.stack([sin * qs, sin])


def _diff_w_ext(w):
    d = w.shape[0]
    wv = w[:, 2 * d:].reshape(d, N_HEADS, HEAD_W)
    wv = jnp.concatenate([wv, jnp.zeros_like(wv)], axis=-1).reshape(d, 2 * d)
    return jnp.concatenate([w[:, :2 * d], wv], axis=1).astype(BF16)


def kernel(x, c, ctx, c_ctx, ada_w, ada_b, norm_mix, norm_ffn, norm_final, hgrn_w_in, hgrn_lb_logits, hgrn_norm, hgrn_w_out, diff_w_in, diff_lambda, diff_subln, diff_w_out, moe_router, moe_w_gate, moe_w_up, moe_w_down):
    b, t, d = x.shape
    n_ctx = ctx.shape[1]
    depth = ada_w.shape[0]
    ne = moe_router.shape[2]
    assert depth == 2 and d == N_HEADS * HEAD_W

    cvec = jnp.concatenate([c, c_ctx[None, :], jnp.zeros((8 - b - 1, d), F32)], axis=0)
    mod = _ada(cvec, ada_w, ada_b)
    lower_bounds = jnp.cumsum(jax.nn.softmax(hgrn_lb_logits.astype(F32), axis=0), axis=0)

    def kinds(layer, k, plus_one=False):
        m = mod[layer, :, k * d:(k + 1) * d]
        v = jnp.stack([jnp.broadcast_to(m[b], (b, d)), m[:b]], axis=1)
        return 1.0 + v if plus_one else v

    x_cat = jnp.concatenate([x, ctx], axis=1)

    proj = _inproj(x_cat, norm_mix[0], kinds(0, 1, True), kinds(0, 0), hgrn_w_in[0].astype(BF16), t, F32)
    o_f = _gla(proj, lower_bounds[0], t, False)
    o_b = _gla(proj, lower_bounds[0], t, True)
    x_cat = _hgrn_out(o_f, o_b, proj, hgrn_norm[0], hgrn_w_out[0].astype(BF16), x_cat, kinds(0, 2), t)
    hb, aff = _moe_pre(x_cat, norm_ffn[0], kinds(0, 4, True), kinds(0, 3), moe_router[0], t)
    gate_f = kinds(0, 5)
    x_lat, x_ctx = _moe(aff, hb, x_cat, [(0, t, gate_f[:, 1:2]), (t, n_ctx, gate_f[:, 0:1])],
                        moe_w_gate, moe_w_up, moe_w_down, 0)
    x_cat = jnp.concatenate([x_lat, x_ctx], axis=1)

    lam_init = 0.8 - 0.6 * math.exp(-0.3 * 1)
    qkv = _inproj(x_cat, norm_mix[1], kinds(1, 1, True), kinds(1, 0), _diff_w_ext(diff_w_in[0]), t, BF16,
                  rope_tabs=_rope_tables(t, n_ctx))
    att = _diff_attn(qkv, diff_lambda[0], diff_subln[0], t, lam_init)
    x_lat = _outproj(att, diff_w_out[0].astype(BF16), x_cat, kinds(1, 2)[:, 1:2])
    hb, aff = _moe_pre(x_lat, norm_ffn[1], kinds(1, 4, True), kinds(1, 3), moe_router[1], t)
    (x_lat,) = _moe(aff, hb, x_lat, [(0, t, kinds(1, 5)[:, 1:2])], moe_w_gate, moe_w_up, moe_w_down, 1)
    return _final(x_lat, norm_final)
```

```python
import functools
import math

import jax
import jax.numpy as jnp
import numpy as np
from jax import lax
from jax.experimental import pallas as pl
from jax.experimental.pallas import tpu as pltpu

F32 = jnp.float32
BF16 = jnp.bfloat16
I32 = jnp.int32

NORM_EPS = 1e-6
LANES = 128
HEAD_W = 128
N_HEADS = 8
GLA_CHUNK = 64
GLA_LEVELS = (32, 16, 8, 4, 2, 1)
ROPE_BASE = 10000.0
GRID_W = 64
TOKEN_BLOCK = 256
VMEM_LIMIT = 56 * 1024 * 1024

_NT = (((1,), (1,)), ((), ()))
_TN = (((0,), (0,)), ((), ()))


def _pick(n, cands):
    for c in cands:
        if n % c == 0:
            return c
    raise ValueError(f"no tile for {n} in {cands}")


def _params(sem):
    return pltpu.CompilerParams(dimension_semantics=sem, vmem_limit_bytes=VMEM_LIMIT)


def _sigmoid(x):
    return 1.0 / (1.0 + jnp.exp(-x))


def _norm_mod(x, g, scale2, shift2, row0, n_lat):
    ms = jnp.mean(x * x, axis=-1, keepdims=True)
    y = x * lax.rsqrt(ms + NORM_EPS) * g
    rows = row0 + lax.broadcasted_iota(I32, (x.shape[0], 1), 0)
    is_ctx = rows >= n_lat
    sc = jnp.where(is_ctx, scale2[0:1], scale2[1:2])
    sh = jnp.where(is_ctx, shift2[0:1], shift2[1:2])
    return y * sc + sh


def _ada_kernel(c_ref, w_ref, b_ref, o_ref):
    c = c_ref[...]
    s = c * _sigmoid(c)
    o_ref[0] = jnp.dot(s, w_ref[0], precision=lax.Precision.HIGHEST,
                       preferred_element_type=F32) + b_ref[0]


def _ada(cvec, ada_w, ada_b):
    depth, d, n = ada_w.shape
    rows = cvec.shape[0]
    tn = _pick(n, (1024, 512, 256, 128))
    return pl.pallas_call(
        _ada_kernel,
        grid=(depth, n // tn),
        in_specs=[pl.BlockSpec((rows, d), lambda l, j: (0, 0)),
                  pl.BlockSpec((1, d, tn), lambda l, j: (l, 0, j)),
                  pl.BlockSpec((1, 1, tn), lambda l, j: (l, 0, j))],
        out_specs=pl.BlockSpec((1, rows, tn), lambda l, j: (l, 0, j)),
        out_shape=jax.ShapeDtypeStruct((depth, rows, n), F32),
        compiler_params=_params(("arbitrary", "arbitrary")),
        name="adaln",
    )(cvec, ada_w, ada_b.reshape(depth, 1, n))


def _inproj_kernel(x_ref, g_ref, sc_ref, sh_ref, w_ref, *rest, tm, n_lat, rope):
    if rope:
        cos_ref, sin_ref, o_ref, h_ref = rest
    else:
        o_ref, h_ref = rest
    i = pl.program_id(1)
    n = pl.program_id(2)

    @pl.when(n == 0)
    def _():
        h = _norm_mod(x_ref[0], g_ref[...], sc_ref[0], sh_ref[0], i * tm, n_lat)
        h_ref[...] = h.astype(BF16)

    acc = jnp.dot(h_ref[...], w_ref[...], preferred_element_type=F32)
    if not rope:
        o_ref[0] = acc.astype(o_ref.dtype)
    else:
        @pl.when(n < 2)
        def _():
            cos = cos_ref[0]
            sin = sin_ref[0]
            lane = lax.broadcasted_iota(I32, (1, HEAD_W), 1)
            first = (lane % 64) < 32
            for hd in range(acc.shape[1] // HEAD_W):
                a = acc[:, hd * HEAD_W:(hd + 1) * HEAD_W]
                rot = jnp.where(first, pltpu.roll(a, HEAD_W - 32, 1), pltpu.roll(a, 32, 1))
                o_ref[0, :, hd * HEAD_W:(hd + 1) * HEAD_W] = (a * cos + rot * sin).astype(o_ref.dtype)

        @pl.when(n >= 2)
        def _():
            col = lax.broadcasted_iota(I32, (1, acc.shape[1]), 1)
            o_ref[0] = jnp.where(col % (2 * HEAD_W) >= HEAD_W, 1.0, acc).astype(o_ref.dtype)


def _inproj(x, g, scale2, shift2, w_bf16, n_lat, out_dtype, rope_tabs=None):
    b, r, d = x.shape
    n = w_bf16.shape[1]
    tm = _pick(r, (768, 512, 384, 256, 128))
    tn = 1024
    rope = rope_tabs is not None
    in_specs = [pl.BlockSpec((1, tm, d), lambda bb, i, j: (bb, i, 0)),
                pl.BlockSpec((1, d), lambda bb, i, j: (0, 0)),
                pl.BlockSpec((1, 2, d), lambda bb, i, j: (bb, 0, 0)),
                pl.BlockSpec((1, 2, d), lambda bb, i, j: (bb, 0, 0)),
                pl.BlockSpec((d, tn), lambda bb, i, j: (0, j))]
    args = [x, g.reshape(1, d), scale2, shift2, w_bf16]
    if rope:
        cos_t, sin_t = rope_tabs
        spec = pl.BlockSpec((1, tm, HEAD_W), lambda bb, i, j: (jnp.minimum(j, 1), i, 0))
        in_specs += [spec, spec]
        args += [cos_t, sin_t]
    return pl.pallas_call(
        functools.partial(_inproj_kernel, tm=tm, n_lat=n_lat, rope=rope),
        grid=(b, r // tm, n // tn),
        in_specs=in_specs,
        out_specs=pl.BlockSpec((1, tm, tn), lambda bb, i, j: (bb, i, j)),
        out_shape=jax.ShapeDtypeStruct((b, r, n), out_dtype),
        scratch_shapes=[pltpu.VMEM((tm, d), BF16)],
        compiler_params=_params(("arbitrary", "arbitrary", "arbitrary")),
        name="inproj_rope" if rope else "inproj",
    )(*args)


def _gla_consts(reverse):
    c = GLA_CHUNK
    nl = len(GLA_LEVELS)
    t = np.arange(c)
    p = c - 1 - t if reverse else t
    tri = (p[None, :] <= p[:, None]).astype(np.float32)
    rowsel = np.zeros((nl, c, HEAD_W), np.float32)
    masks = np.zeros((nl + 1, c, c), np.float32)
    for li, s in enumerate(GLA_LEVELS):
        blk = p // (2 * s)
        second = (p // s) % 2 == 1
        rowsel[li] = second[:, None]
        masks[li] = (blk[:, None] == blk[None, :]) & second[:, None] & (~second)[None, :]
    masks[nl] = np.eye(c)
    return tri, rowsel, masks


def _level_ref(b, s, reverse):
    c = b.shape[0]
    at = s if reverse else s - 1
    if 2 * s >= 8:
        pieces = [jnp.broadcast_to(b[blk * 2 * s + at:blk * 2 * s + at + 1], (2 * s, HEAD_W))
                  for blk in range(c // (2 * s))]
        return pieces[0] if len(pieces) == 1 else jnp.concatenate(pieces, axis=0)
    sub = lax.broadcasted_iota(I32, (8, HEAD_W), 0)
    outs = []
    for v in range(c // 8):
        acc = None
        for blk in range(8 // (2 * s)):
            row = 8 * v + blk * 2 * s + at
            cand = jnp.broadcast_to(b[row:row + 1], (8, HEAD_W))
            acc = cand if acc is None else jnp.where(sub >= blk * 2 * s, cand, acc)
        outs.append(acc)
    return jnp.concatenate(outs, axis=0)


def _gla_kernel(q_ref, f_ref, v_ref, lb_ref, tri_ref, rowsel_ref, masks_ref, o_ref, st_ref, *, tb, hg, reverse):
    c = GLA_CHUNK
    nl = len(GLA_LEVELS)
    nc = tb // c

    @pl.when(pl.program_id(2) == 0)
    def _():
        st_ref[...] = jnp.zeros_like(st_ref)

    tri = tri_ref[...]
    last = 0 if reverse else c - 1
    starts = [(nc - 1 - ci if reverse else ci) * c for ci in range(nc)]
    units = [(hd, r0) for hd in range(hg) for r0 in starts]

    qf, kk, vb, b3 = [], [], [], []
    for hd, r0 in units:
        lanes = slice(hd * HEAD_W, (hd + 1) * HEAD_W)
        lb = lb_ref[0, :, lanes]
        q = q_ref[0, r0:r0 + c, lanes]
        qf.append(q * _sigmoid(q))
        f = lb + (1.0 - lb) * _sigmoid(f_ref[0, r0:r0 + c, lanes])
        lf = jnp.log(f)
        kk.append(1.0 - f)
        vb.append(v_ref[0, r0:r0 + c, lanes].astype(BF16))
        hi = lf.astype(BF16)
        r1 = lf - hi.astype(F32)
        mid = r1.astype(BF16)
        lo = (r1 - mid.astype(F32)).astype(BF16)
        b3.append(jnp.dot(tri, jnp.concatenate([hi, mid, lo], axis=1), preferred_element_type=F32))
    bc = [x[:, :HEAD_W] + x[:, HEAD_W:2 * HEAD_W] + x[:, 2 * HEAD_W:] for x in b3]

    a = []
    for i in range(len(units)):
        ai = masks_ref[nl] * lax.dot_general(qf[i].astype(BF16), kk[i].astype(BF16), _NT,
                                             preferred_element_type=F32)
        for li, s in enumerate(GLA_LEVELS):
            e = jnp.exp(-jnp.abs(bc[i] - _level_ref(bc[i], s, reverse)))
            xb = (jnp.where(rowsel_ref[li] > 0, qf[i], kk[i]) * e).astype(BF16)
            ai = ai + masks_ref[li] * lax.dot_general(xb, xb, _NT, preferred_element_type=F32)
        a.append(ai.astype(BF16))

    o_intra = [jnp.dot(a[i], vb[i], preferred_element_type=F32) for i in range(len(units))]
    upd = []
    for i in range(len(units)):
        kdec = (kk[i] * jnp.exp(bc[i][last:last + 1] - bc[i])).astype(BF16)
        upd.append(lax.dot_general(vb[i], kdec, _TN, preferred_element_type=F32))

    st = [st_ref[hd] for hd in range(hg)]
    for i, (hd, r0) in enumerate(units):
        qd = (qf[i] * jnp.exp(bc[i])).astype(BF16)
        o_ref[0, r0:r0 + c, hd * HEAD_W:(hd + 1) * HEAD_W] = o_intra[i] + lax.dot_general(
            qd, st[hd].astype(BF16), _NT, preferred_element_type=F32)
        st[hd] = st[hd] * jnp.exp(bc[i][last:last + 1]) + upd[i]
    for hd in range(hg):
        st_ref[hd] = st[hd]


def _gla(proj, lb, n_lat, reverse):
    b, r, _ = proj.shape
    tb = TOKEN_BLOCK
    assert n_lat % tb == 0 and r % tb == 0
    nblk = r // tb
    nlb = n_lat // tb
    ncb = nblk - nlb
    tri, rowsel, masks = _gla_consts(reverse)
    nl = len(GLA_LEVELS)
    c = GLA_CHUNK
    hg = 8
    wblk = hg * HEAD_W
    fcol = N_HEADS // hg * (2 if reverse else 1)

    def blk(n):
        if reverse:
            return jnp.where(n < ncb, nblk - 1 - n, nlb - 1 - (n - ncb))
        return jnp.where(n < ncb, nlb + n, n - ncb)

    return pl.pallas_call(
        functools.partial(_gla_kernel, tb=tb, hg=hg, reverse=reverse),
        grid=(b, N_HEADS // hg, nblk),
        in_specs=[pl.BlockSpec((1, tb, wblk), lambda bb, h, n: (bb, blk(n), h)),
                  pl.BlockSpec((1, tb, wblk), lambda bb, h, n: (bb, blk(n), fcol + h)),
                  pl.BlockSpec((1, tb, wblk), lambda bb, h, n: (bb, blk(n), 3 * N_HEADS // hg + h)),
                  pl.BlockSpec((1, 1, wblk), lambda bb, h, n: (h, 0, 0)),
                  pl.BlockSpec((c, c), lambda bb, h, n: (0, 0)),
                  pl.BlockSpec((nl, c, HEAD_W), lambda bb, h, n: (0, 0, 0)),
                  pl.BlockSpec((nl + 1, c, c), lambda bb, h, n: (0, 0, 0))],
        out_specs=pl.BlockSpec((1, tb, wblk), lambda bb, h, n: (bb, blk(n), h)),
        out_shape=jax.ShapeDtypeStruct((b, r, N_HEADS * HEAD_W), F32),
        scratch_shapes=[pltpu.VMEM((hg, HEAD_W, HEAD_W), F32)],
        compiler_params=_params(("arbitrary", "arbitrary", "arbitrary")),
        name="gla_bwd" if reverse else "gla_fwd",
    )(proj, proj, proj, lb.reshape(N_HEADS // hg, 1, wblk), jnp.asarray(tri, BF16),
      jnp.asarray(rowsel), jnp.asarray(masks))


def _hgrn_out_kernel(of_ref, ob_ref, g_ref, ng_ref, w_ref, x_ref, gate_ref, o_ref, y_ref, *, tm, n_lat):
    i = pl.program_id(1)
    o = of_ref[0] + ob_ref[0]
    for h in range(N_HEADS):
        sl = slice(h * HEAD_W, (h + 1) * HEAD_W)
        oh = o[:, sl]
        ms = jnp.mean(oh * oh, axis=-1, keepdims=True)
        g = g_ref[0, :, sl]
        y_ref[:, sl] = (oh * lax.rsqrt(ms + NORM_EPS) * ng_ref[:, sl] * (g * _sigmoid(g))).astype(BF16)
    y = jnp.dot(y_ref[...], w_ref[...], preferred_element_type=F32)
    rows = i * tm + lax.broadcasted_iota(I32, (tm, 1), 0)
    gate = jnp.where(rows >= n_lat, gate_ref[0, 0:1], gate_ref[0, 1:2])
    o_ref[0] = x_ref[0] + gate * y


def _hgrn_out(o_f, o_b, proj, norm_g, w_bf16, x, gate2, n_lat):
    b, r, d = x.shape
    tm = _pick(r, (384, 256, 128))
    return pl.pallas_call(
        functools.partial(_hgrn_out_kernel, tm=tm, n_lat=n_lat),
        grid=(b, r // tm),
        in_specs=[pl.BlockSpec((1, tm, d), lambda bb, i: (bb, i, 0)),
                  pl.BlockSpec((1, tm, d), lambda bb, i: (bb, i, 0)),
                  pl.BlockSpec((1, tm, d), lambda bb, i: (bb, i, 4)),
                  pl.BlockSpec((1, d), lambda bb, i: (0, 0)),
                  pl.BlockSpec((d, d), lambda bb, i: (0, 0)),
                  pl.BlockSpec((1, tm, d), lambda bb, i: (bb, i, 0)),
                  pl.BlockSpec((1, 2, d), lambda bb, i: (bb, 0, 0))],
        out_specs=pl.BlockSpec((1, tm, d), lambda bb, i: (bb, i, 0)),
        out_shape=jax.ShapeDtypeStruct((b, r, d), F32),
        scratch_shapes=[pltpu.VMEM((tm, d), BF16)],
        compiler_params=_params(("arbitrary", "arbitrary")),
        name="hgrn_out",
    )(o_f, o_b, proj, jnp.tile(norm_g, N_HEADS).reshape(1, d), w_bf16, x, gate2)


def _attn_kernel(q_ref, k_ref, v_ref, lam_ref, g_ref, o_ref, *, tq, tk, lam_init):
    nk = k_ref.shape[1] // tk
    q = q_ref[0]
    lane = lax.broadcasted_iota(I32, (1, HEAD_W), 1)
    zero = jnp.zeros_like(q)
    qs = jnp.concatenate([jnp.where(lane < 64, q, zero), jnp.where(lane >= 64, q, zero)], axis=0)

    def scores(ki):
        return lax.dot_general(k_ref[0, ki * tk:(ki + 1) * tk, :], qs, _NT, preferred_element_type=F32)

    m = jnp.full((1, 2 * tq), -jnp.inf, F32)
    acc = jnp.zeros((v_ref.shape[2], 2 * tq), F32)
    s_next = scores(0)
    for ki in range(nk):
        s = s_next
        if ki + 1 < nk:
            s_next = scores(ki + 1)
        m_new = jnp.maximum(m, jnp.max(s, axis=0, keepdims=True))
        p = jnp.exp2(s - m_new).astype(BF16)
        alpha = jnp.exp2(m - m_new)
        acc = alpha * acc + jnp.dot(v_ref[0, 0, :, ki * tk:(ki + 1) * tk], p, preferred_element_type=F32)
        m = m_new
    out = acc[:HEAD_W] / acc[HEAD_W:HEAD_W + 1]

    lv = lam_ref[...]
    s01 = jnp.sum(lv[0:1] * lv[1:2], axis=-1, keepdims=True)
    s23 = jnp.sum(lv[2:3] * lv[3:4], axis=-1, keepdims=True)
    lam = jnp.exp(s01) - jnp.exp(s23) + lam_init
    o = out[:, :tq] - lam * out[:, tq:]
    ms = jnp.mean(o * o, axis=0, keepdims=True)
    y = o * lax.rsqrt(ms + NORM_EPS) * g_ref[...] * (1.0 - lam_init)
    o_ref[0] = y.T.astype(o_ref.dtype)


def _diff_attn(qkv, lam_vecs, subln_g, t, lam_init):
    b, r, dq = qkv.shape
    d = N_HEADS * HEAD_W
    tq = _pick(t, (512, 256, 128))
    tk = _pick(r, (768, 1280, 512, 256))
    vrows = HEAD_W + 16
    vt = qkv[:, :, 2 * d:].reshape(b, r, N_HEADS, 2 * HEAD_W)[..., :vrows].transpose(0, 2, 3, 1)
    return pl.pallas_call(
        functools.partial(_attn_kernel, tq=tq, tk=tk, lam_init=lam_init),
        grid=(b, N_HEADS, t // tq),
        in_specs=[pl.BlockSpec((1, tq, HEAD_W), lambda bb, h, i: (bb, i, h)),
                  pl.BlockSpec((1, r, HEAD_W), lambda bb, h, i: (bb, 0, N_HEADS + h)),
                  pl.BlockSpec((1, 1, vrows, r), lambda bb, h, i: (bb, h, 0, 0)),
                  pl.BlockSpec(lam_vecs.shape, lambda bb, h, i: (0, 0)),
                  pl.BlockSpec((HEAD_W, 1), lambda bb, h, i: (0, 0))],
        out_specs=pl.BlockSpec((1, tq, HEAD_W), lambda bb, h, i: (bb, i, h)),
        out_shape=jax.ShapeDtypeStruct((b, t, d), BF16),
        compiler_params=_params(("arbitrary", "arbitrary", "arbitrary")),
        name="diff_attn",
    )(qkv, qkv, vt, lam_vecs.astype(F32), subln_g.reshape(HEAD_W, 1).astype(F32))


def _outproj_kernel(a_ref, w_ref, x_ref, gate_ref, o_ref):
    y = jnp.dot(a_ref[0], w_ref[...], preferred_element_type=F32)
    o_ref[0] = x_ref[0] + gate_ref[0] * y


def _outproj(a, w_bf16, x_cat, gate):
    b, t, d = a.shape
    tm = _pick(t, (512, 256, 128))
    return pl.pallas_call(
        _outproj_kernel,
        grid=(b, t // tm),
        in_specs=[pl.BlockSpec((1, tm, d), lambda bb, i: (bb, i, 0)),
                  pl.BlockSpec((d, d), lambda bb, i: (0, 0)),
                  pl.BlockSpec((1, tm, d), lambda bb, i: (bb, i, 0)),
                  pl.BlockSpec((1, 1, d), lambda bb, i: (bb, 0, 0))],
        out_specs=pl.BlockSpec((1, tm, d), lambda bb, i: (bb, i, 0)),
        out_shape=jax.ShapeDtypeStruct((b, t, d), F32),
        compiler_params=_params(("arbitrary", "arbitrary")),
        name="outproj",
    )(a, w_bf16, x_cat, gate)


def _moe_pre_kernel(x_ref, g_ref, sc_ref, sh_ref, wr_ref, h_ref, aff_ref, *, tm, n_lat):
    i = pl.program_id(1)
    h = _norm_mod(x_ref[0], g_ref[...], sc_ref[0], sh_ref[0], i * tm, n_lat)
    h_ref[0] = h.astype(BF16)
    logits = lax.dot_general(wr_ref[...], h, _NT, precision=lax.Precision.HIGHEST,
                             preferred_element_type=F32)
    mx = jnp.max(logits, axis=0, keepdims=True)
    ex = jnp.exp(logits - mx)
    aff_ref[0] = ex / jnp.sum(ex, axis=0, keepdims=True)


def _moe_pre(x, g, scale2, shift2, w_router, n_lat):
    b, r, d = x.shape
    ne = w_router.shape[1]
    tm = _pick(r, (768, 512, 384, 256, 128))
    return pl.pallas_call(
        functools.partial(_moe_pre_kernel, tm=tm, n_lat=n_lat),
        grid=(b, r // tm),
        in_specs=[pl.BlockSpec((1, tm, d), lambda bb, i: (bb, i, 0)),
                  pl.BlockSpec((1, d), lambda bb, i: (0, 0)),
                  pl.BlockSpec((1, 2, d), lambda bb, i: (bb, 0, 0)),
                  pl.BlockSpec((1, 2, d), lambda bb, i: (bb, 0, 0)),
                  pl.BlockSpec((ne, d), lambda bb, i: (0, 0))],
        out_specs=[pl.BlockSpec((1, tm, d), lambda bb, i: (bb, i, 0)),
                   pl.BlockSpec((1, ne, tm), lambda bb, i: (bb, 0, i))],
        out_shape=[jax.ShapeDtypeStruct((b, r, d), BF16),
                   jax.ShapeDtypeStruct((b, ne, r), F32)],
        compiler_params=_params(("arbitrary", "arbitrary")),
        name="moe_pre",
    )(x, g.reshape(1, d), scale2, shift2, w_router.T)


def _topk_kernel(aff_ref, tri_ref, pos_ref, gsel_ref, r0_ref, *, cap):
    ne, t = aff_ref.shape[1], aff_ref.shape[2]
    tbk = TOKEN_BLOCK
    nblk = t // tbk
    bits = pltpu.bitcast(aff_ref[0], I32)
    thr = jnp.zeros((ne, 1), I32)
    for bit in range(30, -1, -1):
        cand = thr | (1 << bit)
        cnt = jnp.sum(jnp.where(bits >= cand, 1.0, 0.0), axis=1, keepdims=True)
        thr = jnp.where(cnt >= cap, cand, thr)
    n_gt = jnp.sum(jnp.where(bits > thr, 1.0, 0.0), axis=1, keepdims=True)
    need = cap - n_gt
    tri = tri_ref[...]
    carry_eq = jnp.zeros((ne, 1), F32)
    carry_sel = jnp.zeros((ne, 1), F32)
    r0_ref[0] = jnp.full((ne, LANES), cap, I32)
    for j in range(nblk):
        sl = slice(j * tbk, (j + 1) * tbk)
        a = aff_ref[0, :, sl]
        bj = pltpu.bitcast(a, I32)
        eq = jnp.where(bj == thr, 1.0, 0.0)
        gt = jnp.where(bj > thr, 1.0, 0.0)
        pe = jnp.dot(eq.astype(BF16), tri, preferred_element_type=F32) + carry_eq
        sel = gt + eq * jnp.where(pe - eq < need, 1.0, 0.0)
        ps = jnp.dot(sel.astype(BF16), tri, preferred_element_type=F32) + carry_sel
        pos_ref[0, :, sl] = jnp.where(sel > 0, ps - 1.0, -1.0).astype(I32)
        gsel_ref[0, :, sl] = a * sel
        r0_ref[0, :, j:j + 1] = carry_sel.astype(I32)
        carry_eq = pe[:, tbk - 1:tbk]
        carry_sel = ps[:, tbk - 1:tbk]


def _topk(aff, cap, t, tok_off):
    b, ne, _ = aff.shape
    tbk = TOKEN_BLOCK
    assert t % tbk == 0 and t // tbk < LANES and tok_off % t == 0
    tri = jnp.asarray(np.triu(np.ones((tbk, tbk), np.float32)), BF16)
    return pl.pallas_call(
        functools.partial(_topk_kernel, cap=cap),
        grid=(b,),
        in_specs=[pl.BlockSpec((1, ne, t), lambda bb: (bb, 0, tok_off // t)),
                  pl.BlockSpec((tbk, tbk), lambda bb: (0, 0))],
        out_specs=[pl.BlockSpec((1, ne, t), lambda bb: (bb, 0, 0)),
                   pl.BlockSpec((1, ne, t), lambda bb: (bb, 0, 0)),
                   pl.BlockSpec((1, ne, LANES), lambda bb: (bb, 0, 0))],
        out_shape=[jax.ShapeDtypeStruct((b, ne, t), I32),
                   jax.ShapeDtypeStruct((b, ne, t), F32),
                   jax.ShapeDtypeStruct((b, ne, LANES), I32)],
        compiler_params=_params(("arbitrary",)),
        name="topk",
    )(aff, tri)


def _windows(r0_ref, row, j, w, cap, align):
    lo = r0_ref[row, j]
    hi = r0_ref[row, j + 1]
    off = pl.multiple_of(jnp.minimum((lo // align) * align, cap - w), align)
    return off, (jnp.maximum(hi - (off + w), 0) + w - 1) // w


def _gather_kernel(r0_ref, pos_ref, g_ref, h_ref, xs_ref, gs_ref, acc_ref, gacc_ref, *, w, nblk, ne, cap):
    b, e = pl.program_id(0), pl.program_id(1)
    row = b * ne + e
    tbk = TOKEN_BLOCK
    acc_ref[...] = jnp.zeros_like(acc_ref)
    gacc_ref[...] = jnp.zeros_like(gacc_ref)
    sub = lax.broadcasted_iota(I32, (w, tbk), 0)

    def window(j, off, first_rank):
        rank = off + sub
        if first_rank is not None:
            rank = jnp.where(rank >= first_rank, rank, -2)
        hit = pos_ref[0, 0, j:j + 1, :] == rank
        oh = jnp.where(hit, 1.0, 0.0).astype(BF16)
        acc_ref[pl.ds(off, w), :] += jnp.dot(oh, h_ref[0, j * tbk:(j + 1) * tbk, :], preferred_element_type=F32)
        gsum = jnp.sum(jnp.where(hit, g_ref[0, 0, j:j + 1, :], 0.0), axis=1, keepdims=True)
        gacc_ref[pl.ds(off, w), :] += jnp.broadcast_to(gsum, (w, LANES))

    for j in range(nblk):
        off, n_more = _windows(r0_ref, row, j, w, cap, 8)
        window(j, off, None)

        def more(i, _, j=j, off=off):
            start = off + i * w
            window(j, pl.multiple_of(jnp.minimum(start, cap - w), 8), start)
            return 0

        lax.fori_loop(1, n_more + 1, more, 0)
    xs_ref[0, 0] = acc_ref[...].astype(BF16)
    gs_ref[0, 0] = gacc_ref[...]


def _gather(r0, pos, gate, hb, cap, tok_off):
    b, ne, t = pos.shape
    d = hb.shape[2]
    assert tok_off % t == 0
    tbk = TOKEN_BLOCK
    nblk = t // tbk
    w = min(cap, 64)
    pos4 = pos.reshape(b, ne, nblk, tbk)
    g4 = gate.reshape(b, ne, nblk, tbk)
    grid_spec = pltpu.PrefetchScalarGridSpec(
        num_scalar_prefetch=1,
        grid=(b, ne),
        in_specs=[pl.BlockSpec((1, 1, nblk, tbk), lambda bb, e, r: (bb, e, 0, 0)),
                  pl.BlockSpec((1, 1, nblk, tbk), lambda bb, e, r: (bb, e, 0, 0)),
                  pl.BlockSpec((1, t, d), lambda bb, e, r: (bb, tok_off // t, 0))],
        out_specs=[pl.BlockSpec((1, 1, cap, d), lambda bb, e, r: (e, bb, 0, 0)),
                   pl.BlockSpec((1, 1, cap, LANES), lambda bb, e, r: (e, bb, 0, 0))],
        scratch_shapes=[pltpu.VMEM((cap, d), F32), pltpu.VMEM((cap, LANES), F32)],
    )
    return pl.pallas_call(
        functools.partial(_gather_kernel, w=w, nblk=nblk, ne=ne, cap=cap),
        grid_spec=grid_spec,
        out_shape=[jax.ShapeDtypeStruct((ne, b, cap, d), BF16),
                   jax.ShapeDtypeStruct((ne, b, cap, LANES), F32)],
        compiler_params=_params(("arbitrary", "arbitrary")),
        name="moe_gather",
    )(r0.reshape(b * ne, LANES), pos4, g4, hb)


def _ffn_kernel(*refs, nb, caps, rcs):
    ng = len(caps)
    ins, (wg_ref, wu_ref, wd_ref) = refs[:3 * ng], refs[3 * ng:3 * ng + 3]
    y_refs, acc_refs = refs[3 * ng + 3:4 * ng + 3], refs[4 * ng + 3:]
    f = pl.program_id(1)
    nf = pl.num_programs(1)

    @pl.when(f == 0)
    def _():
        for acc_ref in acc_refs:
            acc_ref[...] = jnp.zeros_like(acc_ref)

    wg = wg_ref[0, 0].astype(BF16)
    wu = wu_ref[0, 0].astype(BF16)
    wd = wd_ref[0, 0].astype(BF16)
    chunks = [(g, c0, rcs[g]) for g in range(ng) for c0 in range(0, nb * caps[g], rcs[g])]

    def gate_up(chunk):
        g, c0, rc = chunk
        x = ins[3 * g][0, c0:c0 + rc, :]
        return jnp.dot(x, wg, preferred_element_type=F32), jnp.dot(x, wu, preferred_element_type=F32)

    nxt = gate_up(chunks[0])
    for i, (g, c0, rc) in enumerate(chunks):
        a, u = nxt
        if i + 1 < len(chunks):
            nxt = gate_up(chunks[i + 1])
        hm = (a * _sigmoid(a) * u).astype(BF16)
        acc_refs[g][c0:c0 + rc, :] += jnp.dot(hm, wd, preferred_element_type=F32)

    @pl.when(f == nf - 1)
    def _():
        for g in range(ng):
            gs_ref, gt_ref = ins[3 * g + 1], ins[3 * g + 2]
            for bb in range(nb):
                sl = slice(bb * caps[g], (bb + 1) * caps[g])
                y_refs[g][0, sl, :] = (acc_refs[g][sl, :] * gs_ref[0, sl, 0:1] * gt_ref[bb]).astype(BF16)


def _ffn(groups, w_gate, w_up, w_down, layer):
    ne, b, _, d = groups[0][0].shape
    fdim = w_gate.shape[3]
    tf = _pick(fdim, (256, 128))
    caps = [xs.shape[2] for xs, _, _ in groups]
    rcs = [_pick(b * cap, (1024, 512, 256, 128, 64)) for cap in caps]
    in_specs, args = [], []
    for (xs, gs, gate_f), cap in zip(groups, caps):
        rows = b * cap
        in_specs += [pl.BlockSpec((1, rows, d), lambda e, f: (e, 0, 0)),
                     pl.BlockSpec((1, rows, LANES), lambda e, f: (e, 0, 0)),
                     pl.BlockSpec((b, 1, d), lambda e, f: (0, 0, 0))]
        args += [xs.reshape(ne, rows, d), gs.reshape(ne, rows, LANES), gate_f]
    in_specs += [pl.BlockSpec((1, 1, d, tf), lambda e, f: (layer, e, 0, f)),
                 pl.BlockSpec((1, 1, d, tf), lambda e, f: (layer, e, 0, f)),
                 pl.BlockSpec((1, 1, tf, d), lambda e, f: (layer, e, f, 0))]
    ys = pl.pallas_call(
        functools.partial(_ffn_kernel, nb=b, caps=caps, rcs=rcs),
        grid=(ne, fdim // tf),
        in_specs=in_specs,
        out_specs=[pl.BlockSpec((1, b * cap, d), lambda e, f: (e, 0, 0)) for cap in caps],
        out_shape=[jax.ShapeDtypeStruct((ne, b * cap, d), BF16) for cap in caps],
        scratch_shapes=[pltpu.VMEM((b * cap, d), F32) for cap in caps],
        compiler_params=_params(("arbitrary", "arbitrary")),
        name="moe_ffn",
    )(*args, w_gate, w_up, w_down)
    return [y.reshape(ne, b, cap, d) for y, cap in zip(ys, caps)]


def _combine_kernel(r0_ref, pos_ref, y_ref, x_ref, o_ref, acc_ref, *, w, ne, cap):
    b, j = pl.program_id(0), pl.program_id(2)
    tbk = TOKEN_BLOCK
    sub = lax.broadcasted_iota(I32, (w, tbk), 0)
    align = 16

    def window(e, off, first_rank):
        rank = off + sub
        if first_rank is not None:
            rank = jnp.where(rank >= first_rank, rank, -2)
        oh = jnp.where(pos_ref[0, 0, e:e + 1, :] == rank, 1.0, 0.0).astype(BF16)
        return lax.dot_general(oh, y_ref[e, 0, pl.ds(off, w), :], _TN, preferred_element_type=F32)

    acc = x_ref[0]
    more = []
    for e in range(ne):
        off, n_more = _windows(r0_ref, b * ne + e, j, w, cap, align)
        acc = acc + window(e, off, None)
        more.append((off, n_more))
    acc_ref[...] = acc
    for e, (off, n_more) in enumerate(more):
        def extra(i, _, e=e, off=off):
            start = off + i * w
            acc_ref[...] += window(e, pl.multiple_of(jnp.minimum(start, cap - w), align), start)
            return 0

        lax.fori_loop(1, n_more + 1, extra, 0)
    o_ref[0] = acc_ref[...]


def _combine(r0, pos, y, x, row_off):
    ne, b, cap, d = y.shape
    t = pos.shape[2]
    tbk = TOKEN_BLOCK
    nblk = t // tbk
    assert row_off % tbk == 0
    boff = row_off // tbk
    w = min(cap, 256)
    dh = d // 2
    pos4 = pos.reshape(b, ne, nblk, tbk).transpose(0, 2, 1, 3)
    grid_spec = pltpu.PrefetchScalarGridSpec(
        num_scalar_prefetch=1,
        grid=(b, 2, nblk),
        in_specs=[pl.BlockSpec((1, 1, ne, tbk), lambda bb, c, j, r: (bb, j, 0, 0)),
                  pl.BlockSpec((ne, 1, cap, dh), lambda bb, c, j, r: (0, bb, 0, c)),
                  pl.BlockSpec((1, tbk, dh), lambda bb, c, j, r: (bb, j + boff, c))],
        out_specs=pl.BlockSpec((1, tbk, dh), lambda bb, c, j, r: (bb, j, c)),
        scratch_shapes=[pltpu.VMEM((tbk, dh), F32)],
    )
    return pl.pallas_call(
        functools.partial(_combine_kernel, w=w, ne=ne, cap=cap),
        grid_spec=grid_spec,
        out_shape=jax.ShapeDtypeStruct((b, t, d), F32),
        compiler_params=_params(("arbitrary", "arbitrary", "arbitrary")),
        name="moe_combine",
    )(r0.reshape(b * ne, LANES), pos4, y, x)


def _final_kernel(x_ref, g_ref, o_ref):
    x = x_ref[0]
    ms = jnp.mean(x * x, axis=-1, keepdims=True)
    o_ref[0] = x * lax.rsqrt(ms + NORM_EPS) * g_ref[...]


def _final(x, g):
    b, t, d = x.shape
    tm = _pick(t, (512, 256, 128))
    return pl.pallas_call(
        _final_kernel,
        grid=(b, t // tm),
        in_specs=[pl.BlockSpec((1, tm, d), lambda bb, i: (bb, i, 0)),
                  pl.BlockSpec((1, d), lambda bb, i: (0, 0))],
        out_specs=pl.BlockSpec((1, tm, d), lambda bb, i: (bb, i, 0)),
        out_shape=jax.ShapeDtypeStruct((b, t, d), F32),
        compiler_params=_params(("arbitrary", "arbitrary")),
        name="final_norm",
    )(x, g.reshape(1, d))


def _moe(aff, hb, x, sets, w_gate, w_up, w_down, layer):
    ne = aff.shape[1]
    routed, groups = [], []
    for off, t, gate_f in sets:
        cap = 2 * t // ne
        pos, gsel, r0 = _topk(aff, cap, t, off)
        xs, gs = _gather(r0, pos, gsel, hb, cap, off)
        routed.append((r0, pos, off))
        groups.append((xs, gs, gate_f))
    ys = _ffn(groups, w_gate, w_up, w_down, layer)
    return [_combine(r0, pos, y, x, off) for (r0, pos, off), y in zip(routed, ys)]


def _rope_tables(t, n_ctx):
    rows = t // GRID_W
    row = jnp.repeat(jnp.arange(rows, dtype=F32), GRID_W)
    col = jnp.tile(jnp.arange(GRID_W, dtype=F32), rows)
    pairs = HEAD_W // 8
    freq = ROPE_BASE ** (-jnp.arange(pairs, dtype=F32) / pairs)
    ang = jnp.concatenate([row[:, None] * freq, col[:, None] * freq], axis=-1)
    cos, sin = jnp.cos(ang), jnp.sin(ang)
    cos = jnp.tile(cos, (1, 4))
    sin = jnp.tile(jnp.concatenate([-sin, sin], axis=-1), (1, 2))
    cos = jnp.concatenate([cos, jnp.ones((n_ctx, HEAD_W), F32)], axis=0)
    sin = jnp.concatenate([sin, jnp.zeros((n_ctx, HEAD_W), F32)], axis=0)
    qs = (HEAD_W // 2) ** -0.5 * math.log2(math.e)
    return jnp.stack([cos * qs, cos]), jnp.stack([sin * qs, sin])


def _diff_w_ext(w):
    d = w.shape[0]
    wv = w[:, 2 * d:].reshape(d, N_HEADS, HEAD_W)
    wv = jnp.concatenate([wv, jnp.zeros_like(wv)], axis=-1).reshape(d, 2 * d)
    return jnp.concatenate([w[:, :2 * d], wv], axis=1).astype(BF16)


def kernel(x, c, ctx, c_ctx, ada_w, ada_b, norm_mix, norm_ffn, norm_final, hgrn_w_in, hgrn_lb_logits, hgrn_norm, hgrn_w_out, diff_w_in, diff_lambda, diff_subln, diff_w_out, moe_router, moe_w_gate, moe_w_up, moe_w_down):
    b, t, d = x.shape
    n_ctx = ctx.shape[1]
    depth = ada_w.shape[0]
    ne = moe_router.shape[2]
    assert depth == 2 and d == N_HEADS * HEAD_W

    cvec = jnp.concatenate([c, c_ctx[None, :], jnp.zeros((8 - b - 1, d), F32)], axis=0)
    mod = _ada(cvec, ada_w, ada_b)
    lower_bounds = jnp.cumsum(jax.nn.softmax(hgrn_lb_logits.astype(F32), axis=0), axis=0)

    def kinds(layer, k, plus_one=False):
        m = mod[layer, :, k * d:(k + 1) * d]
        v = jnp.stack([jnp.broadcast_to(m[b], (b, d)), m[:b]], axis=1)
        return 1.0 + v if plus_one else v

    x_cat = jnp.concatenate([x, ctx], axis=1)

    proj = _inproj(x_cat, norm_mix[0], kinds(0, 1, True), kinds(0, 0), hgrn_w_in[0].astype(BF16), t, F32)
    o_f = _gla(proj, lower_bounds[0], t, False)
    o_b = _gla(proj, lower_bounds[0], t, True)
    x_cat = _hgrn_out(o_f, o_b, proj, hgrn_norm[0], hgrn_w_out[0].astype(BF16), x_cat, kinds(0, 2), t)
    hb, aff = _moe_pre(x_cat, norm_ffn[0], kinds(0, 4, True), kinds(0, 3), moe_router[0], t)
    gate_f = kinds(0, 5)
    x_lat, x_ctx = _moe(aff, hb, x_cat, [(0, t, gate_f[:, 1:2]), (t, n_ctx, gate_f[:, 0:1])],
                        moe_w_gate, moe_w_up, moe_w_down, 0)
    x_cat = jnp.concatenate([x_lat, x_ctx], axis=1)

    lam_init = 0.8 - 0.6 * math.exp(-0.3 * 1)
    qkv = _inproj(x_cat, norm_mix[1], kinds(1, 1, True), kinds(1, 0), _diff_w_ext(diff_w_in[0]), t, BF16,
                  rope_tabs=_rope_tables(t, n_ctx))
    att = _diff_attn(qkv, diff_lambda[0], diff_subln[0], t, lam_init)
    x_lat = _outproj(att, diff_w_out[0].astype(BF16), x_cat, kinds(1, 2)[:, 1:2])
    hb, aff = _moe_pre(x_lat, norm_ffn[1], kinds(1, 4, True), kinds(1, 3), moe_router[1], t)
    (x_lat,) = _moe(aff, hb, x_lat, [(0, t, kinds(1, 5)[:, 1:2])], moe_w_gate, moe_w_up, moe_w_down, 1)
    return _final(x_lat, norm_final)
```

```python
import functools
import math

import jax
import jax.numpy as jnp
import numpy as np
from jax import lax
from jax.experimental import pallas as pl
from jax.experimental.pallas import tpu as pltpu

F32 = jnp.float32
BF16 = jnp.bfloat16
I32 = jnp.int32

NORM_EPS = 1e-6
LANES = 128
HEAD_W = 128
N_HEADS = 8
GLA_CHUNK = 64
GLA_LEVELS = (32, 16, 8, 4, 2, 1)
ROPE_BASE = 10000.0
GRID_W = 64
TOKEN_BLOCK = 256
VMEM_LIMIT = 56 * 1024 * 1024

_NT = (((1,), (1,)), ((), ()))
_TN = (((0,), (0,)), ((), ()))


def _pick(n, cands):
    for c in cands:
        if n % c == 0:
            return c
    raise ValueError(f"no tile for {n} in {cands}")


def _params(sem):
    return pltpu.CompilerParams(dimension_semantics=sem, vmem_limit_bytes=VMEM_LIMIT)


def _sigmoid(x):
    return 1.0 / (1.0 + jnp.exp(-x))


def _norm_mod(x, g, scale2, shift2, row0, n_lat):
    ms = jnp.mean(x * x, axis=-1, keepdims=True)
    y = x * lax.rsqrt(ms + NORM_EPS) * g
    rows = row0 + lax.broadcasted_iota(I32, (x.shape[0], 1), 0)
    is_ctx = rows >= n_lat
    sc = jnp.where(is_ctx, scale2[0:1], scale2[1:2])
    sh = jnp.where(is_ctx, shift2[0:1], shift2[1:2])
    return y * sc + sh


def _ada_kernel(c_ref, w_ref, b_ref, o_ref):
    c = c_ref[...]
    s = c * _sigmoid(c)
    o_ref[0] = jnp.dot(s, w_ref[0], precision=lax.Precision.HIGHEST,
                       preferred_element_type=F32) + b_ref[0]


def _ada(cvec, ada_w, ada_b):
    depth, d, n = ada_w.shape
    rows = cvec.shape[0]
    tn = _pick(n, (1024, 512, 256, 128))
    return pl.pallas_call(
        _ada_kernel,
        grid=(depth, n // tn),
        in_specs=[pl.BlockSpec((rows, d), lambda l, j: (0, 0)),
                  pl.BlockSpec((1, d, tn), lambda l, j: (l, 0, j)),
                  pl.BlockSpec((1, 1, tn), lambda l, j: (l, 0, j))],
        out_specs=pl.BlockSpec((1, rows, tn), lambda l, j: (l, 0, j)),
        out_shape=jax.ShapeDtypeStruct((depth, rows, n), F32),
        compiler_params=_params(("arbitrary", "arbitrary")),
        name="adaln",
    )(cvec, ada_w, ada_b.reshape(depth, 1, n))


def _inproj_kernel(x_ref, g_ref, sc_ref, sh_ref, w_ref, *rest, tm, n_lat, rope):
    if rope:
        cos_ref, sin_ref, o_ref, h_ref = rest
    else:
        o_ref, h_ref = rest
    i = pl.program_id(1)
    n = pl.program_id(2)

    @pl.when(n == 0)
    def _():
        h = _norm_mod(x_ref[0], g_ref[...], sc_ref[0], sh_ref[0], i * tm, n_lat)
        h_ref[...] = h.astype(BF16)

    acc = jnp.dot(h_ref[...], w_ref[...], preferred_element_type=F32)
    if not rope:
        o_ref[0] = acc.astype(o_ref.dtype)
    else:
        @pl.when(n < 2)
        def _():
            cos = cos_ref[0]
            sin = sin_ref[0]
            lane = lax.broadcasted_iota(I32, (1, HEAD_W), 1)
            first = (lane % 64) < 32
            for hd in range(acc.shape[1] // HEAD_W):
                a = acc[:, hd * HEAD_W:(hd + 1) * HEAD_W]
                rot = jnp.where(first, pltpu.roll(a, HEAD_W - 32, 1), pltpu.roll(a, 32, 1))
                o_ref[0, :, hd * HEAD_W:(hd + 1) * HEAD_W] = (a * cos + rot * sin).astype(o_ref.dtype)

        @pl.when(n >= 2)
        def _():
            col = lax.broadcasted_iota(I32, (1, acc.shape[1]), 1)
            o_ref[0] = jnp.where(col % (2 * HEAD_W) >= HEAD_W, 1.0, acc).astype(o_ref.dtype)


def _inproj(x, g, scale2, shift2, w_bf16, n_lat, out_dtype, rope_tabs=None):
    b, r, d = x.shape
    n = w_bf16.shape[1]
    tm = _pick(r, (768, 512, 384, 256, 128))
    tn = 1024
    rope = rope_tabs is not None
    in_specs = [pl.BlockSpec((1, tm, d), lambda bb, i, j: (bb, i, 0)),
                pl.BlockSpec((1, d), lambda bb, i, j: (0, 0)),
                pl.BlockSpec((1, 2, d), lambda bb, i, j: (bb, 0, 0)),
                pl.BlockSpec((1, 2, d), lambda bb, i, j: (bb, 0, 0)),
                pl.BlockSpec((d, tn), lambda bb, i, j: (0, j))]
    args = [x, g.reshape(1, d), scale2, shift2, w_bf16]
    if rope:
        cos_t, sin_t = rope_tabs
        spec = pl.BlockSpec((1, tm, HEAD_W), lambda bb, i, j: (jnp.minimum(j, 1), i, 0))
        in_specs += [spec, spec]
        args += [cos_t, sin_t]
    return pl.pallas_call(
        functools.partial(_inproj_kernel, tm=tm, n_lat=n_lat, rope=rope),
        grid=(b, r // tm, n // tn),
        in_specs=in_specs,
        out_specs=pl.BlockSpec((1, tm, tn), lambda bb, i, j: (bb, i, j)),
        out_shape=jax.ShapeDtypeStruct((b, r, n), out_dtype),
        scratch_shapes=[pltpu.VMEM((tm, d), BF16)],
        compiler_params=_params(("arbitrary", "arbitrary", "arbitrary")),
        name="inproj_rope" if rope else "inproj",
    )(*args)


def _gla_consts(reverse):
    c = GLA_CHUNK
    nl = len(GLA_LEVELS)
    t = np.arange(c)
    p = c - 1 - t if reverse else t
    tri = (p[None, :] <= p[:, None]).astype(np.float32)
    rowsel = np.zeros((nl, c, HEAD_W), np.float32)
    masks = np.zeros((nl + 1, c, c), np.float32)
    for li, s in enumerate(GLA_LEVELS):
        blk = p // (2 * s)
        second = (p // s) % 2 == 1
        rowsel[li] = second[:, None]
        masks[li] = (blk[:, None] == blk[None, :]) & second[:, None] & (~second)[None, :]
    masks[nl] = np.eye(c)
    return tri, rowsel, masks


def _level_ref(b, s, reverse):
    c = b.shape[0]
    at = s if reverse else s - 1
    if 2 * s >= 8:
        pieces = [jnp.broadcast_to(b[blk * 2 * s + at:blk * 2 * s + at + 1], (2 * s, HEAD_W))
                  for blk in range(c // (2 * s))]
        return pieces[0] if len(pieces) == 1 else jnp.concatenate(pieces, axis=0)
    sub = lax.broadcasted_iota(I32, (8, HEAD_W), 0)
    outs = []
    for v in range(c // 8):
        acc = None
        for blk in range(8 // (2 * s)):
            row = 8 * v + blk * 2 * s + at
            cand = jnp.broadcast_to(b[row:row + 1], (8, HEAD_W))
            acc = cand if acc is None else jnp.where(sub >= blk * 2 * s, cand, acc)
        outs.append(acc)
    return jnp.concatenate(outs, axis=0)


def _gla_kernel(q_ref, f_ref, v_ref, lb_ref, tri_ref, rowsel_ref, masks_ref, o_ref, st_ref, *, tb, hg, reverse):
    c = GLA_CHUNK
    nl = len(GLA_LEVELS)
    nc = tb // c

    @pl.when(pl.program_id(2) == 0)
    def _():
        st_ref[...] = jnp.zeros_like(st_ref)

    tri = tri_ref[...]
    last = 0 if reverse else c - 1
    starts = [(nc - 1 - ci if reverse else ci) * c for ci in range(nc)]
    units = [(hd, r0) for hd in range(hg) for r0 in starts]

    qf, kk, vb, b3 = [], [], [], []
    for hd, r0 in units:
        lanes = slice(hd * HEAD_W, (hd + 1) * HEAD_W)
        lb = lb_ref[0, :, lanes]
        q = q_ref[0, r0:r0 + c, lanes]
        qf.append(q * _sigmoid(q))
        f = lb + (1.0 - lb) * _sigmoid(f_ref[0, r0:r0 + c, lanes])
        lf = jnp.log(f)
        kk.append(1.0 - f)
        vb.append(v_ref[0, r0:r0 + c, lanes].astype(BF16))
        hi = lf.astype(BF16)
        r1 = lf - hi.astype(F32)
        mid = r1.astype(BF16)
        lo = (r1 - mid.astype(F32)).astype(BF16)
        b3.append(jnp.dot(tri, jnp.concatenate([hi, mid, lo], axis=1), preferred_element_type=F32))
    bc = [x[:, :HEAD_W] + x[:, HEAD_W:2 * HEAD_W] + x[:, 2 * HEAD_W:] for x in b3]

    a = []
    for i in range(len(units)):
        ai = masks_ref[nl] * lax.dot_general(qf[i].astype(BF16), kk[i].astype(BF16), _NT,
                                             preferred_element_type=F32)
        for li, s in enumerate(GLA_LEVELS):
            e = jnp.exp(-jnp.abs(bc[i] - _level_ref(bc[i], s, reverse)))
            xb = (jnp.where(rowsel_ref[li] > 0, qf[i], kk[i]) * e).astype(BF16)
            ai = ai + masks_ref[li] * lax.dot_general(xb, xb, _NT, preferred_element_type=F32)
        a.append(ai.astype(BF16))

    o_intra = [jnp.dot(a[i], vb[i], preferred_element_type=F32) for i in range(len(units))]
    upd = []
    for i in range(len(units)):
        kdec = (kk[i] * jnp.exp(bc[i][last:last + 1] - bc[i])).astype(BF16)
        upd.append(lax.dot_general(vb[i], kdec, _TN, preferred_element_type=F32))

    st = [st_ref[hd] for hd in range(hg)]
    for i, (hd, r0) in enumerate(units):
        qd = (qf[i] * jnp.exp(bc[i])).astype(BF16)
        o_ref[0, r0:r0 + c, hd * HEAD_W:(hd + 1) * HEAD_W] = o_intra[i] + lax.dot_general(
            qd, st[hd].astype(BF16), _NT, preferred_element_type=F32)
        st[hd] = st[hd] * jnp.exp(bc[i][last:last + 1]) + upd[i]
    for hd in range(hg):
        st_ref[hd] = st[hd]


def _gla(proj, lb, n_lat, reverse):
    b, r, _ = proj.shape
    tb = TOKEN_BLOCK
    assert n_lat % tb == 0 and r % tb == 0
    nblk = r // tb
    nlb = n_lat // tb
    ncb = nblk - nlb
    tri, rowsel, masks = _gla_consts(reverse)
    nl = len(GLA_LEVELS)
    c = GLA_CHUNK
    hg = 8
    wblk = hg * HEAD_W
    fcol = N_HEADS // hg * (2 if reverse else 1)

    def blk(n):
        if reverse:
            return jnp.where(n < ncb, nblk - 1 - n, nlb - 1 - (n - ncb))
        return jnp.where(n < ncb, nlb + n, n - ncb)

    return pl.pallas_call(
        functools.partial(_gla_kernel, tb=tb, hg=hg, reverse=reverse),
        grid=(b, N_HEADS // hg, nblk),
        in_specs=[pl.BlockSpec((1, tb, wblk), lambda bb, h, n: (bb, blk(n), h)),
                  pl.BlockSpec((1, tb, wblk), lambda bb, h, n: (bb, blk(n), fcol + h)),
                  pl.BlockSpec((1, tb, wblk), lambda bb, h, n: (bb, blk(n), 3 * N_HEADS // hg + h)),
                  pl.BlockSpec((1, 1, wblk), lambda bb, h, n: (h, 0, 0)),
                  pl.BlockSpec((c, c), lambda bb, h, n: (0, 0)),
                  pl.BlockSpec((nl, c, HEAD_W), lambda bb, h, n: (0, 0, 0)),
                  pl.BlockSpec((nl + 1, c, c), lambda bb, h, n: (0, 0, 0))],
        out_specs=pl.BlockSpec((1, tb, wblk), lambda bb, h, n: (bb, blk(n), h)),
        out_shape=jax.ShapeDtypeStruct((b, r, N_HEADS * HEAD_W), F32),
        scratch_shapes=[pltpu.VMEM((hg, HEAD_W, HEAD_W), F32)],
        compiler_params=_params(("arbitrary", "arbitrary", "arbitrary")),
        name="gla_bwd" if reverse else "gla_fwd",
    )(proj, proj, proj, lb.reshape(N_HEADS // hg, 1, wblk), jnp.asarray(tri, BF16),
      jnp.asarray(rowsel), jnp.asarray(masks))


def _hgrn_out_kernel(of_ref, ob_ref, g_ref, ng_ref, w_ref, x_ref, gate_ref, o_ref, y_ref, *, tm, n_lat):
    i = pl.program_id(1)
    o = of_ref[0] + ob_ref[0]
    for h in range(N_HEADS):
        sl = slice(h * HEAD_W, (h + 1) * HEAD_W)
        oh = o[:, sl]
        ms = jnp.mean(oh * oh, axis=-1, keepdims=True)
        g = g_ref[0, :, sl]
        y_ref[:, sl] = (oh * lax.rsqrt(ms + NORM_EPS) * ng_ref[:, sl] * (g * _sigmoid(g))).astype(BF16)
    y = jnp.dot(y_ref[...], w_ref[...], preferred_element_type=F32)
    rows = i * tm + lax.broadcasted_iota(I32, (tm, 1), 0)
    gate = jnp.where(rows >= n_lat, gate_ref[0, 0:1], gate_ref[0, 1:2])
    o_ref[0] = x_ref[0] + gate * y


def _hgrn_out(o_f, o_b, proj, norm_g, w_bf16, x, gate2, n_lat):
    b, r, d = x.shape
    tm = _pick(r, (384, 256, 128))
    return pl.pallas_call(
        functools.partial(_hgrn_out_kernel, tm=tm, n_lat=n_lat),
        grid=(b, r // tm),
        in_specs=[pl.BlockSpec((1, tm, d), lambda bb, i: (bb, i, 0)),
                  pl.BlockSpec((1, tm, d), lambda bb, i: (bb, i, 0)),
                  pl.BlockSpec((1, tm, d), lambda bb, i: (bb, i, 4)),
                  pl.BlockSpec((1, d), lambda bb, i: (0, 0)),
                  pl.BlockSpec((d, d), lambda bb, i: (0, 0)),
                  pl.BlockSpec((1, tm, d), lambda bb, i: (bb, i, 0)),
                  pl.BlockSpec((1, 2, d), lambda bb, i: (bb, 0, 0))],
        out_specs=pl.BlockSpec((1, tm, d), lambda bb, i: (bb, i, 0)),
        out_shape=jax.ShapeDtypeStruct((b, r, d), F32),
        scratch_shapes=[pltpu.VMEM((tm, d), BF16)],
        compiler_params=_params(("arbitrary", "arbitrary")),
        name="hgrn_out",
    )(o_f, o_b, proj, jnp.tile(norm_g, N_HEADS).reshape(1, d), w_bf16, x, gate2)


def _attn_kernel(q_ref, k_ref, v_ref, lam_ref, g_ref, o_ref, *, tq, tk, lam_init):
    nk = k_ref.shape[1] // tk
    q = q_ref[0]
    lane = lax.broadcasted_iota(I32, (1, HEAD_W), 1)
    zero = jnp.zeros_like(q)
    qc = [jnp.where(lane < 64, q, zero), jnp.where(lane >= 64, q, zero)]

    def scores(u):
        comp, ki = u
        return lax.dot_general(qc[comp], k_ref[0, ki * tk:(ki + 1) * tk, :], _NT, preferred_element_type=F32)

    units = [(comp, ki) for ki in range(nk) for comp in range(2)]
    m = [jnp.full((tq, LANES), -jnp.inf, F32)] * 2
    acc = [jnp.zeros((tq, 2 * HEAD_W), F32)] * 2
    s_next = scores(units[0])
    for idx, (comp, ki) in enumerate(units):
        s = s_next
        if idx + 1 < len(units):
            s_next = scores(units[idx + 1])
        m_new = jnp.maximum(m[comp], jnp.max(s, axis=-1, keepdims=True))
        p = jnp.exp2(s - jnp.tile(m_new, (1, tk // LANES)))
        alpha = jnp.exp2(m[comp] - m_new)
        v1 = v_ref[0, ki * tk:(ki + 1) * tk, :]
        acc[comp] = jnp.tile(alpha, (1, 2)) * acc[comp] + jnp.dot(p.astype(BF16), v1, preferred_element_type=F32)
        m[comp] = m_new
    out = [a[:, :HEAD_W] / a[:, HEAD_W:] for a in acc]

    lv = lam_ref[...]
    s01 = jnp.sum(lv[0:1] * lv[1:2], axis=-1, keepdims=True)
    s23 = jnp.sum(lv[2:3] * lv[3:4], axis=-1, keepdims=True)
    lam = jnp.exp(s01) - jnp.exp(s23) + lam_init
    o = out[0] - lam * out[1]
    ms = jnp.mean(o * o, axis=-1, keepdims=True)
    o_ref[0] = (o * lax.rsqrt(ms + NORM_EPS) * g_ref[...] * (1.0 - lam_init)).astype(o_ref.dtype)


def _diff_attn(qkv, lam_vecs, subln_g, t, lam_init):
    b, r, _ = qkv.shape
    tq = _pick(t, (512, 256, 128))
    tk = _pick(r, (768, 1280, 512, 256))
    return pl.pallas_call(
        functools.partial(_attn_kernel, tq=tq, tk=tk, lam_init=lam_init),
        grid=(b, N_HEADS, t // tq),
        in_specs=[pl.BlockSpec((1, tq, HEAD_W), lambda bb, h, i: (bb, i, h)),
                  pl.BlockSpec((1, r, HEAD_W), lambda bb, h, i: (bb, 0, N_HEADS + h)),
                  pl.BlockSpec((1, r, 2 * HEAD_W), lambda bb, h, i: (bb, 0, N_HEADS + h)),
                  pl.BlockSpec(lam_vecs.shape, lambda bb, h, i: (0, 0)),
                  pl.BlockSpec((1, HEAD_W), lambda bb, h, i: (0, 0))],
        out_specs=pl.BlockSpec((1, tq, HEAD_W), lambda bb, h, i: (bb, i, h)),
        out_shape=jax.ShapeDtypeStruct((b, t, N_HEADS * HEAD_W), BF16),
        compiler_params=_params(("arbitrary", "arbitrary", "arbitrary")),
        name="diff_attn",
    )(qkv, qkv, qkv, lam_vecs.astype(F32), subln_g.reshape(1, HEAD_W).astype(F32))


def _outproj_kernel(a_ref, w_ref, x_ref, gate_ref, o_ref):
    y = jnp.dot(a_ref[0], w_ref[...], preferred_element_type=F32)
    o_ref[0] = x_ref[0] + gate_ref[0] * y


def _outproj(a, w_bf16, x_cat, gate):
    b, t, d = a.shape
    tm = _pick(t, (512, 256, 128))
    return pl.pallas_call(
        _outproj_kernel,
        grid=(b, t // tm),
        in_specs=[pl.BlockSpec((1, tm, d), lambda bb, i: (bb, i, 0)),
                  pl.BlockSpec((d, d), lambda bb, i: (0, 0)),
                  pl.BlockSpec((1, tm, d), lambda bb, i: (bb, i, 0)),
                  pl.BlockSpec((1, 1, d), lambda bb, i: (bb, 0, 0))],
        out_specs=pl.BlockSpec((1, tm, d), lambda bb, i: (bb, i, 0)),
        out_shape=jax.ShapeDtypeStruct((b, t, d), F32),
        compiler_params=_params(("arbitrary", "arbitrary")),
        name="outproj",
    )(a, w_bf16, x_cat, gate)


def _moe_pre_kernel(x_ref, g_ref, sc_ref, sh_ref, wr_ref, h_ref, aff_ref, *, tm, n_lat):
    i = pl.program_id(1)
    h = _norm_mod(x_ref[0], g_ref[...], sc_ref[0], sh_ref[0], i * tm, n_lat)
    h_ref[0] = h.astype(BF16)
    logits = lax.dot_general(wr_ref[...], h, _NT, precision=lax.Precision.HIGHEST,
                             preferred_element_type=F32)
    mx = jnp.max(logits, axis=0, keepdims=True)
    ex = jnp.exp(logits - mx)
    aff_ref[0] = ex / jnp.sum(ex, axis=0, keepdims=True)


def _moe_pre(x, g, scale2, shift2, w_router, n_lat):
    b, r, d = x.shape
    ne = w_router.shape[1]
    tm = _pick(r, (768, 512, 384, 256, 128))
    return pl.pallas_call(
        functools.partial(_moe_pre_kernel, tm=tm, n_lat=n_lat),
        grid=(b, r // tm),
        in_specs=[pl.BlockSpec((1, tm, d), lambda bb, i: (bb, i, 0)),
                  pl.BlockSpec((1, d), lambda bb, i: (0, 0)),
                  pl.BlockSpec((1, 2, d), lambda bb, i: (bb, 0, 0)),
                  pl.BlockSpec((1, 2, d), lambda bb, i: (bb, 0, 0)),
                  pl.BlockSpec((ne, d), lambda bb, i: (0, 0))],
        out_specs=[pl.BlockSpec((1, tm, d), lambda bb, i: (bb, i, 0)),
                   pl.BlockSpec((1, ne, tm), lambda bb, i: (bb, 0, i))],
        out_shape=[jax.ShapeDtypeStruct((b, r, d), BF16),
                   jax.ShapeDtypeStruct((b, ne, r), F32)],
        compiler_params=_params(("arbitrary", "arbitrary")),
        name="moe_pre",
    )(x, g.reshape(1, d), scale2, shift2, w_router.T)


def _topk_kernel(aff_ref, tri_ref, pos_ref, gsel_ref, r0_ref, *, cap):
    ne, t = aff_ref.shape[1], aff_ref.shape[2]
    tbk = TOKEN_BLOCK
    nblk = t // tbk
    bits = pltpu.bitcast(aff_ref[0], I32)
    thr = jnp.zeros((ne, 1), I32)
    for bit in range(30, -1, -1):
        cand = thr | (1 << bit)
        cnt = jnp.sum(jnp.where(bits >= cand, 1.0, 0.0), axis=1, keepdims=True)
        thr = jnp.where(cnt >= cap, cand, thr)
    n_gt = jnp.sum(jnp.where(bits > thr, 1.0, 0.0), axis=1, keepdims=True)
    need = cap - n_gt
    tri = tri_ref[...]
    carry_eq = jnp.zeros((ne, 1), F32)
    carry_sel = jnp.zeros((ne, 1), F32)
    r0_ref[0] = jnp.full((ne, LANES), cap, I32)
    for j in range(nblk):
        sl = slice(j * tbk, (j + 1) * tbk)
        a = aff_ref[0, :, sl]
        bj = pltpu.bitcast(a, I32)
        eq = jnp.where(bj == thr, 1.0, 0.0)
        gt = jnp.where(bj > thr, 1.0, 0.0)
        pe = jnp.dot(eq.astype(BF16), tri, preferred_element_type=F32) + carry_eq
        sel = gt + eq * jnp.where(pe - eq < need, 1.0, 0.0)
        ps = jnp.dot(sel.astype(BF16), tri, preferred_element_type=F32) + carry_sel
        pos_ref[0, :, sl] = jnp.where(sel > 0, ps - 1.0, -1.0).astype(I32)
        gsel_ref[0, :, sl] = a * sel
        r0_ref[0, :, j:j + 1] = carry_sel.astype(I32)
        carry_eq = pe[:, tbk - 1:tbk]
        carry_sel = ps[:, tbk - 1:tbk]


def _topk(aff, cap, t, tok_off):
    b, ne, _ = aff.shape
    tbk = TOKEN_BLOCK
    assert t % tbk == 0 and t // tbk < LANES and tok_off % t == 0
    tri = jnp.asarray(np.triu(np.ones((tbk, tbk), np.float32)), BF16)
    return pl.pallas_call(
        functools.partial(_topk_kernel, cap=cap),
        grid=(b,),
        in_specs=[pl.BlockSpec((1, ne, t), lambda bb: (bb, 0, tok_off // t)),
                  pl.BlockSpec((tbk, tbk), lambda bb: (0, 0))],
        out_specs=[pl.BlockSpec((1, ne, t), lambda bb: (bb, 0, 0)),
                   pl.BlockSpec((1, ne, t), lambda bb: (bb, 0, 0)),
                   pl.BlockSpec((1, ne, LANES), lambda bb: (bb, 0, 0))],
        out_shape=[jax.ShapeDtypeStruct((b, ne, t), I32),
                   jax.ShapeDtypeStruct((b, ne, t), F32),
                   jax.ShapeDtypeStruct((b, ne, LANES), I32)],
        compiler_params=_params(("arbitrary",)),
        name="topk",
    )(aff, tri)


def _windows(r0_ref, row, j, w, cap, align):
    lo = r0_ref[row, j]
    hi = r0_ref[row, j + 1]
    off = pl.multiple_of(jnp.minimum((lo // align) * align, cap - w), align)
    return off, (jnp.maximum(hi - (off + w), 0) + w - 1) // w


def _gather_kernel(r0_ref, pos_ref, g_ref, h_ref, xs_ref, gs_ref, acc_ref, gacc_ref, *, w, nblk, ne, cap):
    b, e = pl.program_id(0), pl.program_id(1)
    row = b * ne + e
    tbk = TOKEN_BLOCK
    acc_ref[...] = jnp.zeros_like(acc_ref)
    gacc_ref[...] = jnp.zeros_like(gacc_ref)
    sub = lax.broadcasted_iota(I32, (w, tbk), 0)

    def window(j, off, first_rank):
        rank = off + sub
        if first_rank is not None:
            rank = jnp.where(rank >= first_rank, rank, -2)
        hit = pos_ref[0, 0, j:j + 1, :] == rank
        oh = jnp.where(hit, 1.0, 0.0).astype(BF16)
        acc_ref[pl.ds(off, w), :] += jnp.dot(oh, h_ref[0, j * tbk:(j + 1) * tbk, :], preferred_element_type=F32)
        gsum = jnp.sum(jnp.where(hit, g_ref[0, 0, j:j + 1, :], 0.0), axis=1, keepdims=True)
        gacc_ref[pl.ds(off, w), :] += jnp.broadcast_to(gsum, (w, LANES))

    for j in range(nblk):
        off, n_more = _windows(r0_ref, row, j, w, cap, 8)
        window(j, off, None)

        def more(i, _, j=j, off=off):
            start = off + i * w
            window(j, pl.multiple_of(jnp.minimum(start, cap - w), 8), start)
            return 0

        lax.fori_loop(1, n_more + 1, more, 0)
    xs_ref[0, 0] = acc_ref[...].astype(BF16)
    gs_ref[0, 0] = gacc_ref[...]


def _gather(r0, pos, gate, hb, cap, tok_off):
    b, ne, t = pos.shape
    d = hb.shape[2]
    assert tok_off % t == 0
    tbk = TOKEN_BLOCK
    nblk = t // tbk
    w = min(cap, 64)
    pos4 = pos.reshape(b, ne, nblk, tbk)
    g4 = gate.reshape(b, ne, nblk, tbk)
    grid_spec = pltpu.PrefetchScalarGridSpec(
        num_scalar_prefetch=1,
        grid=(b, ne),
        in_specs=[pl.BlockSpec((1, 1, nblk, tbk), lambda bb, e, r: (bb, e, 0, 0)),
                  pl.BlockSpec((1, 1, nblk, tbk), lambda bb, e, r: (bb, e, 0, 0)),
                  pl.BlockSpec((1, t, d), lambda bb, e, r: (bb, tok_off // t, 0))],
        out_specs=[pl.BlockSpec((1, 1, cap, d), lambda bb, e, r: (e, bb, 0, 0)),
                   pl.BlockSpec((1, 1, cap, LANES), lambda bb, e, r: (e, bb, 0, 0))],
        scratch_shapes=[pltpu.VMEM((cap, d), F32), pltpu.VMEM((cap, LANES), F32)],
    )
    return pl.pallas_call(
        functools.partial(_gather_kernel, w=w, nblk=nblk, ne=ne, cap=cap),
        grid_spec=grid_spec,
        out_shape=[jax.ShapeDtypeStruct((ne, b, cap, d), BF16),
                   jax.ShapeDtypeStruct((ne, b, cap, LANES), F32)],
        compiler_params=_params(("arbitrary", "arbitrary")),
        name="moe_gather",
    )(r0.reshape(b * ne, LANES), pos4, g4, hb)


def _ffn_kernel(*refs, nb, caps, rcs):
    ng = len(caps)
    ins, (wg_ref, wu_ref, wd_ref) = refs[:3 * ng], refs[3 * ng:3 * ng + 3]
    y_refs, acc_refs = refs[3 * ng + 3:4 * ng + 3], refs[4 * ng + 3:]
    f = pl.program_id(1)
    nf = pl.num_programs(1)

    @pl.when(f == 0)
    def _():
        for acc_ref in acc_refs:
            acc_ref[...] = jnp.zeros_like(acc_ref)

    wg = wg_ref[0, 0].astype(BF16)
    wu = wu_ref[0, 0].astype(BF16)
    wd = wd_ref[0, 0].astype(BF16)
    chunks = [(g, c0, rcs[g]) for g in range(ng) for c0 in range(0, nb * caps[g], rcs[g])]

    def gate_up(chunk):
        g, c0, rc = chunk
        x = ins[3 * g][0, c0:c0 + rc, :]
        return jnp.dot(x, wg, preferred_element_type=F32), jnp.dot(x, wu, preferred_element_type=F32)

    nxt = gate_up(chunks[0])
    for i, (g, c0, rc) in enumerate(chunks):
        a, u = nxt
        if i + 1 < len(chunks):
            nxt = gate_up(chunks[i + 1])
        hm = (a * _sigmoid(a) * u).astype(BF16)
        acc_refs[g][c0:c0 + rc, :] += jnp.dot(hm, wd, preferred_element_type=F32)

    @pl.when(f == nf - 1)
    def _():
        for g in range(ng):
            gs_ref, gt_ref = ins[3 * g + 1], ins[3 * g + 2]
            for bb in range(nb):
                sl = slice(bb * caps[g], (bb + 1) * caps[g])
                y_refs[g][0, sl, :] = (acc_refs[g][sl, :] * gs_ref[0, sl, 0:1] * gt_ref[bb]).astype(BF16)


def _ffn(groups, w_gate, w_up, w_down, layer):
    ne, b, _, d = groups[0][0].shape
    fdim = w_gate.shape[3]
    tf = _pick(fdim, (256, 128))
    caps = [xs.shape[2] for xs, _, _ in groups]
    rcs = [_pick(b * cap, (1024, 512, 256, 128, 64)) for cap in caps]
    in_specs, args = [], []
    for (xs, gs, gate_f), cap in zip(groups, caps):
        rows = b * cap
        in_specs += [pl.BlockSpec((1, rows, d), lambda e, f: (e, 0, 0)),
                     pl.BlockSpec((1, rows, LANES), lambda e, f: (e, 0, 0)),
                     pl.BlockSpec((b, 1, d), lambda e, f: (0, 0, 0))]
        args += [xs.reshape(ne, rows, d), gs.reshape(ne, rows, LANES), gate_f]
    in_specs += [pl.BlockSpec((1, 1, d, tf), lambda e, f: (layer, e, 0, f)),
                 pl.BlockSpec((1, 1, d, tf), lambda e, f: (layer, e, 0, f)),
                 pl.BlockSpec((1, 1, tf, d), lambda e, f: (layer, e, f, 0))]
    ys = pl.pallas_call(
        functools.partial(_ffn_kernel, nb=b, caps=caps, rcs=rcs),
        grid=(ne, fdim // tf),
        in_specs=in_specs,
        out_specs=[pl.BlockSpec((1, b * cap, d), lambda e, f: (e, 0, 0)) for cap in caps],
        out_shape=[jax.ShapeDtypeStruct((ne, b * cap, d), BF16) for cap in caps],
        scratch_shapes=[pltpu.VMEM((b * cap, d), F32) for cap in caps],
        compiler_params=_params(("arbitrary", "arbitrary")),
        name="moe_ffn",
    )(*args, w_gate, w_up, w_down)
    return [y.reshape(ne, b, cap, d) for y, cap in zip(ys, caps)]


def _combine_kernel(r0_ref, pos_ref, y_ref, x_ref, o_ref, acc_ref, *, w, ne, cap):
    b, j = pl.program_id(0), pl.program_id(2)
    tbk = TOKEN_BLOCK
    sub = lax.broadcasted_iota(I32, (w, tbk), 0)
    align = 16

    def window(e, off, first_rank):
        rank = off + sub
        if first_rank is not None:
            rank = jnp.where(rank >= first_rank, rank, -2)
        oh = jnp.where(pos_ref[0, 0, e:e + 1, :] == rank, 1.0, 0.0).astype(BF16)
        return lax.dot_general(oh, y_ref[e, 0, pl.ds(off, w), :], _TN, preferred_element_type=F32)

    acc = x_ref[0]
    more = []
    for e in range(ne):
        off, n_more = _windows(r0_ref, b * ne + e, j, w, cap, align)
        acc = acc + window(e, off, None)
        more.append((off, n_more))
    acc_ref[...] = acc
    for e, (off, n_more) in enumerate(more):
        def extra(i, _, e=e, off=off):
            start = off + i * w
            acc_ref[...] += window(e, pl.multiple_of(jnp.minimum(start, cap - w), align), start)
            return 0

        lax.fori_loop(1, n_more + 1, extra, 0)
    o_ref[0] = acc_ref[...]


def _combine(r0, pos, y, x, row_off):
    ne, b, cap, d = y.shape
    t = pos.shape[2]
    tbk = TOKEN_BLOCK
    nblk = t // tbk
    assert row_off % tbk == 0
    boff = row_off // tbk
    w = min(cap, 256)
    dh = d // 2
    pos4 = pos.reshape(b, ne, nblk, tbk).transpose(0, 2, 1, 3)
    grid_spec = pltpu.PrefetchScalarGridSpec(
        num_scalar_prefetch=1,
        grid=(b, 2, nblk),
        in_specs=[pl.BlockSpec((1, 1, ne, tbk), lambda bb, c, j, r: (bb, j, 0, 0)),
                  pl.BlockSpec((ne, 1, cap, dh), lambda bb, c, j, r: (0, bb, 0, c)),
                  pl.BlockSpec((1, tbk, dh), lambda bb, c, j, r: (bb, j + boff, c))],
        out_specs=pl.BlockSpec((1, tbk, dh), lambda bb, c, j, r: (bb, j, c)),
        scratch_shapes=[pltpu.VMEM((tbk, dh), F32)],
    )
    return pl.pallas_call(
        functools.partial(_combine_kernel, w=w, ne=ne, cap=cap),
        grid_spec=grid_spec,
        out_shape=jax.ShapeDtypeStruct((b, t, d), F32),
        compiler_params=_params(("arbitrary", "arbitrary", "arbitrary")),
        name="moe_combine",
    )(r0.reshape(b * ne, LANES), pos4, y, x)


def _final_kernel(x_ref, g_ref, o_ref):
    x = x_ref[0]
    ms = jnp.mean(x * x, axis=-1, keepdims=True)
    o_ref[0] = x * lax.rsqrt(ms + NORM_EPS) * g_ref[...]


def _final(x, g):
    b, t, d = x.shape
    tm = _pick(t, (512, 256, 128))
    return pl.pallas_call(
        _final_kernel,
        grid=(b, t // tm),
        in_specs=[pl.BlockSpec((1, tm, d), lambda bb, i: (bb, i, 0)),
                  pl.BlockSpec((1, d), lambda bb, i: (0, 0))],
        out_specs=pl.BlockSpec((1, tm, d), lambda bb, i: (bb, i, 0)),
        out_shape=jax.ShapeDtypeStruct((b, t, d), F32),
        compiler_params=_params(("arbitrary", "arbitrary")),
        name="final_norm",
    )(x, g.reshape(1, d))


def _moe(aff, hb, x, sets, w_gate, w_up, w_down, layer):
    ne = aff.shape[1]
    routed, groups = [], []
    for off, t, gate_f in sets:
        cap = 2 * t // ne
        pos, gsel, r0 = _topk(aff, cap, t, off)
        xs, gs = _gather(r0, pos, gsel, hb, cap, off)
        routed.append((r0, pos, off))
        groups.append((xs, gs, gate_f))
    ys = _ffn(groups, w_gate, w_up, w_down, layer)
    return [_combine(r0, pos, y, x, off) for (r0, pos, off), y in zip(routed, ys)]


def _rope_tables(t, n_ctx):
    rows = t // GRID_W
    row = jnp.repeat(jnp.arange(rows, dtype=F32), GRID_W)
    col = jnp.tile(jnp.arange(GRID_W, dtype=F32), rows)
    pairs = HEAD_W // 8
    freq = ROPE_BASE ** (-jnp.arange(pairs, dtype=F32) / pairs)
    ang = jnp.concatenate([row[:, None] * freq, col[:, None] * freq], axis=-1)
    cos, sin = jnp.cos(ang), jnp.sin(ang)
    cos = jnp.tile(cos, (1, 4))
    sin = jnp.tile(jnp.concatenate([-sin, sin], axis=-1), (1, 2))
    cos = jnp.concatenate([cos, jnp.ones((n_ctx, HEAD_W), F32)], axis=0)
    sin = jnp.concatenate([sin, jnp.zeros((n_ctx, HEAD_W), F32)], axis=0)
    qs = (HEAD_W // 2) ** -0.5 * math.log2(math.e)
    return jnp.stack([cos * qs, cos]), jnp.stack([sin * qs, sin])


def _diff_w_ext(w):
    d = w.shape[0]
    wv = w[:, 2 * d:].reshape(d, N_HEADS, HEAD_W)
    wv = jnp.concatenate([wv, jnp.zeros_like(wv)], axis=-1).reshape(d, 2 * d)
    return jnp.concatenate([w[:, :2 * d], wv], axis=1).astype(BF16)


def kernel(x, c, ctx, c_ctx, ada_w, ada_b, norm_mix, norm_ffn, norm_final, hgrn_w_in, hgrn_lb_logits, hgrn_norm, hgrn_w_out, diff_w_in, diff_lambda, diff_subln, diff_w_out, moe_router, moe_w_gate, moe_w_up, moe_w_down):
    b, t, d = x.shape
    n_ctx = ctx.shape[1]
    depth = ada_w.shape[0]
    ne = moe_router.shape[2]
    assert depth == 2 and d == N_HEADS * HEAD_W

    cvec = jnp.concatenate([c, c_ctx[None, :], jnp.zeros((8 - b - 1, d), F32)], axis=0)
    mod = _ada(cvec, ada_w, ada_b)
    lower_bounds = jnp.cumsum(jax.nn.softmax(hgrn_lb_logits.astype(F32), axis=0), axis=0)

    def kinds(layer, k, plus_one=False):
        m = mod[layer, :, k * d:(k + 1) * d]
        v = jnp.stack([jnp.broadcast_to(m[b], (b, d)), m[:b]], axis=1)
        return 1.0 + v if plus_one else v

    x_cat = jnp.concatenate([x, ctx], axis=1)

    proj = _inproj(x_cat, norm_mix[0], kinds(0, 1, True), kinds(0, 0), hgrn_w_in[0].astype(BF16), t, F32)
    o_f = _gla(proj, lower_bounds[0], t, False)
    o_b = _gla(proj, lower_bounds[0], t, True)
    x_cat = _hgrn_out(o_f, o_b, proj, hgrn_norm[0], hgrn_w_out[0].astype(BF16), x_cat, kinds(0, 2), t)
    hb, aff = _moe_pre(x_cat, norm_ffn[0], kinds(0, 4, True), kinds(0, 3), moe_router[0], t)
    gate_f = kinds(0, 5)
    x_lat, x_ctx = _moe(aff, hb, x_cat, [(0, t, gate_f[:, 1:2]), (t, n_ctx, gate_f[:, 0:1])],
                        moe_w_gate, moe_w_up, moe_w_down, 0)
    x_cat = jnp.concatenate([x_lat, x_ctx], axis=1)

    lam_init = 0.8 - 0.6 * math.exp(-0.3 * 1)
    qkv = _inproj(x_cat, norm_mix[1], kinds(1, 1, True), kinds(1, 0), _diff_w_ext(diff_w_in[0]), t, BF16,
                  rope_tabs=_rope_tables(t, n_ctx))
    att = _diff_attn(qkv, diff_lambda[0], diff_subln[0], t, lam_init)
    x_lat = _outproj(att, diff_w_out[0].astype(BF16), x_cat, kinds(1, 2)[:, 1:2])
    hb, aff = _moe_pre(x_lat, norm_ffn[1], kinds(1, 4, True), kinds(1, 3), moe_router[1], t)
    (x_lat,) = _moe(aff, hb, x_lat, [(0, t, kinds(1, 5)[:, 1:2])], moe_w_gate, moe_w_up, moe_w_down, 1)
    return _final(x_lat, norm_final)
```

```python
import functools
import math

import jax
import jax.numpy as jnp
import numpy as np
from jax import lax
from jax.experimental import pallas as pl
from jax.experimental.pallas import tpu as pltpu

F32 = jnp.float32
BF16 = jnp.bfloat16
I32 = jnp.int32

NORM_EPS = 1e-6
LANES = 128
HEAD_W = 128
N_HEADS = 8
GLA_CHUNK = 64
GLA_LEVELS = (32, 16, 8, 4, 2, 1)
ROPE_BASE = 10000.0
GRID_W = 64
TOKEN_BLOCK = 256
VMEM_LIMIT = 56 * 1024 * 1024

_NT = (((1,), (1,)), ((), ()))
_TN = (((0,), (0,)), ((), ()))


def _pick(n, cands):
    for c in cands:
        if n % c == 0:
            return c
    raise ValueError(f"no tile for {n} in {cands}")


def _params(sem):
    return pltpu.CompilerParams(dimension_semantics=sem, vmem_limit_bytes=VMEM_LIMIT)


def _sigmoid(x):
    return 1.0 / (1.0 + jnp.exp(-x))


def _norm_mod(x, g, scale2, shift2, row0, n_lat):
    ms = jnp.mean(x * x, axis=-1, keepdims=True)
    y = x * lax.rsqrt(ms + NORM_EPS) * g
    rows = row0 + lax.broadcasted_iota(I32, (x.shape[0], 1), 0)
    is_ctx = rows >= n_lat
    sc = jnp.where(is_ctx, scale2[0:1], scale2[1:2])
    sh = jnp.where(is_ctx, shift2[0:1], shift2[1:2])
    return y * sc + sh


def _ada_kernel(c_ref, w_ref, b_ref, o_ref):
    c = c_ref[...]
    s = c * _sigmoid(c)
    o_ref[0] = jnp.dot(s, w_ref[0], precision=lax.Precision.HIGHEST,
                       preferred_element_type=F32) + b_ref[0]


def _ada(cvec, ada_w, ada_b):
    depth, d, n = ada_w.shape
    rows = cvec.shape[0]
    tn = _pick(n, (1024, 512, 256, 128))
    return pl.pallas_call(
        _ada_kernel,
        grid=(depth, n // tn),
        in_specs=[pl.BlockSpec((rows, d), lambda l, j: (0, 0)),
                  pl.BlockSpec((1, d, tn), lambda l, j: (l, 0, j)),
                  pl.BlockSpec((1, 1, tn), lambda l, j: (l, 0, j))],
        out_specs=pl.BlockSpec((1, rows, tn), lambda l, j: (l, 0, j)),
        out_shape=jax.ShapeDtypeStruct((depth, rows, n), F32),
        compiler_params=_params(("arbitrary", "arbitrary")),
        name="adaln",
    )(cvec, ada_w, ada_b.reshape(depth, 1, n))


def _inproj_kernel(x_ref, g_ref, sc_ref, sh_ref, w_ref, *rest, tm, n_lat, rope):
    if rope:
        cos_ref, sin_ref, o_ref, h_ref = rest
    else:
        o_ref, h_ref = rest
    i = pl.program_id(1)
    n = pl.program_id(2)

    @pl.when(n == 0)
    def _():
        h = _norm_mod(x_ref[0], g_ref[...], sc_ref[0], sh_ref[0], i * tm, n_lat)
        h_ref[...] = h.astype(BF16)

    acc = jnp.dot(h_ref[...], w_ref[...], preferred_element_type=F32)
    if not rope:
        o_ref[0] = acc.astype(o_ref.dtype)
    else:
        @pl.when(n < 2)
        def _():
            cos = cos_ref[0]
            sin = sin_ref[0]
            lane = lax.broadcasted_iota(I32, (1, HEAD_W), 1)
            first = (lane % 64) < 32
            for hd in range(acc.shape[1] // HEAD_W):
                a = acc[:, hd * HEAD_W:(hd + 1) * HEAD_W]
                rot = jnp.where(first, pltpu.roll(a, HEAD_W - 32, 1), pltpu.roll(a, 32, 1))
                o_ref[0, :, hd * HEAD_W:(hd + 1) * HEAD_W] = (a * cos + rot * sin).astype(o_ref.dtype)

        @pl.when(n >= 2)
        def _():
            col = lax.broadcasted_iota(I32, (1, acc.shape[1]), 1)
            o_ref[0] = jnp.where(col % (2 * HEAD_W) >= HEAD_W, 1.0, acc).astype(o_ref.dtype)


def _inproj(x, g, scale2, shift2, w_bf16, n_lat, out_dtype, rope_tabs=None):
    b, r, d = x.shape
    n = w_bf16.shape[1]
    tm = _pick(r, (768, 512, 384, 256, 128))
    tn = 1024
    rope = rope_tabs is not None
    in_specs = [pl.BlockSpec((1, tm, d), lambda bb, i, j: (bb, i, 0)),
                pl.BlockSpec((1, d), lambda bb, i, j: (0, 0)),
                pl.BlockSpec((1, 2, d), lambda bb, i, j: (bb, 0, 0)),
                pl.BlockSpec((1, 2, d), lambda bb, i, j: (bb, 0, 0)),
                pl.BlockSpec((d, tn), lambda bb, i, j: (0, j))]
    args = [x, g.reshape(1, d), scale2, shift2, w_bf16]
    if rope:
        cos_t, sin_t = rope_tabs
        spec = pl.BlockSpec((1, tm, HEAD_W), lambda bb, i, j: (jnp.minimum(j, 1), i, 0))
        in_specs += [spec, spec]
        args += [cos_t, sin_t]
    return pl.pallas_call(
        functools.partial(_inproj_kernel, tm=tm, n_lat=n_lat, rope=rope),
        grid=(b, r // tm, n // tn),
        in_specs=in_specs,
        out_specs=pl.BlockSpec((1, tm, tn), lambda bb, i, j: (bb, i, j)),
        out_shape=jax.ShapeDtypeStruct((b, r, n), out_dtype),
        scratch_shapes=[pltpu.VMEM((tm, d), BF16)],
        compiler_params=_params(("arbitrary", "arbitrary", "arbitrary")),
        name="inproj_rope" if rope else "inproj",
    )(*args)


def _gla_consts(reverse):
    c = GLA_CHUNK
    nl = len(GLA_LEVELS)
    t = np.arange(c)
    p = c - 1 - t if reverse else t
    tri = (p[None, :] <= p[:, None]).astype(np.float32)
    rowsel = np.zeros((nl, c, HEAD_W), np.float32)
    masks = np.zeros((nl + 1, c, c), np.float32)
    for li, s in enumerate(GLA_LEVELS):
        blk = p // (2 * s)
        second = (p // s) % 2 == 1
        rowsel[li] = second[:, None]
        masks[li] = (blk[:, None] == blk[None, :]) & second[:, None] & (~second)[None, :]
    masks[nl] = np.eye(c)
    return tri, rowsel, masks


def _level_ref(b, s, reverse):
    c = b.shape[0]
    at = s if reverse else s - 1
    if 2 * s >= 8:
        pieces = [jnp.broadcast_to(b[blk * 2 * s + at:blk * 2 * s + at + 1], (2 * s, HEAD_W))
                  for blk in range(c // (2 * s))]
        return pieces[0] if len(pieces) == 1 else jnp.concatenate(pieces, axis=0)
    sub = lax.broadcasted_iota(I32, (8, HEAD_W), 0)
    outs = []
    for v in range(c // 8):
        acc = None
        for blk in range(8 // (2 * s)):
            row = 8 * v + blk * 2 * s + at
            cand = jnp.broadcast_to(b[row:row + 1], (8, HEAD_W))
            acc = cand if acc is None else jnp.where(sub >= blk * 2 * s, cand, acc)
        outs.append(acc)
    return jnp.concatenate(outs, axis=0)


def _gla_kernel(q_ref, f_ref, v_ref, lb_ref, tri_ref, rowsel_ref, masks_ref, o_ref, st_ref, *, tb, hg, reverse):
    c = GLA_CHUNK
    nl = len(GLA_LEVELS)
    nc = tb // c

    @pl.when(pl.program_id(2) == 0)
    def _():
        st_ref[...] = jnp.zeros_like(st_ref)

    tri = tri_ref[...]
    last = 0 if reverse else c - 1
    starts = [(nc - 1 - ci if reverse else ci) * c for ci in range(nc)]
    units = [(hd, r0) for hd in range(hg) for r0 in starts]

    qf, kk, vb, b3 = [], [], [], []
    for hd, r0 in units:
        lanes = slice(hd * HEAD_W, (hd + 1) * HEAD_W)
        lb = lb_ref[0, :, lanes]
        q = q_ref[0, r0:r0 + c, lanes]
        qf.append(q * _sigmoid(q))
        f = lb + (1.0 - lb) * _sigmoid(f_ref[0, r0:r0 + c, lanes])
        lf = jnp.log(f)
        kk.append(1.0 - f)
        vb.append(v_ref[0, r0:r0 + c, lanes].astype(BF16))
        hi = lf.astype(BF16)
        r1 = lf - hi.astype(F32)
        mid = r1.astype(BF16)
        lo = (r1 - mid.astype(F32)).astype(BF16)
        b3.append(jnp.dot(tri, jnp.concatenate([hi, mid, lo], axis=1), preferred_element_type=F32))
    bc = [x[:, :HEAD_W] + x[:, HEAD_W:2 * HEAD_W] + x[:, 2 * HEAD_W:] for x in b3]

    a = []
    for i in range(len(units)):
        ai = masks_ref[nl] * lax.dot_general(qf[i].astype(BF16), kk[i].astype(BF16), _NT,
                                             preferred_element_type=F32)
        for li, s in enumerate(GLA_LEVELS):
            e = jnp.exp(-jnp.abs(bc[i] - _level_ref(bc[i], s, reverse)))
            xb = (jnp.where(rowsel_ref[li] > 0, qf[i], kk[i]) * e).astype(BF16)
            ai = ai + masks_ref[li] * lax.dot_general(xb, xb, _NT, preferred_element_type=F32)
        a.append(ai.astype(BF16))

    o_intra = [jnp.dot(a[i], vb[i], preferred_element_type=F32) for i in range(len(units))]
    upd = []
    for i in range(len(units)):
        kdec = (kk[i] * jnp.exp(bc[i][last:last + 1] - bc[i])).astype(BF16)
        upd.append(lax.dot_general(vb[i], kdec, _TN, preferred_element_type=F32))

    st = [st_ref[hd] for hd in range(hg)]
    for i, (hd, r0) in enumerate(units):
        qd = (qf[i] * jnp.exp(bc[i])).astype(BF16)
        o_ref[0, r0:r0 + c, hd * HEAD_W:(hd + 1) * HEAD_W] = o_intra[i] + lax.dot_general(
            qd, st[hd].astype(BF16), _NT, preferred_element_type=F32)
        st[hd] = st[hd] * jnp.exp(bc[i][last:last + 1]) + upd[i]
    for hd in range(hg):
        st_ref[hd] = st[hd]


def _gla(proj, lb, n_lat, reverse):
    b, r, _ = proj.shape
    tb = TOKEN_BLOCK
    assert n_lat % tb == 0 and r % tb == 0
    nblk = r // tb
    nlb = n_lat // tb
    ncb = nblk - nlb
    tri, rowsel, masks = _gla_consts(reverse)
    nl = len(GLA_LEVELS)
    c = GLA_CHUNK
    hg = 8
    wblk = hg * HEAD_W
    fcol = N_HEADS // hg * (2 if reverse else 1)

    def blk(n):
        if reverse:
            return jnp.where(n < ncb, nblk - 1 - n, nlb - 1 - (n - ncb))
        return jnp.where(n < ncb, nlb + n, n - ncb)

    return pl.pallas_call(
        functools.partial(_gla_kernel, tb=tb, hg=hg, reverse=reverse),
        grid=(b, N_HEADS // hg, nblk),
        in_specs=[pl.BlockSpec((1, tb, wblk), lambda bb, h, n: (bb, blk(n), h)),
                  pl.BlockSpec((1, tb, wblk), lambda bb, h, n: (bb, blk(n), fcol + h)),
                  pl.BlockSpec((1, tb, wblk), lambda bb, h, n: (bb, blk(n), 3 * N_HEADS // hg + h)),
                  pl.BlockSpec((1, 1, wblk), lambda bb, h, n: (h, 0, 0)),
                  pl.BlockSpec((c, c), lambda bb, h, n: (0, 0)),
                  pl.BlockSpec((nl, c, HEAD_W), lambda bb, h, n: (0, 0, 0)),
                  pl.BlockSpec((nl + 1, c, c), lambda bb, h, n: (0, 0, 0))],
        out_specs=pl.BlockSpec((1, tb, wblk), lambda bb, h, n: (bb, blk(n), h)),
        out_shape=jax.ShapeDtypeStruct((b, r, N_HEADS * HEAD_W), F32),
        scratch_shapes=[pltpu.VMEM((hg, HEAD_W, HEAD_W), F32)],
        compiler_params=_params(("arbitrary", "arbitrary", "arbitrary")),
        name="gla_bwd" if reverse else "gla_fwd",
    )(proj, proj, proj, lb.reshape(N_HEADS // hg, 1, wblk), jnp.asarray(tri, BF16),
      jnp.asarray(rowsel), jnp.asarray(masks))


def _hgrn_out_kernel(of_ref, ob_ref, g_ref, ng_ref, w_ref, x_ref, gate_ref, o_ref, y_ref, *, tm, n_lat):
    i = pl.program_id(1)
    o = of_ref[0] + ob_ref[0]
    for h in range(N_HEADS):
        sl = slice(h * HEAD_W, (h + 1) * HEAD_W)
        oh = o[:, sl]
        ms = jnp.mean(oh * oh, axis=-1, keepdims=True)
        g = g_ref[0, :, sl]
        y_ref[:, sl] = (oh * lax.rsqrt(ms + NORM_EPS) * ng_ref[:, sl] * (g * _sigmoid(g))).astype(BF16)
    y = jnp.dot(y_ref[...], w_ref[...], preferred_element_type=F32)
    rows = i * tm + lax.broadcasted_iota(I32, (tm, 1), 0)
    gate = jnp.where(rows >= n_lat, gate_ref[0, 0:1], gate_ref[0, 1:2])
    o_ref[0] = x_ref[0] + gate * y


def _hgrn_out(o_f, o_b, proj, norm_g, w_bf16, x, gate2, n_lat):
    b, r, d = x.shape
    tm = _pick(r, (384, 256, 128))
    return pl.pallas_call(
        functools.partial(_hgrn_out_kernel, tm=tm, n_lat=n_lat),
        grid=(b, r // tm),
        in_specs=[pl.BlockSpec((1, tm, d), lambda bb, i: (bb, i, 0)),
                  pl.BlockSpec((1, tm, d), lambda bb, i: (bb, i, 0)),
                  pl.BlockSpec((1, tm, d), lambda bb, i: (bb, i, 4)),
                  pl.BlockSpec((1, d), lambda bb, i: (0, 0)),
                  pl.BlockSpec((d, d), lambda bb, i: (0, 0)),
                  pl.BlockSpec((1, tm, d), lambda bb, i: (bb, i, 0)),
                  pl.BlockSpec((1, 2, d), lambda bb, i: (bb, 0, 0))],
        out_specs=pl.BlockSpec((1, tm, d), lambda bb, i: (bb, i, 0)),
        out_shape=jax.ShapeDtypeStruct((b, r, d), F32),
        scratch_shapes=[pltpu.VMEM((tm, d), BF16)],
        compiler_params=_params(("arbitrary", "arbitrary")),
        name="hgrn_out",
    )(o_f, o_b, proj, jnp.tile(norm_g, N_HEADS).reshape(1, d), w_bf16, x, gate2)


def _attn_kernel(q_ref, k_ref, v_ref, lam_ref, g_ref, o_ref, *, tq, tk, lam_init):
    nk = k_ref.shape[1] // tk
    q = q_ref[0]
    lane = lax.broadcasted_iota(I32, (1, HEAD_W), 1)
    zero = jnp.zeros_like(q)
    qc = [jnp.where(lane < 64, q, zero), jnp.where(lane >= 64, q, zero)]

    def scores(u):
        comp, ki = u
        return lax.dot_general(qc[comp], k_ref[0, ki * tk:(ki + 1) * tk, :], _NT, preferred_element_type=F32)

    units = [(comp, ki) for ki in range(nk) for comp in range(2)]
    m = [jnp.full((tq, LANES), -jnp.inf, F32)] * 2
    acc = [jnp.zeros((tq, 2 * HEAD_W), F32)] * 2
    s_next = scores(units[0])
    for idx, (comp, ki) in enumerate(units):
        s = s_next
        if idx + 1 < len(units):
            s_next = scores(units[idx + 1])
        m_new = jnp.maximum(m[comp], jnp.max(s, axis=-1, keepdims=True))
        p = jnp.exp2((s - jnp.tile(m_new, (1, tk // LANES))).astype(BF16))
        alpha = jnp.exp2(m[comp] - m_new)
        v1 = v_ref[0, ki * tk:(ki + 1) * tk, :]
        acc[comp] = jnp.tile(alpha, (1, 2)) * acc[comp] + jnp.dot(p, v1, preferred_element_type=F32)
        m[comp] = m_new
    out = [a[:, :HEAD_W] / a[:, HEAD_W:] for a in acc]

    lv = lam_ref[...]
    s01 = jnp.sum(lv[0:1] * lv[1:2], axis=-1, keepdims=True)
    s23 = jnp.sum(lv[2:3] * lv[3:4], axis=-1, keepdims=True)
    lam = jnp.exp(s01) - jnp.exp(s23) + lam_init
    o = out[0] - lam * out[1]
    ms = jnp.mean(o * o, axis=-1, keepdims=True)
    o_ref[0] = (o * lax.rsqrt(ms + NORM_EPS) * g_ref[...] * (1.0 - lam_init)).astype(o_ref.dtype)


def _diff_attn(qkv, lam_vecs, subln_g, t, lam_init):
    b, r, _ = qkv.shape
    tq = _pick(t, (512, 256, 128))
    tk = _pick(r, (768, 1280, 512, 256))
    return pl.pallas_call(
        functools.partial(_attn_kernel, tq=tq, tk=tk, lam_init=lam_init),
        grid=(b, N_HEADS, t // tq),
        in_specs=[pl.BlockSpec((1, tq, HEAD_W), lambda bb, h, i: (bb, i, h)),
                  pl.BlockSpec((1, r, HEAD_W), lambda bb, h, i: (bb, 0, N_HEADS + h)),
                  pl.BlockSpec((1, r, 2 * HEAD_W), lambda bb, h, i: (bb, 0, N_HEADS + h)),
                  pl.BlockSpec(lam_vecs.shape, lambda bb, h, i: (0, 0)),
                  pl.BlockSpec((1, HEAD_W), lambda bb, h, i: (0, 0))],
        out_specs=pl.BlockSpec((1, tq, HEAD_W), lambda bb, h, i: (bb, i, h)),
        out_shape=jax.ShapeDtypeStruct((b, t, N_HEADS * HEAD_W), BF16),
        compiler_params=_params(("arbitrary", "arbitrary", "arbitrary")),
        name="diff_attn",
    )(qkv, qkv, qkv, lam_vecs.astype(F32), subln_g.reshape(1, HEAD_W).astype(F32))


def _outproj_kernel(a_ref, w_ref, x_ref, gate_ref, o_ref):
    y = jnp.dot(a_ref[0], w_ref[...], preferred_element_type=F32)
    o_ref[0] = x_ref[0] + gate_ref[0] * y


def _outproj(a, w_bf16, x_cat, gate):
    b, t, d = a.shape
    tm = _pick(t, (512, 256, 128))
    return pl.pallas_call(
        _outproj_kernel,
        grid=(b, t // tm),
        in_specs=[pl.BlockSpec((1, tm, d), lambda bb, i: (bb, i, 0)),
                  pl.BlockSpec((d, d), lambda bb, i: (0, 0)),
                  pl.BlockSpec((1, tm, d), lambda bb, i: (bb, i, 0)),
                  pl.BlockSpec((1, 1, d), lambda bb, i: (bb, 0, 0))],
        out_specs=pl.BlockSpec((1, tm, d), lambda bb, i: (bb, i, 0)),
        out_shape=jax.ShapeDtypeStruct((b, t, d), F32),
        compiler_params=_params(("arbitrary", "arbitrary")),
        name="outproj",
    )(a, w_bf16, x_cat, gate)


def _moe_pre_kernel(x_ref, g_ref, sc_ref, sh_ref, wr_ref, h_ref, aff_ref, *, tm, n_lat):
    i = pl.program_id(1)
    h = _norm_mod(x_ref[0], g_ref[...], sc_ref[0], sh_ref[0], i * tm, n_lat)
    h_ref[0] = h.astype(BF16)
    logits = lax.dot_general(wr_ref[...], h, _NT, precision=lax.Precision.HIGHEST,
                             preferred_element_type=F32)
    mx = jnp.max(logits, axis=0, keepdims=True)
    ex = jnp.exp(logits - mx)
    aff_ref[0] = ex / jnp.sum(ex, axis=0, keepdims=True)


def _moe_pre(x, g, scale2, shift2, w_router, n_lat):
    b, r, d = x.shape
    ne = w_router.shape[1]
    tm = _pick(r, (768, 512, 384, 256, 128))
    return pl.pallas_call(
        functools.partial(_moe_pre_kernel, tm=tm, n_lat=n_lat),
        grid=(b, r // tm),
        in_specs=[pl.BlockSpec((1, tm, d), lambda bb, i: (bb, i, 0)),
                  pl.BlockSpec((1, d), lambda bb, i: (0, 0)),
                  pl.BlockSpec((1, 2, d), lambda bb, i: (bb, 0, 0)),
                  pl.BlockSpec((1, 2, d), lambda bb, i: (bb, 0, 0)),
                  pl.BlockSpec((ne, d), lambda bb, i: (0, 0))],
        out_specs=[pl.BlockSpec((1, tm, d), lambda bb, i: (bb, i, 0)),
                   pl.BlockSpec((1, ne, tm), lambda bb, i: (bb, 0, i))],
        out_shape=[jax.ShapeDtypeStruct((b, r, d), BF16),
                   jax.ShapeDtypeStruct((b, ne, r), F32)],
        compiler_params=_params(("arbitrary", "arbitrary")),
        name="moe_pre",
    )(x, g.reshape(1, d), scale2, shift2, w_router.T)


def _topk_kernel(aff_ref, tri_ref, pos_ref, gsel_ref, r0_ref, *, cap):
    ne, t = aff_ref.shape[1], aff_ref.shape[2]
    tbk = TOKEN_BLOCK
    nblk = t // tbk
    bits = pltpu.bitcast(aff_ref[0], I32)
    thr = jnp.zeros((ne, 1), I32)
    for bit in range(30, -1, -1):
        cand = thr | (1 << bit)
        cnt = jnp.sum(jnp.where(bits >= cand, 1.0, 0.0), axis=1, keepdims=True)
        thr = jnp.where(cnt >= cap, cand, thr)
    n_gt = jnp.sum(jnp.where(bits > thr, 1.0, 0.0), axis=1, keepdims=True)
    need = cap - n_gt
    tri = tri_ref[...]
    carry_eq = jnp.zeros((ne, 1), F32)
    carry_sel = jnp.zeros((ne, 1), F32)
    r0_ref[0] = jnp.full((ne, LANES), cap, I32)
    for j in range(nblk):
        sl = slice(j * tbk, (j + 1) * tbk)
        a = aff_ref[0, :, sl]
        bj = pltpu.bitcast(a, I32)
        eq = jnp.where(bj == thr, 1.0, 0.0)
        gt = jnp.where(bj > thr, 1.0, 0.0)
        pe = jnp.dot(eq.astype(BF16), tri, preferred_element_type=F32) + carry_eq
        sel = gt + eq * jnp.where(pe - eq < need, 1.0, 0.0)
        ps = jnp.dot(sel.astype(BF16), tri, preferred_element_type=F32) + carry_sel
        pos_ref[0, :, sl] = jnp.where(sel > 0, ps - 1.0, -1.0).astype(I32)
        gsel_ref[0, :, sl] = a * sel
        r0_ref[0, :, j:j + 1] = carry_sel.astype(I32)
        carry_eq = pe[:, tbk - 1:tbk]
        carry_sel = ps[:, tbk - 1:tbk]


def _topk(aff, cap, t, tok_off):
    b, ne, _ = aff.shape
    tbk = TOKEN_BLOCK
    assert t % tbk == 0 and t // tbk < LANES and tok_off % t == 0
    tri = jnp.asarray(np.triu(np.ones((tbk, tbk), np.float32)), BF16)
    return pl.pallas_call(
        functools.partial(_topk_kernel, cap=cap),
        grid=(b,),
        in_specs=[pl.BlockSpec((1, ne, t), lambda bb: (bb, 0, tok_off // t)),
                  pl.BlockSpec((tbk, tbk), lambda bb: (0, 0))],
        out_specs=[pl.BlockSpec((1, ne, t), lambda bb: (bb, 0, 0)),
                   pl.BlockSpec((1, ne, t), lambda bb: (bb, 0, 0)),
                   pl.BlockSpec((1, ne, LANES), lambda bb: (bb, 0, 0))],
        out_shape=[jax.ShapeDtypeStruct((b, ne, t), I32),
                   jax.ShapeDtypeStruct((b, ne, t), F32),
                   jax.ShapeDtypeStruct((b, ne, LANES), I32)],
        compiler_params=_params(("arbitrary",)),
        name="topk",
    )(aff, tri)


def _windows(r0_ref, row, j, w, cap, align):
    lo = r0_ref[row, j]
    hi = r0_ref[row, j + 1]
    off = pl.multiple_of(jnp.minimum((lo // align) * align, cap - w), align)
    return off, (jnp.maximum(hi - (off + w), 0) + w - 1) // w


def _gather_kernel(r0_ref, pos_ref, g_ref, h_ref, xs_ref, gs_ref, acc_ref, gacc_ref, *, w, nblk, ne, cap):
    b, e = pl.program_id(0), pl.program_id(1)
    row = b * ne + e
    tbk = TOKEN_BLOCK
    acc_ref[...] = jnp.zeros_like(acc_ref)
    gacc_ref[...] = jnp.zeros_like(gacc_ref)
    sub = lax.broadcasted_iota(I32, (w, tbk), 0)

    def window(j, off, first_rank):
        rank = off + sub
        if first_rank is not None:
            rank = jnp.where(rank >= first_rank, rank, -2)
        hit = pos_ref[0, 0, j:j + 1, :] == rank
        oh = jnp.where(hit, 1.0, 0.0).astype(BF16)
        acc_ref[pl.ds(off, w), :] += jnp.dot(oh, h_ref[0, j * tbk:(j + 1) * tbk, :], preferred_element_type=F32)
        gsum = jnp.sum(jnp.where(hit, g_ref[0, 0, j:j + 1, :], 0.0), axis=1, keepdims=True)
        gacc_ref[pl.ds(off, w), :] += jnp.broadcast_to(gsum, (w, LANES))

    for j in range(nblk):
        off, n_more = _windows(r0_ref, row, j, w, cap, 8)
        window(j, off, None)

        def more(i, _, j=j, off=off):
            start = off + i * w
            window(j, pl.multiple_of(jnp.minimum(start, cap - w), 8), start)
            return 0

        lax.fori_loop(1, n_more + 1, more, 0)
    xs_ref[0, 0] = acc_ref[...].astype(BF16)
    gs_ref[0, 0] = gacc_ref[...]


def _gather(r0, pos, gate, hb, cap, tok_off):
    b, ne, t = pos.shape
    d = hb.shape[2]
    assert tok_off % t == 0
    tbk = TOKEN_BLOCK
    nblk = t // tbk
    w = min(cap, 64)
    pos4 = pos.reshape(b, ne, nblk, tbk)
    g4 = gate.reshape(b, ne, nblk, tbk)
    grid_spec = pltpu.PrefetchScalarGridSpec(
        num_scalar_prefetch=1,
        grid=(b, ne),
        in_specs=[pl.BlockSpec((1, 1, nblk, tbk), lambda bb, e, r: (bb, e, 0, 0)),
                  pl.BlockSpec((1, 1, nblk, tbk), lambda bb, e, r: (bb, e, 0, 0)),
                  pl.BlockSpec((1, t, d), lambda bb, e, r: (bb, tok_off // t, 0))],
        out_specs=[pl.BlockSpec((1, 1, cap, d), lambda bb, e, r: (e, bb, 0, 0)),
                   pl.BlockSpec((1, 1, cap, LANES), lambda bb, e, r: (e, bb, 0, 0))],
        scratch_shapes=[pltpu.VMEM((cap, d), F32), pltpu.VMEM((cap, LANES), F32)],
    )
    return pl.pallas_call(
        functools.partial(_gather_kernel, w=w, nblk=nblk, ne=ne, cap=cap),
        grid_spec=grid_spec,
        out_shape=[jax.ShapeDtypeStruct((ne, b, cap, d), BF16),
                   jax.ShapeDtypeStruct((ne, b, cap, LANES), F32)],
        compiler_params=_params(("arbitrary", "arbitrary")),
        name="moe_gather",
    )(r0.reshape(b * ne, LANES), pos4, g4, hb)


def _ffn_kernel(*refs, nb, caps, rcs):
    ng = len(caps)
    ins, (wg_ref, wu_ref, wd_ref) = refs[:3 * ng], refs[3 * ng:3 * ng + 3]
    y_refs, acc_refs = refs[3 * ng + 3:4 * ng + 3], refs[4 * ng + 3:]
    f = pl.program_id(1)
    nf = pl.num_programs(1)

    @pl.when(f == 0)
    def _():
        for acc_ref in acc_refs:
            acc_ref[...] = jnp.zeros_like(acc_ref)

    wg = wg_ref[0, 0].astype(BF16)
    wu = wu_ref[0, 0].astype(BF16)
    wd = wd_ref[0, 0].astype(BF16)
    chunks = [(g, c0, rcs[g]) for g in range(ng) for c0 in range(0, nb * caps[g], rcs[g])]

    def gate_up(chunk):
        g, c0, rc = chunk
        x = ins[3 * g][0, c0:c0 + rc, :]
        return jnp.dot(x, wg, preferred_element_type=F32), jnp.dot(x, wu, preferred_element_type=F32)

    nxt = gate_up(chunks[0])
    for i, (g, c0, rc) in enumerate(chunks):
        a, u = nxt
        if i + 1 < len(chunks):
            nxt = gate_up(chunks[i + 1])
        hm = (a * _sigmoid(a) * u).astype(BF16)
        acc_refs[g][c0:c0 + rc, :] += jnp.dot(hm, wd, preferred_element_type=F32)

    @pl.when(f == nf - 1)
    def _():
        for g in range(ng):
            gs_ref, gt_ref = ins[3 * g + 1], ins[3 * g + 2]
            for bb in range(nb):
                sl = slice(bb * caps[g], (bb + 1) * caps[g])
                y_refs[g][0, sl, :] = (acc_refs[g][sl, :] * gs_ref[0, sl, 0:1] * gt_ref[bb]).astype(BF16)


def _ffn(groups, w_gate, w_up, w_down, layer):
    ne, b, _, d = groups[0][0].shape
    fdim = w_gate.shape[3]
    tf = _pick(fdim, (256, 128))
    caps = [xs.shape[2] for xs, _, _ in groups]
    rcs = [_pick(b * cap, (1024, 512, 256, 128, 64)) for cap in caps]
    in_specs, args = [], []
    for (xs, gs, gate_f), cap in zip(groups, caps):
        rows = b * cap
        in_specs += [pl.BlockSpec((1, rows, d), lambda e, f: (e, 0, 0)),
                     pl.BlockSpec((1, rows, LANES), lambda e, f: (e, 0, 0)),
                     pl.BlockSpec((b, 1, d), lambda e, f: (0, 0, 0))]
        args += [xs.reshape(ne, rows, d), gs.reshape(ne, rows, LANES), gate_f]
    in_specs += [pl.BlockSpec((1, 1, d, tf), lambda e, f: (layer, e, 0, f)),
                 pl.BlockSpec((1, 1, d, tf), lambda e, f: (layer, e, 0, f)),
                 pl.BlockSpec((1, 1, tf, d), lambda e, f: (layer, e, f, 0))]
    ys = pl.pallas_call(
        functools.partial(_ffn_kernel, nb=b, caps=caps, rcs=rcs),
        grid=(ne, fdim // tf),
        in_specs=in_specs,
        out_specs=[pl.BlockSpec((1, b * cap, d), lambda e, f: (e, 0, 0)) for cap in caps],
        out_shape=[jax.ShapeDtypeStruct((ne, b * cap, d), BF16) for cap in caps],
        scratch_shapes=[pltpu.VMEM((b * cap, d), F32) for cap in caps],
        compiler_params=_params(("arbitrary", "arbitrary")),
        name="moe_ffn",
    )(*args, w_gate, w_up, w_down)
    return [y.reshape(ne, b, cap, d) for y, cap in zip(ys, caps)]


def _combine_kernel(r0_ref, pos_ref, y_ref, x_ref, o_ref, acc_ref, *, w, ne, cap):
    b, j = pl.program_id(0), pl.program_id(2)
    tbk = TOKEN_BLOCK
    sub = lax.broadcasted_iota(I32, (w, tbk), 0)
    align = 16

    def window(e, off, first_rank):
        rank = off + sub
        if first_rank is not None:
            rank = jnp.where(rank >= first_rank, rank, -2)
        oh = jnp.where(pos_ref[0, 0, e:e + 1, :] == rank, 1.0, 0.0).astype(BF16)
        return lax.dot_general(oh, y_ref[e, 0, pl.ds(off, w), :], _TN, preferred_element_type=F32)

    acc = x_ref[0]
    more = []
    for e in range(ne):
        off, n_more = _windows(r0_ref, b * ne + e, j, w, cap, align)
        acc = acc + window(e, off, None)
        more.append((off, n_more))
    acc_ref[...] = acc
    for e, (off, n_more) in enumerate(more):
        def extra(i, _, e=e, off=off):
            start = off + i * w
            acc_ref[...] += window(e, pl.multiple_of(jnp.minimum(start, cap - w), align), start)
            return 0

        lax.fori_loop(1, n_more + 1, extra, 0)
    o_ref[0] = acc_ref[...]


def _combine(r0, pos, y, x, row_off):
    ne, b, cap, d = y.shape
    t = pos.shape[2]
    tbk = TOKEN_BLOCK
    nblk = t // tbk
    assert row_off % tbk == 0
    boff = row_off // tbk
    w = min(cap, 256)
    dh = d // 2
    pos4 = pos.reshape(b, ne, nblk, tbk).transpose(0, 2, 1, 3)
    grid_spec = pltpu.PrefetchScalarGridSpec(
        num_scalar_prefetch=1,
        grid=(b, 2, nblk),
        in_specs=[pl.BlockSpec((1, 1, ne, tbk), lambda bb, c, j, r: (bb, j, 0, 0)),
                  pl.BlockSpec((ne, 1, cap, dh), lambda bb, c, j, r: (0, bb, 0, c)),
                  pl.BlockSpec((1, tbk, dh), lambda bb, c, j, r: (bb, j + boff, c))],
        out_specs=pl.BlockSpec((1, tbk, dh), lambda bb, c, j, r: (bb, j, c)),
        scratch_shapes=[pltpu.VMEM((tbk, dh), F32)],
    )
    return pl.pallas_call(
        functools.partial(_combine_kernel, w=w, ne=ne, cap=cap),
        grid_spec=grid_spec,
        out_shape=jax.ShapeDtypeStruct((b, t, d), F32),
        compiler_params=_params(("arbitrary", "arbitrary", "arbitrary")),
        name="moe_combine",
    )(r0.reshape(b * ne, LANES), pos4, y, x)


def _final_kernel(x_ref, g_ref, o_ref):
    x = x_ref[0]
    ms = jnp.mean(x * x, axis=-1, keepdims=True)
    o_ref[0] = x * lax.rsqrt(ms + NORM_EPS) * g_ref[...]


def _final(x, g):
    b, t, d = x.shape
    tm = _pick(t, (512, 256, 128))
    return pl.pallas_call(
        _final_kernel,
        grid=(b, t // tm),
        in_specs=[pl.BlockSpec((1, tm, d), lambda bb, i: (bb, i, 0)),
                  pl.BlockSpec((1, d), lambda bb, i: (0, 0))],
        out_specs=pl.BlockSpec((1, tm, d), lambda bb, i: (bb, i, 0)),
        out_shape=jax.ShapeDtypeStruct((b, t, d), F32),
        compiler_params=_params(("arbitrary", "arbitrary")),
        name="final_norm",
    )(x, g.reshape(1, d))


def _moe(aff, hb, x, sets, w_gate, w_up, w_down, layer):
    ne = aff.shape[1]
    routed, groups = [], []
    for off, t, gate_f in sets:
        cap = 2 * t // ne
        pos, gsel, r0 = _topk(aff, cap, t, off)
        xs, gs = _gather(r0, pos, gsel, hb, cap, off)
        routed.append((r0, pos, off))
        groups.append((xs, gs, gate_f))
    ys = _ffn(groups, w_gate, w_up, w_down, layer)
    return [_combine(r0, pos, y, x, off) for (r0, pos, off), y in zip(routed, ys)]


def _rope_tables(t, n_ctx):
    rows = t // GRID_W
    row = jnp.repeat(jnp.arange(rows, dtype=F32), GRID_W)
    col = jnp.tile(jnp.arange(GRID_W, dtype=F32), rows)
    pairs = HEAD_W // 8
    freq = ROPE_BASE ** (-jnp.arange(pairs, dtype=F32) / pairs)
    ang = jnp.concatenate([row[:, None] * freq, col[:, None] * freq], axis=-1)
    cos, sin = jnp.cos(ang), jnp.sin(ang)
    cos = jnp.tile(cos, (1, 4))
    sin = jnp.tile(jnp.concatenate([-sin, sin], axis=-1), (1, 2))
    cos = jnp.concatenate([cos, jnp.ones((n_ctx, HEAD_W), F32)], axis=0)
    sin = jnp.concatenate([sin, jnp.zeros((n_ctx, HEAD_W), F32)], axis=0)
    qs = (HEAD_W // 2) ** -0.5 * math.log2(math.e)
    return jnp.stack([cos * qs, cos]), jnp.stack([sin * qs, sin])


def _diff_w_ext(w):
    d = w.shape[0]
    wv = w[:, 2 * d:].reshape(d, N_HEADS, HEAD_W)
    wv = jnp.concatenate([wv, jnp.zeros_like(wv)], axis=-1).reshape(d, 2 * d)
    return jnp.concatenate([w[:, :2 * d], wv], axis=1).astype(BF16)


def kernel(x, c, ctx, c_ctx, ada_w, ada_b, norm_mix, norm_ffn, norm_final, hgrn_w_in, hgrn_lb_logits, hgrn_norm, hgrn_w_out, diff_w_in, diff_lambda, diff_subln, diff_w_out, moe_router, moe_w_gate, moe_w_up, moe_w_down):
    b, t, d = x.shape
    n_ctx = ctx.shape[1]
    depth = ada_w.shape[0]
    ne = moe_router.shape[2]
    assert depth == 2 and d == N_HEADS * HEAD_W

    cvec = jnp.concatenate([c, c_ctx[None, :], jnp.zeros((8 - b - 1, d), F32)], axis=0)
    mod = _ada(cvec, ada_w, ada_b)
    lower_bounds = jnp.cumsum(jax.nn.softmax(hgrn_lb_logits.astype(F32), axis=0), axis=0)

    def kinds(layer, k, plus_one=False):
        m = mod[layer, :, k * d:(k + 1) * d]
        v = jnp.stack([jnp.broadcast_to(m[b], (b, d)), m[:b]], axis=1)
        return 1.0 + v if plus_one else v

    x_cat = jnp.concatenate([x, ctx], axis=1)

    proj = _inproj(x_cat, norm_mix[0], kinds(0, 1, True), kinds(0, 0), hgrn_w_in[0].astype(BF16), t, F32)
    o_f = _gla(proj, lower_bounds[0], t, False)
    o_b = _gla(proj, lower_bounds[0], t, True)
    x_cat = _hgrn_out(o_f, o_b, proj, hgrn_norm[0], hgrn_w_out[0].astype(BF16), x_cat, kinds(0, 2), t)
    hb, aff = _moe_pre(x_cat, norm_ffn[0], kinds(0, 4, True), kinds(0, 3), moe_router[0], t)
    gate_f = kinds(0, 5)
    x_lat, x_ctx = _moe(aff, hb, x_cat, [(0, t, gate_f[:, 1:2]), (t, n_ctx, gate_f[:, 0:1])],
                        moe_w_gate, moe_w_up, moe_w_down, 0)
    x_cat = jnp.concatenate([x_lat, x_ctx], axis=1)

    lam_init = 0.8 - 0.6 * math.exp(-0.3 * 1)
    qkv = _inproj(x_cat, norm_mix[1], kinds(1, 1, True), kinds(1, 0), _diff_w_ext(diff_w_in[0]), t, BF16,
                  rope_tabs=_rope_tables(t, n_ctx))
    att = _diff_attn(qkv, diff_lambda[0], diff_subln[0], t, lam_init)
    x_lat = _outproj(att, diff_w_out[0].astype(BF16), x_cat, kinds(1, 2)[:, 1:2])
    hb, aff = _moe_pre(x_lat, norm_ffn[1], kinds(1, 4, True), kinds(1, 3), moe_router[1], t)
    (x_lat,) = _moe(aff, hb, x_lat, [(0, t, kinds(1, 5)[:, 1:2])], moe_w_gate, moe_w_up, moe_w_down, 1)
    return _final(x_lat, norm_final)
```

```python
import functools
import math

import jax
import jax.numpy as jnp
import numpy as np
from jax import lax
from jax.experimental import pallas as pl
from jax.experimental.pallas import tpu as pltpu

F32 = jnp.float32
BF16 = jnp.bfloat16
I32 = jnp.int32

NORM_EPS = 1e-6
LANES = 128
HEAD_W = 128
N_HEADS = 8
GLA_CHUNK = 64
GLA_LEVELS = (32, 16, 8, 4, 2, 1)
ROPE_BASE = 10000.0
GRID_W = 64
TOKEN_BLOCK = 256
VMEM_LIMIT = 56 * 1024 * 1024

_NT = (((1,), (1,)), ((), ()))
_TN = (((0,), (0,)), ((), ()))


def _pick(n, cands):
    for c in cands:
        if n % c == 0:
            return c
    raise ValueError(f"no tile for {n} in {cands}")


def _params(sem):
    return pltpu.CompilerParams(dimension_semantics=sem, vmem_limit_bytes=VMEM_LIMIT)


def _sigmoid(x):
    return 1.0 / (1.0 + jnp.exp(-x))


def _norm_mod(x, g, scale2, shift2, row0, n_lat):
    ms = jnp.mean(x * x, axis=-1, keepdims=True)
    y = x * lax.rsqrt(ms + NORM_EPS) * g
    rows = row0 + lax.broadcasted_iota(I32, (x.shape[0], 1), 0)
    is_ctx = rows >= n_lat
    sc = jnp.where(is_ctx, scale2[0:1], scale2[1:2])
    sh = jnp.where(is_ctx, shift2[0:1], shift2[1:2])
    return y * sc + sh


def _ada_kernel(c_ref, w_ref, b_ref, o_ref):
    c = c_ref[...]
    s = c * _sigmoid(c)
    o_ref[0] = jnp.dot(s, w_ref[0], precision=lax.Precision.HIGHEST,
                       preferred_element_type=F32) + b_ref[0]


def _ada(cvec, ada_w, ada_b):
    depth, d, n = ada_w.shape
    rows = cvec.shape[0]
    tn = _pick(n, (1024, 512, 256, 128))
    return pl.pallas_call(
        _ada_kernel,
        grid=(depth, n // tn),
        in_specs=[pl.BlockSpec((rows, d), lambda l, j: (0, 0)),
                  pl.BlockSpec((1, d, tn), lambda l, j: (l, 0, j)),
                  pl.BlockSpec((1, 1, tn), lambda l, j: (l, 0, j))],
        out_specs=pl.BlockSpec((1, rows, tn), lambda l, j: (l, 0, j)),
        out_shape=jax.ShapeDtypeStruct((depth, rows, n), F32),
        compiler_params=_params(("arbitrary", "arbitrary")),
        name="adaln",
    )(cvec, ada_w, ada_b.reshape(depth, 1, n))


def _inproj_kernel(x_ref, g_ref, sc_ref, sh_ref, w_ref, *rest, tm, tn, n_lat, rope):
    if rope:
        cos_ref, sin_ref, o_ref = rest
    else:
        (o_ref,) = rest
    i = pl.program_id(1)
    h = _norm_mod(x_ref[0], g_ref[...], sc_ref[0], sh_ref[0], i * tm, n_lat).astype(BF16)
    for n in range(w_ref.shape[1] // tn):
        cols = slice(n * tn, (n + 1) * tn)
        acc = jnp.dot(h, w_ref[:, cols], preferred_element_type=F32)
        if not rope:
            o_ref[0, :, cols] = acc.astype(o_ref.dtype)
        elif n < 2:
            cos = cos_ref[n]
            sin = sin_ref[n]
            lane = lax.broadcasted_iota(I32, (1, HEAD_W), 1)
            first = (lane % 64) < 32
            for hd in range(tn // HEAD_W):
                a = acc[:, hd * HEAD_W:(hd + 1) * HEAD_W]
                rot = jnp.where(first, pltpu.roll(a, HEAD_W - 32, 1), pltpu.roll(a, 32, 1))
                o_ref[0, :, n * tn + hd * HEAD_W:n * tn + (hd + 1) * HEAD_W] = (
                    a * cos + rot * sin).astype(o_ref.dtype)
        else:
            col = lax.broadcasted_iota(I32, (1, tn), 1)
            o_ref[0, :, cols] = jnp.where(col % (2 * HEAD_W) >= HEAD_W, 1.0, acc).astype(o_ref.dtype)


def _inproj(x, g, scale2, shift2, w_bf16, n_lat, out_dtype, rope_tabs=None):
    b, r, d = x.shape
    n = w_bf16.shape[1]
    tm = _pick(r, (384, 256, 128))
    tn = 1024
    rope = rope_tabs is not None
    in_specs = [pl.BlockSpec((1, tm, d), lambda bb, i: (bb, i, 0)),
                pl.BlockSpec((1, d), lambda bb, i: (0, 0)),
                pl.BlockSpec((1, 2, d), lambda bb, i: (bb, 0, 0)),
                pl.BlockSpec((1, 2, d), lambda bb, i: (bb, 0, 0)),
                pl.BlockSpec((d, n), lambda bb, i: (0, 0))]
    args = [x, g.reshape(1, d), scale2, shift2, w_bf16]
    if rope:
        cos_t, sin_t = rope_tabs
        spec = pl.BlockSpec((2, tm, HEAD_W), lambda bb, i: (0, i, 0))
        in_specs += [spec, spec]
        args += [cos_t, sin_t]
    return pl.pallas_call(
        functools.partial(_inproj_kernel, tm=tm, tn=tn, n_lat=n_lat, rope=rope),
        grid=(b, r // tm),
        in_specs=in_specs,
        out_specs=pl.BlockSpec((1, tm, n), lambda bb, i: (bb, i, 0)),
        out_shape=jax.ShapeDtypeStruct((b, r, n), out_dtype),
        compiler_params=_params(("arbitrary", "arbitrary")),
        name="inproj_rope" if rope else "inproj",
    )(*args)


def _gla_consts(reverse):
    c = GLA_CHUNK
    nl = len(GLA_LEVELS)
    t = np.arange(c)
    p = c - 1 - t if reverse else t
    tri = (p[None, :] <= p[:, None]).astype(np.float32)
    rowsel = np.zeros((nl, c, HEAD_W), np.float32)
    masks = np.zeros((nl + 1, c, c), np.float32)
    for li, s in enumerate(GLA_LEVELS):
        blk = p // (2 * s)
        second = (p // s) % 2 == 1
        rowsel[li] = second[:, None]
        masks[li] = (blk[:, None] == blk[None, :]) & second[:, None] & (~second)[None, :]
    masks[nl] = np.eye(c)
    return tri, rowsel, masks


def _level_ref(b, s, reverse):
    c = b.shape[0]
    at = s if reverse else s - 1
    if 2 * s >= 8:
        pieces = [jnp.broadcast_to(b[blk * 2 * s + at:blk * 2 * s + at + 1], (2 * s, HEAD_W))
                  for blk in range(c // (2 * s))]
        return pieces[0] if len(pieces) == 1 else jnp.concatenate(pieces, axis=0)
    sub = lax.broadcasted_iota(I32, (8, HEAD_W), 0)
    outs = []
    for v in range(c // 8):
        acc = None
        for blk in range(8 // (2 * s)):
            row = 8 * v + blk * 2 * s + at
            cand = jnp.broadcast_to(b[row:row + 1], (8, HEAD_W))
            acc = cand if acc is None else jnp.where(sub >= blk * 2 * s, cand, acc)
        outs.append(acc)
    return jnp.concatenate(outs, axis=0)


def _gla_kernel(q_ref, f_ref, v_ref, lb_ref, tri_ref, rowsel_ref, masks_ref, o_ref, st_ref, *, tb, hg, reverse):
    c = GLA_CHUNK
    nl = len(GLA_LEVELS)
    nc = tb // c

    @pl.when(pl.program_id(2) == 0)
    def _():
        st_ref[...] = jnp.zeros_like(st_ref)

    tri = tri_ref[...]
    last = 0 if reverse else c - 1
    starts = [(nc - 1 - ci if reverse else ci) * c for ci in range(nc)]
    units = [(hd, r0) for hd in range(hg) for r0 in starts]

    qf, kk, vb, b3 = [], [], [], []
    for hd, r0 in units:
        lanes = slice(hd * HEAD_W, (hd + 1) * HEAD_W)
        lb = lb_ref[0, :, lanes]
        q = q_ref[0, r0:r0 + c, lanes]
        qf.append(q * _sigmoid(q))
        f = lb + (1.0 - lb) * _sigmoid(f_ref[0, r0:r0 + c, lanes])
        lf = jnp.log(f)
        kk.append(1.0 - f)
        vb.append(v_ref[0, r0:r0 + c, lanes].astype(BF16))
        hi = lf.astype(BF16)
        r1 = lf - hi.astype(F32)
        mid = r1.astype(BF16)
        lo = (r1 - mid.astype(F32)).astype(BF16)
        b3.append(jnp.dot(tri, jnp.concatenate([hi, mid, lo], axis=1), preferred_element_type=F32))
    bc = [x[:, :HEAD_W] + x[:, HEAD_W:2 * HEAD_W] + x[:, 2 * HEAD_W:] for x in b3]

    a = []
    for i in range(len(units)):
        ai = masks_ref[nl] * lax.dot_general(qf[i].astype(BF16), kk[i].astype(BF16), _NT,
                                             preferred_element_type=F32)
        for li, s in enumerate(GLA_LEVELS):
            e = jnp.exp(-jnp.abs(bc[i] - _level_ref(bc[i], s, reverse)))
            xb = (jnp.where(rowsel_ref[li] > 0, qf[i], kk[i]) * e).astype(BF16)
            ai = ai + masks_ref[li] * lax.dot_general(xb, xb, _NT, preferred_element_type=F32)
        a.append(ai.astype(BF16))

    o_intra = [jnp.dot(a[i], vb[i], preferred_element_type=F32) for i in range(len(units))]
    upd = []
    for i in range(len(units)):
        kdec = (kk[i] * jnp.exp(bc[i][last:last + 1] - bc[i])).astype(BF16)
        upd.append(lax.dot_general(vb[i], kdec, _TN, preferred_element_type=F32))

    st = [st_ref[hd] for hd in range(hg)]
    for i, (hd, r0) in enumerate(units):
        qd = (qf[i] * jnp.exp(bc[i])).astype(BF16)
        o_ref[0, r0:r0 + c, hd * HEAD_W:(hd + 1) * HEAD_W] = o_intra[i] + lax.dot_general(
            qd, st[hd].astype(BF16), _NT, preferred_element_type=F32)
        st[hd] = st[hd] * jnp.exp(bc[i][last:last + 1]) + upd[i]
    for hd in range(hg):
        st_ref[hd] = st[hd]


def _gla(proj, lb, n_lat, reverse):
    b, r, _ = proj.shape
    tb = TOKEN_BLOCK
    assert n_lat % tb == 0 and r % tb == 0
    nblk = r // tb
    nlb = n_lat // tb
    ncb = nblk - nlb
    tri, rowsel, masks = _gla_consts(reverse)
    nl = len(GLA_LEVELS)
    c = GLA_CHUNK
    hg = 8
    wblk = hg * HEAD_W
    fcol = N_HEADS // hg * (2 if reverse else 1)

    def blk(n):
        if reverse:
            return jnp.where(n < ncb, nblk - 1 - n, nlb - 1 - (n - ncb))
        return jnp.where(n < ncb, nlb + n, n - ncb)

    return pl.pallas_call(
        functools.partial(_gla_kernel, tb=tb, hg=hg, reverse=reverse),
        grid=(b, N_HEADS // hg, nblk),
        in_specs=[pl.BlockSpec((1, tb, wblk), lambda bb, h, n: (bb, blk(n), h)),
                  pl.BlockSpec((1, tb, wblk), lambda bb, h, n: (bb, blk(n), fcol + h)),
                  pl.BlockSpec((1, tb, wblk), lambda bb, h, n: (bb, blk(n), 3 * N_HEADS // hg + h)),
                  pl.BlockSpec((1, 1, wblk), lambda bb, h, n: (h, 0, 0)),
                  pl.BlockSpec((c, c), lambda bb, h, n: (0, 0)),
                  pl.BlockSpec((nl, c, HEAD_W), lambda bb, h, n: (0, 0, 0)),
                  pl.BlockSpec((nl + 1, c, c), lambda bb, h, n: (0, 0, 0))],
        out_specs=pl.BlockSpec((1, tb, wblk), lambda bb, h, n: (bb, blk(n), h)),
        out_shape=jax.ShapeDtypeStruct((b, r, N_HEADS * HEAD_W), F32),
        scratch_shapes=[pltpu.VMEM((hg, HEAD_W, HEAD_W), F32)],
        compiler_params=_params(("arbitrary", "arbitrary", "arbitrary")),
        name="gla_bwd" if reverse else "gla_fwd",
    )(proj, proj, proj, lb.reshape(N_HEADS // hg, 1, wblk), jnp.asarray(tri, BF16),
      jnp.asarray(rowsel), jnp.asarray(masks))


def _hgrn_out_kernel(of_ref, ob_ref, g_ref, ng_ref, w_ref, x_ref, gate_ref, o_ref, y_ref, *, tm, n_lat):
    i = pl.program_id(1)
    o = of_ref[0] + ob_ref[0]
    for h in range(N_HEADS):
        sl = slice(h * HEAD_W, (h + 1) * HEAD_W)
        oh = o[:, sl]
        ms = jnp.mean(oh * oh, axis=-1, keepdims=True)
        g = g_ref[0, :, sl]
        y_ref[:, sl] = (oh * lax.rsqrt(ms + NORM_EPS) * ng_ref[:, sl] * (g * _sigmoid(g))).astype(BF16)
    y = jnp.dot(y_ref[...], w_ref[...], preferred_element_type=F32)
    rows = i * tm + lax.broadcasted_iota(I32, (tm, 1), 0)
    gate = jnp.where(rows >= n_lat, gate_ref[0, 0:1], gate_ref[0, 1:2])
    o_ref[0] = x_ref[0] + gate * y


def _hgrn_out(o_f, o_b, proj, norm_g, w_bf16, x, gate2, n_lat):
    b, r, d = x.shape
    tm = _pick(r, (384, 256, 128))
    return pl.pallas_call(
        functools.partial(_hgrn_out_kernel, tm=tm, n_lat=n_lat),
        grid=(b, r // tm),
        in_specs=[pl.BlockSpec((1, tm, d), lambda bb, i: (bb, i, 0)),
                  pl.BlockSpec((1, tm, d), lambda bb, i: (bb, i, 0)),
                  pl.BlockSpec((1, tm, d), lambda bb, i: (bb, i, 4)),
                  pl.BlockSpec((1, d), lambda bb, i: (0, 0)),
                  pl.BlockSpec((d, d), lambda bb, i: (0, 0)),
                  pl.BlockSpec((1, tm, d), lambda bb, i: (bb, i, 0)),
                  pl.BlockSpec((1, 2, d), lambda bb, i: (bb, 0, 0))],
        out_specs=pl.BlockSpec((1, tm, d), lambda bb, i: (bb, i, 0)),
        out_shape=jax.ShapeDtypeStruct((b, r, d), F32),
        scratch_shapes=[pltpu.VMEM((tm, d), BF16)],
        compiler_params=_params(("arbitrary", "arbitrary")),
        name="hgrn_out",
    )(o_f, o_b, proj, jnp.tile(norm_g, N_HEADS).reshape(1, d), w_bf16, x, gate2)


def _attn_kernel(q_ref, k_ref, v_ref, lam_ref, g_ref, o_ref, *, tq, tk, lam_init):
    nk = k_ref.shape[1] // tk
    q = q_ref[0]
    lane = lax.broadcasted_iota(I32, (1, HEAD_W), 1)
    zero = jnp.zeros_like(q)
    qc = [jnp.where(lane < 64, q, zero), jnp.where(lane >= 64, q, zero)]

    def scores(u):
        comp, ki = u
        return lax.dot_general(qc[comp], k_ref[0, ki * tk:(ki + 1) * tk, :], _NT, preferred_element_type=F32)

    units = [(comp, ki) for ki in range(nk) for comp in range(2)]
    m = [jnp.full((tq, LANES), -jnp.inf, F32)] * 2
    acc = [jnp.zeros((tq, 2 * HEAD_W), F32)] * 2
    s_next = scores(units[0])
    for idx, (comp, ki) in enumerate(units):
        s = s_next
        if idx + 1 < len(units):
            s_next = scores(units[idx + 1])
        m_new = jnp.maximum(m[comp], jnp.max(s, axis=-1, keepdims=True))
        p = jnp.exp2(s - jnp.tile(m_new, (1, tk // LANES)))
        alpha = jnp.exp2(m[comp] - m_new)
        v1 = v_ref[0, ki * tk:(ki + 1) * tk, :]
        acc[comp] = jnp.tile(alpha, (1, 2)) * acc[comp] + jnp.dot(p.astype(BF16), v1, preferred_element_type=F32)
        m[comp] = m_new
    out = [a[:, :HEAD_W] / a[:, HEAD_W:] for a in acc]

    lv = lam_ref[...]
    s01 = jnp.sum(lv[0:1] * lv[1:2], axis=-1, keepdims=True)
    s23 = jnp.sum(lv[2:3] * lv[3:4], axis=-1, keepdims=True)
    lam = jnp.exp(s01) - jnp.exp(s23) + lam_init
    o = out[0] - lam * out[1]
    ms = jnp.mean(o * o, axis=-1, keepdims=True)
    o_ref[0] = (o * lax.rsqrt(ms + NORM_EPS) * g_ref[...] * (1.0 - lam_init)).astype(o_ref.dtype)


def _diff_attn(qkv, lam_vecs, subln_g, t, lam_init):
    b, r, _ = qkv.shape
    tq = _pick(t, (512, 256, 128))
    tk = _pick(r, (768, 1280, 512, 256))
    return pl.pallas_call(
        functools.partial(_attn_kernel, tq=tq, tk=tk, lam_init=lam_init),
        grid=(b, N_HEADS, t // tq),
        in_specs=[pl.BlockSpec((1, tq, HEAD_W), lambda bb, h, i: (bb, i, h)),
                  pl.BlockSpec((1, r, HEAD_W), lambda bb, h, i: (bb, 0, N_HEADS + h)),
                  pl.BlockSpec((1, r, 2 * HEAD_W), lambda bb, h, i: (bb, 0, N_HEADS + h)),
                  pl.BlockSpec(lam_vecs.shape, lambda bb, h, i: (0, 0)),
                  pl.BlockSpec((1, HEAD_W), lambda bb, h, i: (0, 0))],
        out_specs=pl.BlockSpec((1, tq, HEAD_W), lambda bb, h, i: (bb, i, h)),
        out_shape=jax.ShapeDtypeStruct((b, t, N_HEADS * HEAD_W), BF16),
        compiler_params=_params(("arbitrary", "arbitrary", "arbitrary")),
        name="diff_attn",
    )(qkv, qkv, qkv, lam_vecs.astype(F32), subln_g.reshape(1, HEAD_W).astype(F32))


def _outproj_kernel(a_ref, w_ref, x_ref, gate_ref, o_ref):
    y = jnp.dot(a_ref[0], w_ref[...], preferred_element_type=F32)
    o_ref[0] = x_ref[0] + gate_ref[0] * y


def _outproj(a, w_bf16, x_cat, gate):
    b, t, d = a.shape
    tm = _pick(t, (512, 256, 128))
    return pl.pallas_call(
        _outproj_kernel,
        grid=(b, t // tm),
        in_specs=[pl.BlockSpec((1, tm, d), lambda bb, i: (bb, i, 0)),
                  pl.BlockSpec((d, d), lambda bb, i: (0, 0)),
                  pl.BlockSpec((1, tm, d), lambda bb, i: (bb, i, 0)),
                  pl.BlockSpec((1, 1, d), lambda bb, i: (bb, 0, 0))],
        out_specs=pl.BlockSpec((1, tm, d), lambda bb, i: (bb, i, 0)),
        out_shape=jax.ShapeDtypeStruct((b, t, d), F32),
        compiler_params=_params(("arbitrary", "arbitrary")),
        name="outproj",
    )(a, w_bf16, x_cat, gate)


def _moe_pre_kernel(x_ref, g_ref, sc_ref, sh_ref, wr_ref, h_ref, aff_ref, *, tm, n_lat):
    i = pl.program_id(1)
    h = _norm_mod(x_ref[0], g_ref[...], sc_ref[0], sh_ref[0], i * tm, n_lat)
    h_ref[0] = h.astype(BF16)
    logits = lax.dot_general(wr_ref[...], h, _NT, precision=lax.Precision.HIGHEST,
                             preferred_element_type=F32)
    mx = jnp.max(logits, axis=0, keepdims=True)
    ex = jnp.exp(logits - mx)
    aff_ref[0] = ex / jnp.sum(ex, axis=0, keepdims=True)


def _moe_pre(x, g, scale2, shift2, w_router, n_lat):
    b, r, d = x.shape
    ne = w_router.shape[1]
    tm = _pick(r, (768, 512, 384, 256, 128))
    return pl.pallas_call(
        functools.partial(_moe_pre_kernel, tm=tm, n_lat=n_lat),
        grid=(b, r // tm),
        in_specs=[pl.BlockSpec((1, tm, d), lambda bb, i: (bb, i, 0)),
                  pl.BlockSpec((1, d), lambda bb, i: (0, 0)),
                  pl.BlockSpec((1, 2, d), lambda bb, i: (bb, 0, 0)),
                  pl.BlockSpec((1, 2, d), lambda bb, i: (bb, 0, 0)),
                  pl.BlockSpec((ne, d), lambda bb, i: (0, 0))],
        out_specs=[pl.BlockSpec((1, tm, d), lambda bb, i: (bb, i, 0)),
                   pl.BlockSpec((1, ne, tm), lambda bb, i: (bb, 0, i))],
        out_shape=[jax.ShapeDtypeStruct((b, r, d), BF16),
                   jax.ShapeDtypeStruct((b, ne, r), F32)],
        compiler_params=_params(("arbitrary", "arbitrary")),
        name="moe_pre",
    )(x, g.reshape(1, d), scale2, shift2, w_router.T)


def _topk_kernel(aff_ref, tri_ref, pos_ref, gsel_ref, r0_ref, *, cap):
    ne, t = aff_ref.shape[1], aff_ref.shape[2]
    tbk = TOKEN_BLOCK
    nblk = t // tbk
    bits = pltpu.bitcast(aff_ref[0], I32)
    thr = jnp.zeros((ne, 1), I32)
    for bit in range(30, -1, -1):
        cand = thr | (1 << bit)
        cnt = jnp.sum(jnp.where(bits >= cand, 1.0, 0.0), axis=1, keepdims=True)
        thr = jnp.where(cnt >= cap, cand, thr)
    n_gt = jnp.sum(jnp.where(bits > thr, 1.0, 0.0), axis=1, keepdims=True)
    need = cap - n_gt
    tri = tri_ref[...]
    carry_eq = jnp.zeros((ne, 1), F32)
    carry_sel = jnp.zeros((ne, 1), F32)
    r0_ref[0] = jnp.full((ne, LANES), cap, I32)
    for j in range(nblk):
        sl = slice(j * tbk, (j + 1) * tbk)
        a = aff_ref[0, :, sl]
        bj = pltpu.bitcast(a, I32)
        eq = jnp.where(bj == thr, 1.0, 0.0)
        gt = jnp.where(bj > thr, 1.0, 0.0)
        pe = jnp.dot(eq.astype(BF16), tri, preferred_element_type=F32) + carry_eq
        sel = gt + eq * jnp.where(pe - eq < need, 1.0, 0.0)
        ps = jnp.dot(sel.astype(BF16), tri, preferred_element_type=F32) + carry_sel
        pos_ref[0, :, sl] = jnp.where(sel > 0, ps - 1.0, -1.0).astype(I32)
        gsel_ref[0, :, sl] = a * sel
        r0_ref[0, :, j:j + 1] = carry_sel.astype(I32)
        carry_eq = pe[:, tbk - 1:tbk]
        carry_sel = ps[:, tbk - 1:tbk]


def _topk(aff, cap, t, tok_off):
    b, ne, _ = aff.shape
    tbk = TOKEN_BLOCK
    assert t % tbk == 0 and t // tbk < LANES and tok_off % t == 0
    tri = jnp.asarray(np.triu(np.ones((tbk, tbk), np.float32)), BF16)
    return pl.pallas_call(
        functools.partial(_topk_kernel, cap=cap),
        grid=(b,),
        in_specs=[pl.BlockSpec((1, ne, t), lambda bb: (bb, 0, tok_off // t)),
                  pl.BlockSpec((tbk, tbk), lambda bb: (0, 0))],
        out_specs=[pl.BlockSpec((1, ne, t), lambda bb: (bb, 0, 0)),
                   pl.BlockSpec((1, ne, t), lambda bb: (bb, 0, 0)),
                   pl.BlockSpec((1, ne, LANES), lambda bb: (bb, 0, 0))],
        out_shape=[jax.ShapeDtypeStruct((b, ne, t), I32),
                   jax.ShapeDtypeStruct((b, ne, t), F32),
                   jax.ShapeDtypeStruct((b, ne, LANES), I32)],
        compiler_params=_params(("arbitrary",)),
        name="topk",
    )(aff, tri)


def _windows(r0_ref, row, j, w, cap, align):
    lo = r0_ref[row, j]
    hi = r0_ref[row, j + 1]
    off = pl.multiple_of(jnp.minimum((lo // align) * align, cap - w), align)
    return off, (jnp.maximum(hi - (off + w), 0) + w - 1) // w


def _gather_kernel(r0_ref, pos_ref, g_ref, h_ref, xs_ref, gs_ref, acc_ref, gacc_ref, *, w, nblk, ne, cap):
    b, e = pl.program_id(0), pl.program_id(1)
    row = b * ne + e
    tbk = TOKEN_BLOCK
    acc_ref[...] = jnp.zeros_like(acc_ref)
    gacc_ref[...] = jnp.zeros_like(gacc_ref)
    sub = lax.broadcasted_iota(I32, (w, tbk), 0)

    def window(j, off, first_rank):
        rank = off + sub
        if first_rank is not None:
            rank = jnp.where(rank >= first_rank, rank, -2)
        hit = pos_ref[0, 0, j:j + 1, :] == rank
        oh = jnp.where(hit, 1.0, 0.0).astype(BF16)
        acc_ref[pl.ds(off, w), :] += jnp.dot(oh, h_ref[0, j * tbk:(j + 1) * tbk, :], preferred_element_type=F32)
        gsum = jnp.sum(jnp.where(hit, g_ref[0, 0, j:j + 1, :], 0.0), axis=1, keepdims=True)
        gacc_ref[pl.ds(off, w), :] += jnp.broadcast_to(gsum, (w, LANES))

    for j in range(nblk):
        off, n_more = _windows(r0_ref, row, j, w, cap, 8)
        window(j, off, None)

        def more(i, _, j=j, off=off):
            start = off + i * w
            window(j, pl.multiple_of(jnp.minimum(start, cap - w), 8), start)
            return 0

        lax.fori_loop(1, n_more + 1, more, 0)
    xs_ref[0, 0] = acc_ref[...].astype(BF16)
    gs_ref[0, 0] = gacc_ref[...]


def _gather(r0, pos, gate, hb, cap, tok_off):
    b, ne, t = pos.shape
    d = hb.shape[2]
    assert tok_off % t == 0
    tbk = TOKEN_BLOCK
    nblk = t // tbk
    w = min(cap, 64)
    pos4 = pos.reshape(b, ne, nblk, tbk)
    g4 = gate.reshape(b, ne, nblk, tbk)
    grid_spec = pltpu.PrefetchScalarGridSpec(
        num_scalar_prefetch=1,
        grid=(b, ne),
        in_specs=[pl.BlockSpec((1, 1, nblk, tbk), lambda bb, e, r: (bb, e, 0, 0)),
                  pl.BlockSpec((1, 1, nblk, tbk), lambda bb, e, r: (bb, e, 0, 0)),
                  pl.BlockSpec((1, t, d), lambda bb, e, r: (bb, tok_off // t, 0))],
        out_specs=[pl.BlockSpec((1, 1, cap, d), lambda bb, e, r: (e, bb, 0, 0)),
                   pl.BlockSpec((1, 1, cap, LANES), lambda bb, e, r: (e, bb, 0, 0))],
        scratch_shapes=[pltpu.VMEM((cap, d), F32), pltpu.VMEM((cap, LANES), F32)],
    )
    return pl.pallas_call(
        functools.partial(_gather_kernel, w=w, nblk=nblk, ne=ne, cap=cap),
        grid_spec=grid_spec,
        out_shape=[jax.ShapeDtypeStruct((ne, b, cap, d), BF16),
                   jax.ShapeDtypeStruct((ne, b, cap, LANES), F32)],
        compiler_params=_params(("arbitrary", "arbitrary")),
        name="moe_gather",
    )(r0.reshape(b * ne, LANES), pos4, g4, hb)


def _ffn_kernel(*refs, nb, caps, rcs):
    ng = len(caps)
    ins, (wg_ref, wu_ref, wd_ref) = refs[:3 * ng], refs[3 * ng:3 * ng + 3]
    y_refs, acc_refs = refs[3 * ng + 3:4 * ng + 3], refs[4 * ng + 3:]
    f = pl.program_id(1)
    nf = pl.num_programs(1)

    @pl.when(f == 0)
    def _():
        for acc_ref in acc_refs:
            acc_ref[...] = jnp.zeros_like(acc_ref)

    wg = wg_ref[0, 0].astype(BF16)
    wu = wu_ref[0, 0].astype(BF16)
    wd = wd_ref[0, 0].astype(BF16)
    chunks = [(g, c0, rcs[g]) for g in range(ng) for c0 in range(0, nb * caps[g], rcs[g])]

    def gate_up(chunk):
        g, c0, rc = chunk
        x = ins[3 * g][0, c0:c0 + rc, :]
        return jnp.dot(x, wg, preferred_element_type=F32), jnp.dot(x, wu, preferred_element_type=F32)

    nxt = gate_up(chunks[0])
    for i, (g, c0, rc) in enumerate(chunks):
        a, u = nxt
        if i + 1 < len(chunks):
            nxt = gate_up(chunks[i + 1])
        hm = (a * _sigmoid(a) * u).astype(BF16)
        acc_refs[g][c0:c0 + rc, :] += jnp.dot(hm, wd, preferred_element_type=F32)

    @pl.when(f == nf - 1)
    def _():
        for g in range(ng):
            gs_ref, gt_ref = ins[3 * g + 1], ins[3 * g + 2]
            for bb in range(nb):
                sl = slice(bb * caps[g], (bb + 1) * caps[g])
                y_refs[g][0, sl, :] = (acc_refs[g][sl, :] * gs_ref[0, sl, 0:1] * gt_ref[bb]).astype(BF16)


def _ffn(groups, w_gate, w_up, w_down, layer):
    ne, b, _, d = groups[0][0].shape
    fdim = w_gate.shape[3]
    tf = _pick(fdim, (256, 128))
    caps = [xs.shape[2] for xs, _, _ in groups]
    rcs = [_pick(b * cap, (1024, 512, 256, 128, 64)) for cap in caps]
    in_specs, args = [], []
    for (xs, gs, gate_f), cap in zip(groups, caps):
        rows = b * cap
        in_specs += [pl.BlockSpec((1, rows, d), lambda e, f: (e, 0, 0)),
                     pl.BlockSpec((1, rows, LANES), lambda e, f: (e, 0, 0)),
                     pl.BlockSpec((b, 1, d), lambda e, f: (0, 0, 0))]
        args += [xs.reshape(ne, rows, d), gs.reshape(ne, rows, LANES), gate_f]
    in_specs += [pl.BlockSpec((1, 1, d, tf), lambda e, f: (layer, e, 0, f)),
                 pl.BlockSpec((1, 1, d, tf), lambda e, f: (layer, e, 0, f)),
                 pl.BlockSpec((1, 1, tf, d), lambda e, f: (layer, e, f, 0))]
    ys = pl.pallas_call(
        functools.partial(_ffn_kernel, nb=b, caps=caps, rcs=rcs),
        grid=(ne, fdim // tf),
        in_specs=in_specs,
        out_specs=[pl.BlockSpec((1, b * cap, d), lambda e, f: (e, 0, 0)) for cap in caps],
        out_shape=[jax.ShapeDtypeStruct((ne, b * cap, d), BF16) for cap in caps],
        scratch_shapes=[pltpu.VMEM((b * cap, d), F32) for cap in caps],
        compiler_params=_params(("arbitrary", "arbitrary")),
        name="moe_ffn",
    )(*args, w_gate, w_up, w_down)
    return [y.reshape(ne, b, cap, d) for y, cap in zip(ys, caps)]


def _combine_kernel(r0_ref, pos_ref, y_ref, x_ref, o_ref, acc_ref, *, w, ne, cap):
    b, j = pl.program_id(0), pl.program_id(2)
    tbk = TOKEN_BLOCK
    sub = lax.broadcasted_iota(I32, (w, tbk), 0)
    align = 16

    def window(e, off, first_rank):
        rank = off + sub
        if first_rank is not None:
            rank = jnp.where(rank >= first_rank, rank, -2)
        oh = jnp.where(pos_ref[0, 0, e:e + 1, :] == rank, 1.0, 0.0).astype(BF16)
        return lax.dot_general(oh, y_ref[e, 0, pl.ds(off, w), :], _TN, preferred_element_type=F32)

    acc = x_ref[0]
    more = []
    for e in range(ne):
        off, n_more = _windows(r0_ref, b * ne + e, j, w, cap, align)
        acc = acc + window(e, off, None)
        more.append((off, n_more))
    acc_ref[...] = acc
    for e, (off, n_more) in enumerate(more):
        def extra(i, _, e=e, off=off):
            start = off + i * w
            acc_ref[...] += window(e, pl.multiple_of(jnp.minimum(start, cap - w), align), start)
            return 0

        lax.fori_loop(1, n_more + 1, extra, 0)
    o_ref[0] = acc_ref[...]


def _combine(r0, pos, y, x, row_off):
    ne, b, cap, d = y.shape
    t = pos.shape[2]
    tbk = TOKEN_BLOCK
    nblk = t // tbk
    assert row_off % tbk == 0
    boff = row_off // tbk
    w = min(cap, 256)
    dh = d // 2
    pos4 = pos.reshape(b, ne, nblk, tbk).transpose(0, 2, 1, 3)
    grid_spec = pltpu.PrefetchScalarGridSpec(
        num_scalar_prefetch=1,
        grid=(b, 2, nblk),
        in_specs=[pl.BlockSpec((1, 1, ne, tbk), lambda bb, c, j, r: (bb, j, 0, 0)),
                  pl.BlockSpec((ne, 1, cap, dh), lambda bb, c, j, r: (0, bb, 0, c)),
                  pl.BlockSpec((1, tbk, dh), lambda bb, c, j, r: (bb, j + boff, c))],
        out_specs=pl.BlockSpec((1, tbk, dh), lambda bb, c, j, r: (bb, j, c)),
        scratch_shapes=[pltpu.VMEM((tbk, dh), F32)],
    )
    return pl.pallas_call(
        functools.partial(_combine_kernel, w=w, ne=ne, cap=cap),
        grid_spec=grid_spec,
        out_shape=jax.ShapeDtypeStruct((b, t, d), F32),
        compiler_params=_params(("arbitrary", "arbitrary", "arbitrary")),
        name="moe_combine",
    )(r0.reshape(b * ne, LANES), pos4, y, x)


def _final_kernel(x_ref, g_ref, o_ref):
    x = x_ref[0]
    ms = jnp.mean(x * x, axis=-1, keepdims=True)
    o_ref[0] = x * lax.rsqrt(ms + NORM_EPS) * g_ref[...]


def _final(x, g):
    b, t, d = x.shape
    tm = _pick(t, (512, 256, 128))
    return pl.pallas_call(
        _final_kernel,
        grid=(b, t // tm),
        in_specs=[pl.BlockSpec((1, tm, d), lambda bb, i: (bb, i, 0)),
                  pl.BlockSpec((1, d), lambda bb, i: (0, 0))],
        out_specs=pl.BlockSpec((1, tm, d), lambda bb, i: (bb, i, 0)),
        out_shape=jax.ShapeDtypeStruct((b, t, d), F32),
        compiler_params=_params(("arbitrary", "arbitrary")),
        name="final_norm",
    )(x, g.reshape(1, d))


def _moe(aff, hb, x, sets, w_gate, w_up, w_down, layer):
    ne = aff.shape[1]
    routed, groups = [], []
    for off, t, gate_f in sets:
        cap = 2 * t // ne
        pos, gsel, r0 = _topk(aff, cap, t, off)
        xs, gs = _gather(r0, pos, gsel, hb, cap, off)
        routed.append((r0, pos, off))
        groups.append((xs, gs, gate_f))
    ys = _ffn(groups, w_gate, w_up, w_down, layer)
    return [_combine(r0, pos, y, x, off) for (r0, pos, off), y in zip(routed, ys)]


def _rope_tables(t, n_ctx):
    rows = t // GRID_W
    row = jnp.repeat(jnp.arange(rows, dtype=F32), GRID_W)
    col = jnp.tile(jnp.arange(GRID_W, dtype=F32), rows)
    pairs = HEAD_W // 8
    freq = ROPE_BASE ** (-jnp.arange(pairs, dtype=F32) / pairs)
    ang = jnp.concatenate([row[:, None] * freq, col[:, None] * freq], axis=-1)
    cos, sin = jnp.cos(ang), jnp.sin(ang)
    cos = jnp.tile(cos, (1, 4))
    sin = jnp.tile(jnp.concatenate([-sin, sin], axis=-1), (1, 2))
    cos = jnp.concatenate([cos, jnp.ones((n_ctx, HEAD_W), F32)], axis=0)
    sin = jnp.concatenate([sin, jnp.zeros((n_ctx, HEAD_W), F32)], axis=0)
    qs = (HEAD_W // 2) ** -0.5 * math.log2(math.e)
    return jnp.stack([cos * qs, cos]), jnp.stack([sin * qs, sin])


def _diff_w_ext(w):
    d = w.shape[0]
    wv = w[:, 2 * d:].reshape(d, N_HEADS, HEAD_W)
    wv = jnp.concatenate([wv, jnp.zeros_like(wv)], axis=-1).reshape(d, 2 * d)
    return jnp.concatenate([w[:, :2 * d], wv], axis=1).astype(BF16)


def kernel(x, c, ctx, c_ctx, ada_w, ada_b, norm_mix, norm_ffn, norm_final, hgrn_w_in, hgrn_lb_logits, hgrn_norm, hgrn_w_out, diff_w_in, diff_lambda, diff_subln, diff_w_out, moe_router, moe_w_gate, moe_w_up, moe_w_down):
    b, t, d = x.shape
    n_ctx = ctx.shape[1]
    depth = ada_w.shape[0]
    ne = moe_router.shape[2]
    assert depth == 2 and d == N_HEADS * HEAD_W

    cvec = jnp.concatenate([c, c_ctx[None, :], jnp.zeros((8 - b - 1, d), F32)], axis=0)
    mod = _ada(cvec, ada_w, ada_b)
    lower_bounds = jnp.cumsum(jax.nn.softmax(hgrn_lb_logits.astype(F32), axis=0), axis=0)

    def kinds(layer, k, plus_one=False):
        m = mod[layer, :, k * d:(k + 1) * d]
        v = jnp.stack([jnp.broadcast_to(m[b], (b, d)), m[:b]], axis=1)
        return 1.0 + v if plus_one else v

    x_cat = jnp.concatenate([x, ctx], axis=1)

    proj = _inproj(x_cat, norm_mix[0], kinds(0, 1, True), kinds(0, 0), hgrn_w_in[0].astype(BF16), t, F32)
    o_f = _gla(proj, lower_bounds[0], t, False)
    o_b = _gla(proj, lower_bounds[0], t, True)
    x_cat = _hgrn_out(o_f, o_b, proj, hgrn_norm[0], hgrn_w_out[0].astype(BF16), x_cat, kinds(0, 2), t)
    hb, aff = _moe_pre(x_cat, norm_ffn[0], kinds(0, 4, True), kinds(0, 3), moe_router[0], t)
    gate_f = kinds(0, 5)
    x_lat, x_ctx = _moe(aff, hb, x_cat, [(0, t, gate_f[:, 1:2]), (t, n_ctx, gate_f[:, 0:1])],
                        moe_w_gate, moe_w_up, moe_w_down, 0)
    x_cat = jnp.concatenate([x_lat, x_ctx], axis=1)

    lam_init = 0.8 - 0.6 * math.exp(-0.3 * 1)
    qkv = _inproj(x_cat, norm_mix[1], kinds(1, 1, True), kinds(1, 0), _diff_w_ext(diff_w_in[0]), t, BF16,
                  rope_tabs=_rope_tables(t, n_ctx))
    att = _diff_attn(qkv, diff_lambda[0], diff_subln[0], t, lam_init)
    x_lat = _outproj(att, diff_w_out[0].astype(BF16), x_cat, kinds(1, 2)[:, 1:2])
    hb, aff = _moe_pre(x_lat, norm_ffn[1], kinds(1, 4, True), kinds(1, 3), moe_router[1], t)
    (x_lat,) = _moe(aff, hb, x_lat, [(0, t, kinds(1, 5)[:, 1:2])], moe_w_gate, moe_w_up, moe_w_down, 1)
    return _final(x_lat, norm_final)
```

```python
import functools
import math

import jax
import jax.numpy as jnp
import numpy as np
from jax import lax
from jax.experimental import pallas as pl
from jax.experimental.pallas import tpu as pltpu

F32 = jnp.float32
BF16 = jnp.bfloat16
I32 = jnp.int32

NORM_EPS = 1e-6
LANES = 128
HEAD_W = 128
N_HEADS = 8
GLA_CHUNK = 64
GLA_LEVELS = (32, 16, 8, 4, 2, 1)
ROPE_BASE = 10000.0
GRID_W = 64
TOKEN_BLOCK = 256
VMEM_LIMIT = 56 * 1024 * 1024

_NT = (((1,), (1,)), ((), ()))
_TN = (((0,), (0,)), ((), ()))


def _pick(n, cands):
    for c in cands:
        if n % c == 0:
            return c
    raise ValueError(f"no tile for {n} in {cands}")


def _params(sem):
    return pltpu.CompilerParams(dimension_semantics=sem, vmem_limit_bytes=VMEM_LIMIT)


def _sigmoid(x):
    return 1.0 / (1.0 + jnp.exp(-x))


def _norm_mod(x, g, scale2, shift2, row0, n_lat):
    ms = jnp.mean(x * x, axis=-1, keepdims=True)
    y = x * lax.rsqrt(ms + NORM_EPS) * g
    rows = row0 + lax.broadcasted_iota(I32, (x.shape[0], 1), 0)
    is_ctx = rows >= n_lat
    sc = jnp.where(is_ctx, scale2[0:1], scale2[1:2])
    sh = jnp.where(is_ctx, shift2[0:1], shift2[1:2])
    return y * sc + sh


def _ada_kernel(c_ref, w_ref, b_ref, o_ref):
    c = c_ref[...]
    s = c * _sigmoid(c)
    o_ref[0] = jnp.dot(s, w_ref[0], precision=lax.Precision.HIGHEST,
                       preferred_element_type=F32) + b_ref[0]


def _ada(cvec, ada_w, ada_b):
    depth, d, n = ada_w.shape
    rows = cvec.shape[0]
    tn = _pick(n, (1024, 512, 256, 128))
    return pl.pallas_call(
        _ada_kernel,
        grid=(depth, n // tn),
        in_specs=[pl.BlockSpec((rows, d), lambda l, j: (0, 0)),
                  pl.BlockSpec((1, d, tn), lambda l, j: (l, 0, j)),
                  pl.BlockSpec((1, 1, tn), lambda l, j: (l, 0, j))],
        out_specs=pl.BlockSpec((1, rows, tn), lambda l, j: (l, 0, j)),
        out_shape=jax.ShapeDtypeStruct((depth, rows, n), F32),
        compiler_params=_params(("arbitrary", "arbitrary")),
        name="adaln",
    )(cvec, ada_w, ada_b.reshape(depth, 1, n))


def _inproj_kernel(x_ref, g_ref, sc_ref, sh_ref, w_ref, *rest, tm, tn, n_lat, rope):
    if rope:
        cos_ref, sin_ref, o_ref = rest
    else:
        (o_ref,) = rest
    i = pl.program_id(1)
    h = _norm_mod(x_ref[0], g_ref[...], sc_ref[0], sh_ref[0], i * tm, n_lat).astype(BF16)
    for n in range(w_ref.shape[1] // tn):
        cols = slice(n * tn, (n + 1) * tn)
        acc = jnp.dot(h, w_ref[:, cols], preferred_element_type=F32)
        if not rope:
            o_ref[0, :, cols] = acc.astype(o_ref.dtype)
        elif n < 2:
            cos = cos_ref[n]
            sin = sin_ref[n]
            lane = lax.broadcasted_iota(I32, (1, HEAD_W), 1)
            first = (lane % 64) < 32
            for hd in range(tn // HEAD_W):
                a = acc[:, hd * HEAD_W:(hd + 1) * HEAD_W]
                rot = jnp.where(first, pltpu.roll(a, HEAD_W - 32, 1), pltpu.roll(a, 32, 1))
                o_ref[0, :, n * tn + hd * HEAD_W:n * tn + (hd + 1) * HEAD_W] = (
                    a * cos + rot * sin).astype(o_ref.dtype)
        else:
            col = lax.broadcasted_iota(I32, (1, tn), 1)
            o_ref[0, :, cols] = jnp.where(col % (2 * HEAD_W) >= HEAD_W, 1.0, acc).astype(o_ref.dtype)


def _inproj(x, g, scale2, shift2, w_bf16, n_lat, out_dtype, rope_tabs=None):
    b, r, d = x.shape
    n = w_bf16.shape[1]
    tm = _pick(r, (384, 256, 128))
    tn = 1024
    rope = rope_tabs is not None
    in_specs = [pl.BlockSpec((1, tm, d), lambda bb, i: (bb, i, 0)),
                pl.BlockSpec((1, d), lambda bb, i: (0, 0)),
                pl.BlockSpec((1, 2, d), lambda bb, i: (bb, 0, 0)),
                pl.BlockSpec((1, 2, d), lambda bb, i: (bb, 0, 0)),
                pl.BlockSpec((d, n), lambda bb, i: (0, 0))]
    args = [x, g.reshape(1, d), scale2, shift2, w_bf16]
    if rope:
        cos_t, sin_t = rope_tabs
        spec = pl.BlockSpec((2, tm, HEAD_W), lambda bb, i: (0, i, 0))
        in_specs += [spec, spec]
        args += [cos_t, sin_t]
    return pl.pallas_call(
        functools.partial(_inproj_kernel, tm=tm, tn=tn, n_lat=n_lat, rope=rope),
        grid=(b, r // tm),
        in_specs=in_specs,
        out_specs=pl.BlockSpec((1, tm, n), lambda bb, i: (bb, i, 0)),
        out_shape=jax.ShapeDtypeStruct((b, r, n), out_dtype),
        compiler_params=_params(("arbitrary", "arbitrary")),
        name="inproj_rope" if rope else "inproj",
    )(*args)


def _gla_consts(reverse):
    c = GLA_CHUNK
    nl = len(GLA_LEVELS)
    t = np.arange(c)
    p = c - 1 - t if reverse else t
    rowsel = np.zeros((nl, c, HEAD_W), np.float32)
    masks = np.zeros((nl + 1, c, c), np.float32)
    for li, s in enumerate(GLA_LEVELS):
        blk = p // (2 * s)
        second = (p // s) % 2 == 1
        rowsel[li] = second[:, None]
        masks[li] = (blk[:, None] == blk[None, :]) & second[:, None] & (~second)[None, :]
    masks[nl] = np.eye(c)
    return rowsel, masks


def _cumsum_steps(x, reverse):
    c = x.shape[0]
    row = lax.broadcasted_iota(I32, x.shape, 0)
    d = 1
    while d < c:
        if reverse:
            x = x + jnp.where(row < c - d, pltpu.roll(x, c - d, 0), 0.0)
        else:
            x = x + jnp.where(row >= d, pltpu.roll(x, d, 0), 0.0)
        d *= 2
    return x


def _level_ref(b, s, reverse):
    c = b.shape[0]
    at = s if reverse else s - 1
    if 2 * s >= 8:
        pieces = [jnp.broadcast_to(b[blk * 2 * s + at:blk * 2 * s + at + 1], (2 * s, HEAD_W))
                  for blk in range(c // (2 * s))]
        return pieces[0] if len(pieces) == 1 else jnp.concatenate(pieces, axis=0)
    sub = lax.broadcasted_iota(I32, (8, HEAD_W), 0)
    outs = []
    for v in range(c // 8):
        acc = None
        for blk in range(8 // (2 * s)):
            row = 8 * v + blk * 2 * s + at
            cand = jnp.broadcast_to(b[row:row + 1], (8, HEAD_W))
            acc = cand if acc is None else jnp.where(sub >= blk * 2 * s, cand, acc)
        outs.append(acc)
    return jnp.concatenate(outs, axis=0)


def _gla_kernel(q_ref, f_ref, v_ref, lb_ref, rowsel_ref, masks_ref, o_ref, st_ref, *, tb, hg, reverse):
    c = GLA_CHUNK
    nl = len(GLA_LEVELS)
    nc = tb // c

    @pl.when(pl.program_id(2) == 0)
    def _():
        st_ref[...] = jnp.zeros_like(st_ref)

    last = 0 if reverse else c - 1
    starts = [(nc - 1 - ci if reverse else ci) * c for ci in range(nc)]
    units = [(hd, r0) for hd in range(hg) for r0 in starts]

    qf, kk, vb, bc = [], [], [], []
    for hd, r0 in units:
        lanes = slice(hd * HEAD_W, (hd + 1) * HEAD_W)
        lb = lb_ref[0, :, lanes]
        q = q_ref[0, r0:r0 + c, lanes]
        qf.append(q * _sigmoid(q))
        f = lb + (1.0 - lb) * _sigmoid(f_ref[0, r0:r0 + c, lanes])
        lf = jnp.log(f)
        kk.append(1.0 - f)
        vb.append(v_ref[0, r0:r0 + c, lanes].astype(BF16))
        bc.append(_cumsum_steps(lf, reverse))

    a = []
    for i in range(len(units)):
        ai = masks_ref[nl] * jnp.sum(qf[i] * kk[i], axis=1, keepdims=True)
        for li, s in enumerate(GLA_LEVELS):
            e = jnp.exp(-jnp.abs(bc[i] - _level_ref(bc[i], s, reverse)))
            xb = (jnp.where(rowsel_ref[li] > 0, qf[i], kk[i]) * e).astype(BF16)
            ai = ai + masks_ref[li] * lax.dot_general(xb, xb, _NT, preferred_element_type=F32)
        a.append(ai.astype(BF16))

    o_intra = [jnp.dot(a[i], vb[i], preferred_element_type=F32) for i in range(len(units))]
    upd = []
    for i in range(len(units)):
        kdec = (kk[i] * jnp.exp(bc[i][last:last + 1] - bc[i])).astype(BF16)
        upd.append(lax.dot_general(vb[i], kdec, _TN, preferred_element_type=F32))

    st = [st_ref[hd] for hd in range(hg)]
    for i, (hd, r0) in enumerate(units):
        qd = (qf[i] * jnp.exp(bc[i])).astype(BF16)
        o_ref[0, r0:r0 + c, hd * HEAD_W:(hd + 1) * HEAD_W] = o_intra[i] + lax.dot_general(
            qd, st[hd].astype(BF16), _NT, preferred_element_type=F32)
        st[hd] = st[hd] * jnp.exp(bc[i][last:last + 1]) + upd[i]
    for hd in range(hg):
        st_ref[hd] = st[hd]


def _gla(proj, lb, n_lat, reverse):
    b, r, _ = proj.shape
    tb = TOKEN_BLOCK
    assert n_lat % tb == 0 and r % tb == 0
    nblk = r // tb
    nlb = n_lat // tb
    ncb = nblk - nlb
    rowsel, masks = _gla_consts(reverse)
    nl = len(GLA_LEVELS)
    c = GLA_CHUNK
    hg = 8
    wblk = hg * HEAD_W
    fcol = N_HEADS // hg * (2 if reverse else 1)

    def blk(n):
        if reverse:
            return jnp.where(n < ncb, nblk - 1 - n, nlb - 1 - (n - ncb))
        return jnp.where(n < ncb, nlb + n, n - ncb)

    return pl.pallas_call(
        functools.partial(_gla_kernel, tb=tb, hg=hg, reverse=reverse),
        grid=(b, N_HEADS // hg, nblk),
        in_specs=[pl.BlockSpec((1, tb, wblk), lambda bb, h, n: (bb, blk(n), h)),
                  pl.BlockSpec((1, tb, wblk), lambda bb, h, n: (bb, blk(n), fcol + h)),
                  pl.BlockSpec((1, tb, wblk), lambda bb, h, n: (bb, blk(n), 3 * N_HEADS // hg + h)),
                  pl.BlockSpec((1, 1, wblk), lambda bb, h, n: (h, 0, 0)),
                  pl.BlockSpec((nl, c, HEAD_W), lambda bb, h, n: (0, 0, 0)),
                  pl.BlockSpec((nl + 1, c, c), lambda bb, h, n: (0, 0, 0))],
        out_specs=pl.BlockSpec((1, tb, wblk), lambda bb, h, n: (bb, blk(n), h)),
        out_shape=jax.ShapeDtypeStruct((b, r, N_HEADS * HEAD_W), F32),
        scratch_shapes=[pltpu.VMEM((hg, HEAD_W, HEAD_W), F32)],
        compiler_params=_params(("arbitrary", "arbitrary", "arbitrary")),
        name="gla_bwd" if reverse else "gla_fwd",
    )(proj, proj, proj, lb.reshape(N_HEADS // hg, 1, wblk), jnp.asarray(rowsel), jnp.asarray(masks))


def _hgrn_out_kernel(of_ref, ob_ref, g_ref, ng_ref, w_ref, x_ref, gate_ref, o_ref, y_ref, *, tm, n_lat):
    i = pl.program_id(1)
    o = of_ref[0] + ob_ref[0]
    for h in range(N_HEADS):
        sl = slice(h * HEAD_W, (h + 1) * HEAD_W)
        oh = o[:, sl]
        ms = jnp.mean(oh * oh, axis=-1, keepdims=True)
        g = g_ref[0, :, sl]
        y_ref[:, sl] = (oh * lax.rsqrt(ms + NORM_EPS) * ng_ref[:, sl] * (g * _sigmoid(g))).astype(BF16)
    y = jnp.dot(y_ref[...], w_ref[...], preferred_element_type=F32)
    rows = i * tm + lax.broadcasted_iota(I32, (tm, 1), 0)
    gate = jnp.where(rows >= n_lat, gate_ref[0, 0:1], gate_ref[0, 1:2])
    o_ref[0] = x_ref[0] + gate * y


def _hgrn_out(o_f, o_b, proj, norm_g, w_bf16, x, gate2, n_lat):
    b, r, d = x.shape
    tm = _pick(r, (384, 256, 128))
    return pl.pallas_call(
        functools.partial(_hgrn_out_kernel, tm=tm, n_lat=n_lat),
        grid=(b, r // tm),
        in_specs=[pl.BlockSpec((1, tm, d), lambda bb, i: (bb, i, 0)),
                  pl.BlockSpec((1, tm, d), lambda bb, i: (bb, i, 0)),
                  pl.BlockSpec((1, tm, d), lambda bb, i: (bb, i, 4)),
                  pl.BlockSpec((1, d), lambda bb, i: (0, 0)),
                  pl.BlockSpec((d, d), lambda bb, i: (0, 0)),
                  pl.BlockSpec((1, tm, d), lambda bb, i: (bb, i, 0)),
                  pl.BlockSpec((1, 2, d), lambda bb, i: (bb, 0, 0))],
        out_specs=pl.BlockSpec((1, tm, d), lambda bb, i: (bb, i, 0)),
        out_shape=jax.ShapeDtypeStruct((b, r, d), F32),
        scratch_shapes=[pltpu.VMEM((tm, d), BF16)],
        compiler_params=_params(("arbitrary", "arbitrary")),
        name="hgrn_out",
    )(o_f, o_b, proj, jnp.tile(norm_g, N_HEADS).reshape(1, d), w_bf16, x, gate2)


def _attn_kernel(q_ref, k_ref, v_ref, lam_ref, g_ref, o_ref, *, tq, tk, lam_init):
    nk = k_ref.shape[1] // tk
    q = q_ref[0]
    lane = lax.broadcasted_iota(I32, (1, HEAD_W), 1)
    zero = jnp.zeros_like(q)
    qc = [jnp.where(lane < 64, q, zero), jnp.where(lane >= 64, q, zero)]

    def scores(u):
        comp, ki = u
        return lax.dot_general(qc[comp], k_ref[0, ki * tk:(ki + 1) * tk, :], _NT, preferred_element_type=F32)

    units = [(comp, ki) for ki in range(nk) for comp in range(2)]
    m = [jnp.full((tq, LANES), -jnp.inf, F32)] * 2
    acc = [jnp.zeros((tq, 2 * HEAD_W), F32)] * 2
    s_next = scores(units[0])
    for idx, (comp, ki) in enumerate(units):
        s = s_next
        if idx + 1 < len(units):
            s_next = scores(units[idx + 1])
        m_new = jnp.maximum(m[comp], jnp.max(s, axis=-1, keepdims=True))
        p = jnp.exp2(s - jnp.tile(m_new, (1, tk // LANES)))
        alpha = jnp.exp2(m[comp] - m_new)
        v1 = v_ref[0, ki * tk:(ki + 1) * tk, :]
        acc[comp] = jnp.tile(alpha, (1, 2)) * acc[comp] + jnp.dot(p.astype(BF16), v1, preferred_element_type=F32)
        m[comp] = m_new
    out = [a[:, :HEAD_W] / a[:, HEAD_W:] for a in acc]

    lv = lam_ref[...]
    s01 = jnp.sum(lv[0:1] * lv[1:2], axis=-1, keepdims=True)
    s23 = jnp.sum(lv[2:3] * lv[3:4], axis=-1, keepdims=True)
    lam = jnp.exp(s01) - jnp.exp(s23) + lam_init
    o = out[0] - lam * out[1]
    ms = jnp.mean(o * o, axis=-1, keepdims=True)
    o_ref[0] = (o * lax.rsqrt(ms + NORM_EPS) * g_ref[...] * (1.0 - lam_init)).astype(o_ref.dtype)


def _diff_attn(qkv, lam_vecs, subln_g, t, lam_init):
    b, r, _ = qkv.shape
    tq = _pick(t, (512, 256, 128))
    tk = _pick(r, (768, 1280, 512, 256))
    return pl.pallas_call(
        functools.partial(_attn_kernel, tq=tq, tk=tk, lam_init=lam_init),
        grid=(b, N_HEADS, t // tq),
        in_specs=[pl.BlockSpec((1, tq, HEAD_W), lambda bb, h, i: (bb, i, h)),
                  pl.BlockSpec((1, r, HEAD_W), lambda bb, h, i: (bb, 0, N_HEADS + h)),
                  pl.BlockSpec((1, r, 2 * HEAD_W), lambda bb, h, i: (bb, 0, N_HEADS + h)),
                  pl.BlockSpec(lam_vecs.shape, lambda bb, h, i: (0, 0)),
                  pl.BlockSpec((1, HEAD_W), lambda bb, h, i: (0, 0))],
        out_specs=pl.BlockSpec((1, tq, HEAD_W), lambda bb, h, i: (bb, i, h)),
        out_shape=jax.ShapeDtypeStruct((b, t, N_HEADS * HEAD_W), BF16),
        compiler_params=_params(("arbitrary", "arbitrary", "arbitrary")),
        name="diff_attn",
    )(qkv, qkv, qkv, lam_vecs.astype(F32), subln_g.reshape(1, HEAD_W).astype(F32))


def _outproj_kernel(a_ref, w_ref, x_ref, gate_ref, o_ref):
    y = jnp.dot(a_ref[0], w_ref[...], preferred_element_type=F32)
    o_ref[0] = x_ref[0] + gate_ref[0] * y


def _outproj(a, w_bf16, x_cat, gate):
    b, t, d = a.shape
    tm = _pick(t, (512, 256, 128))
    return pl.pallas_call(
        _outproj_kernel,
        grid=(b, t // tm),
        in_specs=[pl.BlockSpec((1, tm, d), lambda bb, i: (bb, i, 0)),
                  pl.BlockSpec((d, d), lambda bb, i: (0, 0)),
                  pl.BlockSpec((1, tm, d), lambda bb, i: (bb, i, 0)),
                  pl.BlockSpec((1, 1, d), lambda bb, i: (bb, 0, 0))],
        out_specs=pl.BlockSpec((1, tm, d), lambda bb, i: (bb, i, 0)),
        out_shape=jax.ShapeDtypeStruct((b, t, d), F32),
        compiler_params=_params(("arbitrary", "arbitrary")),
        name="outproj",
    )(a, w_bf16, x_cat, gate)


def _moe_pre_kernel(x_ref, g_ref, sc_ref, sh_ref, wr_ref, h_ref, aff_ref, *, tm, n_lat):
    i = pl.program_id(1)
    h = _norm_mod(x_ref[0], g_ref[...], sc_ref[0], sh_ref[0], i * tm, n_lat)
    h_ref[0] = h.astype(BF16)
    logits = lax.dot_general(wr_ref[...], h, _NT, precision=lax.Precision.HIGHEST,
                             preferred_element_type=F32)
    mx = jnp.max(logits, axis=0, keepdims=True)
    ex = jnp.exp(logits - mx)
    aff_ref[0] = ex / jnp.sum(ex, axis=0, keepdims=True)


def _moe_pre(x, g, scale2, shift2, w_router, n_lat):
    b, r, d = x.shape
    ne = w_router.shape[1]
    tm = _pick(r, (768, 512, 384, 256, 128))
    return pl.pallas_call(
        functools.partial(_moe_pre_kernel, tm=tm, n_lat=n_lat),
        grid=(b, r // tm),
        in_specs=[pl.BlockSpec((1, tm, d), lambda bb, i: (bb, i, 0)),
                  pl.BlockSpec((1, d), lambda bb, i: (0, 0)),
                  pl.BlockSpec((1, 2, d), lambda bb, i: (bb, 0, 0)),
                  pl.BlockSpec((1, 2, d), lambda bb, i: (bb, 0, 0)),
                  pl.BlockSpec((ne, d), lambda bb, i: (0, 0))],
        out_specs=[pl.BlockSpec((1, tm, d), lambda bb, i: (bb, i, 0)),
                   pl.BlockSpec((1, ne, tm), lambda bb, i: (bb, 0, i))],
        out_shape=[jax.ShapeDtypeStruct((b, r, d), BF16),
                   jax.ShapeDtypeStruct((b, ne, r), F32)],
        compiler_params=_params(("arbitrary", "arbitrary")),
        name="moe_pre",
    )(x, g.reshape(1, d), scale2, shift2, w_router.T)


def _topk_kernel(aff_ref, tri_ref, pos_ref, gsel_ref, r0_ref, *, cap):
    ne, t = aff_ref.shape[1], aff_ref.shape[2]
    tbk = TOKEN_BLOCK
    nblk = t // tbk
    bits = pltpu.bitcast(aff_ref[0], I32)
    thr = jnp.zeros((ne, 1), I32)
    for bit in range(30, -1, -1):
        cand = thr | (1 << bit)
        cnt = jnp.sum(jnp.where(bits >= cand, 1.0, 0.0), axis=1, keepdims=True)
        thr = jnp.where(cnt >= cap, cand, thr)
    n_gt = jnp.sum(jnp.where(bits > thr, 1.0, 0.0), axis=1, keepdims=True)
    need = cap - n_gt
    tri = tri_ref[...]
    carry_eq = jnp.zeros((ne, 1), F32)
    carry_sel = jnp.zeros((ne, 1), F32)
    r0_ref[0] = jnp.full((ne, LANES), cap, I32)
    for j in range(nblk):
        sl = slice(j * tbk, (j + 1) * tbk)
        a = aff_ref[0, :, sl]
        bj = pltpu.bitcast(a, I32)
        eq = jnp.where(bj == thr, 1.0, 0.0)
        gt = jnp.where(bj > thr, 1.0, 0.0)
        pe = jnp.dot(eq.astype(BF16), tri, preferred_element_type=F32) + carry_eq
        sel = gt + eq * jnp.where(pe - eq < need, 1.0, 0.0)
        ps = jnp.dot(sel.astype(BF16), tri, preferred_element_type=F32) + carry_sel
        pos_ref[0, :, sl] = jnp.where(sel > 0, ps - 1.0, -1.0).astype(I32)
        gsel_ref[0, :, sl] = a * sel
        r0_ref[0, :, j:j + 1] = carry_sel.astype(I32)
        carry_eq = pe[:, tbk - 1:tbk]
        carry_sel = ps[:, tbk - 1:tbk]


def _topk(aff, cap, t, tok_off):
    b, ne, _ = aff.shape
    tbk = TOKEN_BLOCK
    assert t % tbk == 0 and t // tbk < LANES and tok_off % t == 0
    tri = jnp.asarray(np.triu(np.ones((tbk, tbk), np.float32)), BF16)
    return pl.pallas_call(
        functools.partial(_topk_kernel, cap=cap),
        grid=(b,),
        in_specs=[pl.BlockSpec((1, ne, t), lambda bb: (bb, 0, tok_off // t)),
                  pl.BlockSpec((tbk, tbk), lambda bb: (0, 0))],
        out_specs=[pl.BlockSpec((1, ne, t), lambda bb: (bb, 0, 0)),
                   pl.BlockSpec((1, ne, t), lambda bb: (bb, 0, 0)),
                   pl.BlockSpec((1, ne, LANES), lambda bb: (bb, 0, 0))],
        out_shape=[jax.ShapeDtypeStruct((b, ne, t), I32),
                   jax.ShapeDtypeStruct((b, ne, t), F32),
                   jax.ShapeDtypeStruct((b, ne, LANES), I32)],
        compiler_params=_params(("arbitrary",)),
        name="topk",
    )(aff, tri)


def _windows(r0_ref, row, j, w, cap, align):
    lo = r0_ref[row, j]
    hi = r0_ref[row, j + 1]
    off = pl.multiple_of(jnp.minimum((lo // align) * align, cap - w), align)
    return off, (jnp.maximum(hi - (off + w), 0) + w - 1) // w


def _gather_kernel(r0_ref, pos_ref, g_ref, h_ref, xs_ref, gs_ref, acc_ref, gacc_ref, *, w, nblk, ne, cap):
    b, e = pl.program_id(0), pl.program_id(1)
    row = b * ne + e
    tbk = TOKEN_BLOCK
    acc_ref[...] = jnp.zeros_like(acc_ref)
    gacc_ref[...] = jnp.zeros_like(gacc_ref)
    sub = lax.broadcasted_iota(I32, (w, tbk), 0)

    def window(j, off, first_rank):
        rank = off + sub
        if first_rank is not None:
            rank = jnp.where(rank >= first_rank, rank, -2)
        hit = pos_ref[0, 0, j:j + 1, :] == rank
        oh = jnp.where(hit, 1.0, 0.0).astype(BF16)
        acc_ref[pl.ds(off, w), :] += jnp.dot(oh, h_ref[0, j * tbk:(j + 1) * tbk, :], preferred_element_type=F32)
        gsum = jnp.sum(jnp.where(hit, g_ref[0, 0, j:j + 1, :], 0.0), axis=1, keepdims=True)
        gacc_ref[pl.ds(off, w), :] += jnp.broadcast_to(gsum, (w, LANES))

    for j in range(nblk):
        off, n_more = _windows(r0_ref, row, j, w, cap, 8)
        window(j, off, None)

        def more(i, _, j=j, off=off):
            start = off + i * w
            window(j, pl.multiple_of(jnp.minimum(start, cap - w), 8), start)
            return 0

        lax.fori_loop(1, n_more + 1, more, 0)
    xs_ref[0, 0] = acc_ref[...].astype(BF16)
    gs_ref[0, 0] = gacc_ref[...]


def _gather(r0, pos, gate, hb, cap, tok_off):
    b, ne, t = pos.shape
    d = hb.shape[2]
    assert tok_off % t == 0
    tbk = TOKEN_BLOCK
    nblk = t // tbk
    w = min(cap, 64)
    pos4 = pos.reshape(b, ne, nblk, tbk)
    g4 = gate.reshape(b, ne, nblk, tbk)
    grid_spec = pltpu.PrefetchScalarGridSpec(
        num_scalar_prefetch=1,
        grid=(b, ne),
        in_specs=[pl.BlockSpec((1, 1, nblk, tbk), lambda bb, e, r: (bb, e, 0, 0)),
                  pl.BlockSpec((1, 1, nblk, tbk), lambda bb, e, r: (bb, e, 0, 0)),
                  pl.BlockSpec((1, t, d), lambda bb, e, r: (bb, tok_off // t, 0))],
        out_specs=[pl.BlockSpec((1, 1, cap, d), lambda bb, e, r: (e, bb, 0, 0)),
                   pl.BlockSpec((1, 1, cap, LANES), lambda bb, e, r: (e, bb, 0, 0))],
        scratch_shapes=[pltpu.VMEM((cap, d), F32), pltpu.VMEM((cap, LANES), F32)],
    )
    return pl.pallas_call(
        functools.partial(_gather_kernel, w=w, nblk=nblk, ne=ne, cap=cap),
        grid_spec=grid_spec,
        out_shape=[jax.ShapeDtypeStruct((ne, b, cap, d), BF16),
                   jax.ShapeDtypeStruct((ne, b, cap, LANES), F32)],
        compiler_params=_params(("arbitrary", "arbitrary")),
        name="moe_gather",
    )(r0.reshape(b * ne, LANES), pos4, g4, hb)


def _ffn_kernel(*refs, nb, caps, rcs):
    ng = len(caps)
    ins, (wg_ref, wu_ref, wd_ref) = refs[:3 * ng], refs[3 * ng:3 * ng + 3]
    y_refs, acc_refs = refs[3 * ng + 3:4 * ng + 3], refs[4 * ng + 3:]
    f = pl.program_id(1)
    nf = pl.num_programs(1)

    @pl.when(f == 0)
    def _():
        for acc_ref in acc_refs:
            acc_ref[...] = jnp.zeros_like(acc_ref)

    wg = wg_ref[0, 0].astype(BF16)
    wu = wu_ref[0, 0].astype(BF16)
    wd = wd_ref[0, 0].astype(BF16)
    chunks = [(g, c0, rcs[g]) for g in range(ng) for c0 in range(0, nb * caps[g], rcs[g])]

    def gate_up(chunk):
        g, c0, rc = chunk
        x = ins[3 * g][0, c0:c0 + rc, :]
        return jnp.dot(x, wg, preferred_element_type=F32), jnp.dot(x, wu, preferred_element_type=F32)

    nxt = gate_up(chunks[0])
    for i, (g, c0, rc) in enumerate(chunks):
        a, u = nxt
        if i + 1 < len(chunks):
            nxt = gate_up(chunks[i + 1])
        hm = (a * _sigmoid(a) * u).astype(BF16)
        acc_refs[g][c0:c0 + rc, :] += jnp.dot(hm, wd, preferred_element_type=F32)

    @pl.when(f == nf - 1)
    def _():
        for g in range(ng):
            gs_ref, gt_ref = ins[3 * g + 1], ins[3 * g + 2]
            for bb in range(nb):
                sl = slice(bb * caps[g], (bb + 1) * caps[g])
                y_refs[g][0, sl, :] = (acc_refs[g][sl, :] * gs_ref[0, sl, 0:1] * gt_ref[bb]).astype(BF16)


def _ffn(groups, w_gate, w_up, w_down, layer):
    ne, b, _, d = groups[0][0].shape
    fdim = w_gate.shape[3]
    tf = _pick(fdim, (256, 128))
    caps = [xs.shape[2] for xs, _, _ in groups]
    rcs = [_pick(b * cap, (1024, 512, 256, 128, 64)) for cap in caps]
    in_specs, args = [], []
    for (xs, gs, gate_f), cap in zip(groups, caps):
        rows = b * cap
        in_specs += [pl.BlockSpec((1, rows, d), lambda e, f: (e, 0, 0)),
                     pl.BlockSpec((1, rows, LANES), lambda e, f: (e, 0, 0)),
                     pl.BlockSpec((b, 1, d), lambda e, f: (0, 0, 0))]
        args += [xs.reshape(ne, rows, d), gs.reshape(ne, rows, LANES), gate_f]
    in_specs += [pl.BlockSpec((1, 1, d, tf), lambda e, f: (layer, e, 0, f)),
                 pl.BlockSpec((1, 1, d, tf), lambda e, f: (layer, e, 0, f)),
                 pl.BlockSpec((1, 1, tf, d), lambda e, f: (layer, e, f, 0))]
    ys = pl.pallas_call(
        functools.partial(_ffn_kernel, nb=b, caps=caps, rcs=rcs),
        grid=(ne, fdim // tf),
        in_specs=in_specs,
        out_specs=[pl.BlockSpec((1, b * cap, d), lambda e, f: (e, 0, 0)) for cap in caps],
        out_shape=[jax.ShapeDtypeStruct((ne, b * cap, d), BF16) for cap in caps],
        scratch_shapes=[pltpu.VMEM((b * cap, d), F32) for cap in caps],
        compiler_params=_params(("arbitrary", "arbitrary")),
        name="moe_ffn",
    )(*args, w_gate, w_up, w_down)
    return [y.reshape(ne, b, cap, d) for y, cap in zip(ys, caps)]


def _combine_kernel(r0_ref, pos_ref, y_ref, x_ref, o_ref, acc_ref, *, w, ne, cap):
    b, j = pl.program_id(0), pl.program_id(2)
    tbk = TOKEN_BLOCK
    sub = lax.broadcasted_iota(I32, (w, tbk), 0)
    align = 16

    def window(e, off, first_rank):
        rank = off + sub
        if first_rank is not None:
            rank = jnp.where(rank >= first_rank, rank, -2)
        oh = jnp.where(pos_ref[0, 0, e:e + 1, :] == rank, 1.0, 0.0).astype(BF16)
        return lax.dot_general(oh, y_ref[e, 0, pl.ds(off, w), :], _TN, preferred_element_type=F32)

    acc = x_ref[0]
    more = []
    for e in range(ne):
        off, n_more = _windows(r0_ref, b * ne + e, j, w, cap, align)
        acc = acc + window(e, off, None)
        more.append((off, n_more))
    acc_ref[...] = acc
    for e, (off, n_more) in enumerate(more):
        def extra(i, _, e=e, off=off):
            start = off + i * w
            acc_ref[...] += window(e, pl.multiple_of(jnp.minimum(start, cap - w), align), start)
            return 0

        lax.fori_loop(1, n_more + 1, extra, 0)
    o_ref[0] = acc_ref[...]


def _combine(r0, pos, y, x, row_off):
    ne, b, cap, d = y.shape
    t = pos.shape[2]
    tbk = TOKEN_BLOCK
    nblk = t // tbk
    assert row_off % tbk == 0
    boff = row_off // tbk
    w = min(cap, 256)
    dh = d // 2
    pos4 = pos.reshape(b, ne, nblk, tbk).transpose(0, 2, 1, 3)
    grid_spec = pltpu.PrefetchScalarGridSpec(
        num_scalar_prefetch=1,
        grid=(b, 2, nblk),
        in_specs=[pl.BlockSpec((1, 1, ne, tbk), lambda bb, c, j, r: (bb, j, 0, 0)),
                  pl.BlockSpec((ne, 1, cap, dh), lambda bb, c, j, r: (0, bb, 0, c)),
                  pl.BlockSpec((1, tbk, dh), lambda bb, c, j, r: (bb, j + boff, c))],
        out_specs=pl.BlockSpec((1, tbk, dh), lambda bb, c, j, r: (bb, j, c)),
        scratch_shapes=[pltpu.VMEM((tbk, dh), F32)],
    )
    return pl.pallas_call(
        functools.partial(_combine_kernel, w=w, ne=ne, cap=cap),
        grid_spec=grid_spec,
        out_shape=jax.ShapeDtypeStruct((b, t, d), F32),
        compiler_params=_params(("arbitrary", "arbitrary", "arbitrary")),
        name="moe_combine",
    )(r0.reshape(b * ne, LANES), pos4, y, x)


def _final_kernel(x_ref, g_ref, o_ref):
    x = x_ref[0]
    ms = jnp.mean(x * x, axis=-1, keepdims=True)
    o_ref[0] = x * lax.rsqrt(ms + NORM_EPS) * g_ref[...]


def _final(x, g):
    b, t, d = x.shape
    tm = _pick(t, (512, 256, 128))
    return pl.pallas_call(
        _final_kernel,
        grid=(b, t // tm),
        in_specs=[pl.BlockSpec((1, tm, d), lambda bb, i: (bb, i, 0)),
                  pl.BlockSpec((1, d), lambda bb, i: (0, 0))],
        out_specs=pl.BlockSpec((1, tm, d), lambda bb, i: (bb, i, 0)),
        out_shape=jax.ShapeDtypeStruct((b, t, d), F32),
        compiler_params=_params(("arbitrary", "arbitrary")),
        name="final_norm",
    )(x, g.reshape(1, d))


def _moe(aff, hb, x, sets, w_gate, w_up, w_down, layer):
    ne = aff.shape[1]
    routed, groups = [], []
    for off, t, gate_f in sets:
        cap = 2 * t // ne
        pos, gsel, r0 = _topk(aff, cap, t, off)
        xs, gs = _gather(r0, pos, gsel, hb, cap, off)
        routed.append((r0, pos, off))
        groups.append((xs, gs, gate_f))
    ys = _ffn(groups, w_gate, w_up, w_down, layer)
    return [_combine(r0, pos, y, x, off) for (r0, pos, off), y in zip(routed, ys)]


def _rope_tables(t, n_ctx):
    rows = t // GRID_W
    row = jnp.repeat(jnp.arange(rows, dtype=F32), GRID_W)
    col = jnp.tile(jnp.arange(GRID_W, dtype=F32), rows)
    pairs = HEAD_W // 8
    freq = ROPE_BASE ** (-jnp.arange(pairs, dtype=F32) / pairs)
    ang = jnp.concatenate([row[:, None] * freq, col[:, None] * freq], axis=-1)
    cos, sin = jnp.cos(ang), jnp.sin(ang)
    cos = jnp.tile(cos, (1, 4))
    sin = jnp.tile(jnp.concatenate([-sin, sin], axis=-1), (1, 2))
    cos = jnp.concatenate([cos, jnp.ones((n_ctx, HEAD_W), F32)], axis=0)
    sin = jnp.concatenate([sin, jnp.zeros((n_ctx, HEAD_W), F32)], axis=0)
    qs = (HEAD_W // 2) ** -0.5 * math.log2(math.e)
    return jnp.stack([cos * qs, cos]), jnp.stack([sin * qs, sin])


def _diff_w_ext(w):
    d = w.shape[0]
    wv = w[:, 2 * d:].reshape(d, N_HEADS, HEAD_W)
    wv = jnp.concatenate([wv, jnp.zeros_like(wv)], axis=-1).reshape(d, 2 * d)
    return jnp.concatenate([w[:, :2 * d], wv], axis=1).astype(BF16)


def kernel(x, c, ctx, c_ctx, ada_w, ada_b, norm_mix, norm_ffn, norm_final, hgrn_w_in, hgrn_lb_logits, hgrn_norm, hgrn_w_out, diff_w_in, diff_lambda, diff_subln, diff_w_out, moe_router, moe_w_gate, moe_w_up, moe_w_down):
    b, t, d = x.shape
    n_ctx = ctx.shape[1]
    depth = ada_w.shape[0]
    ne = moe_router.shape[2]
    assert depth == 2 and d == N_HEADS * HEAD_W

    cvec = jnp.concatenate([c, c_ctx[None, :], jnp.zeros((8 - b - 1, d), F32)], axis=0)
    mod = _ada(cvec, ada_w, ada_b)
    lower_bounds = jnp.cumsum(jax.nn.softmax(hgrn_lb_logits.astype(F32), axis=0), axis=0)

    def kinds(layer, k, plus_one=False):
        m = mod[layer, :, k * d:(k + 1) * d]
        v = jnp.stack([jnp.broadcast_to(m[b], (b, d)), m[:b]], axis=1)
        return 1.0 + v if plus_one else v

    x_cat = jnp.concatenate([x, ctx], axis=1)

    proj = _inproj(x_cat, norm_mix[0], kinds(0, 1, True), kinds(0, 0), hgrn_w_in[0].astype(BF16), t, F32)
    o_f = _gla(proj, lower_bounds[0], t, False)
    o_b = _gla(proj, lower_bounds[0], t, True)
    x_cat = _hgrn_out(o_f, o_b, proj, hgrn_norm[0], hgrn_w_out[0].astype(BF16), x_cat, kinds(0, 2), t)
    hb, aff = _moe_pre(x_cat, norm_ffn[0], kinds(0, 4, True), kinds(0, 3), moe_router[0], t)
    gate_f = kinds(0, 5)
    x_lat, x_ctx = _moe(aff, hb, x_cat, [(0, t, gate_f[:, 1:2]), (t, n_ctx, gate_f[:, 0:1])],
                        moe_w_gate, moe_w_up, moe_w_down, 0)
    x_cat = jnp.concatenate([x_lat, x_ctx], axis=1)

    lam_init = 0.8 - 0.6 * math.exp(-0.3 * 1)
    qkv = _inproj(x_cat, norm_mix[1], kinds(1, 1, True), kinds(1, 0), _diff_w_ext(diff_w_in[0]), t, BF16,
                  rope_tabs=_rope_tables(t, n_ctx))
    att = _diff_attn(qkv, diff_lambda[0], diff_subln[0], t, lam_init)
    x_lat = _outproj(att, diff_w_out[0].astype(BF16), x_cat, kinds(1, 2)[:, 1:2])
    hb, aff = _moe_pre(x_lat, norm_ffn[1], kinds(1, 4, True), kinds(1, 3), moe_router[1], t)
    (x_lat,) = _moe(aff, hb, x_lat, [(0, t, kinds(1, 5)[:, 1:2])], moe_w_gate, moe_w_up, moe_w_down, 1)
    return _final(x_lat, norm_final)
```

```python
import functools
import math

import jax
import jax.numpy as jnp
import numpy as np
from jax import lax
from jax.experimental import pallas as pl
from jax.experimental.pallas import tpu as pltpu

F32 = jnp.float32
BF16 = jnp.bfloat16
I32 = jnp.int32

NORM_EPS = 1e-6
LANES = 128
HEAD_W = 128
N_HEADS = 8
GLA_CHUNK = 64
GLA_LEVELS = (32, 16, 8, 4, 2, 1)
ROPE_BASE = 10000.0
GRID_W = 64
TOKEN_BLOCK = 256
VMEM_LIMIT = 56 * 1024 * 1024

_NT = (((1,), (1,)), ((), ()))
_TN = (((0,), (0,)), ((), ()))


def _pick(n, cands):
    for c in cands:
        if n % c == 0:
            return c
    raise ValueError(f"no tile for {n} in {cands}")


def _params(sem):
    return pltpu.CompilerParams(dimension_semantics=sem, vmem_limit_bytes=VMEM_LIMIT)


def _sigmoid(x):
    return 1.0 / (1.0 + jnp.exp(-x))


def _norm_mod(x, g, scale2, shift2, row0, n_lat):
    ms = jnp.mean(x * x, axis=-1, keepdims=True)
    y = x * lax.rsqrt(ms + NORM_EPS) * g
    rows = row0 + lax.broadcasted_iota(I32, (x.shape[0], 1), 0)
    is_ctx = rows >= n_lat
    sc = jnp.where(is_ctx, scale2[0:1], scale2[1:2])
    sh = jnp.where(is_ctx, shift2[0:1], shift2[1:2])
    return y * sc + sh


def _ada_kernel(c_ref, w_ref, b_ref, o_ref):
    c = c_ref[...]
    s = c * _sigmoid(c)
    o_ref[0] = jnp.dot(s, w_ref[0], precision=lax.Precision.HIGHEST,
                       preferred_element_type=F32) + b_ref[0]


def _ada(cvec, ada_w, ada_b):
    depth, d, n = ada_w.shape
    rows = cvec.shape[0]
    tn = _pick(n, (1024, 512, 256, 128))
    return pl.pallas_call(
        _ada_kernel,
        grid=(depth, n // tn),
        in_specs=[pl.BlockSpec((rows, d), lambda l, j: (0, 0)),
                  pl.BlockSpec((1, d, tn), lambda l, j: (l, 0, j)),
                  pl.BlockSpec((1, 1, tn), lambda l, j: (l, 0, j))],
        out_specs=pl.BlockSpec((1, rows, tn), lambda l, j: (l, 0, j)),
        out_shape=jax.ShapeDtypeStruct((depth, rows, n), F32),
        compiler_params=_params(("arbitrary", "arbitrary")),
        name="adaln",
    )(cvec, ada_w, ada_b.reshape(depth, 1, n))


def _inproj_kernel(x_ref, g_ref, sc_ref, sh_ref, w_ref, *rest, tm, tn, n_lat, rope):
    if rope:
        cos_ref, sin_ref, o_ref = rest
    else:
        (o_ref,) = rest
    i = pl.program_id(1)
    h = _norm_mod(x_ref[0], g_ref[...], sc_ref[0], sh_ref[0], i * tm, n_lat).astype(BF16)
    for n in range(w_ref.shape[1] // tn):
        cols = slice(n * tn, (n + 1) * tn)
        acc = jnp.dot(h, w_ref[:, cols], preferred_element_type=F32)
        if not rope:
            o_ref[0, :, cols] = acc.astype(o_ref.dtype)
        elif n < 2:
            cos = cos_ref[n]
            sin = sin_ref[n]
            lane = lax.broadcasted_iota(I32, (1, HEAD_W), 1)
            first = (lane % 64) < 32
            for hd in range(tn // HEAD_W):
                a = acc[:, hd * HEAD_W:(hd + 1) * HEAD_W]
                rot = jnp.where(first, pltpu.roll(a, HEAD_W - 32, 1), pltpu.roll(a, 32, 1))
                o_ref[0, :, n * tn + hd * HEAD_W:n * tn + (hd + 1) * HEAD_W] = (
                    a * cos + rot * sin).astype(o_ref.dtype)
        else:
            col = lax.broadcasted_iota(I32, (1, tn), 1)
            o_ref[0, :, cols] = jnp.where(col % (2 * HEAD_W) >= HEAD_W, 1.0, acc).astype(o_ref.dtype)


def _inproj(x, g, scale2, shift2, w_bf16, n_lat, out_dtype, rope_tabs=None):
    b, r, d = x.shape
    n = w_bf16.shape[1]
    tm = _pick(r, (384, 256, 128))
    tn = 1024
    rope = rope_tabs is not None
    in_specs = [pl.BlockSpec((1, tm, d), lambda bb, i: (bb, i, 0)),
                pl.BlockSpec((1, d), lambda bb, i: (0, 0)),
                pl.BlockSpec((1, 2, d), lambda bb, i: (bb, 0, 0)),
                pl.BlockSpec((1, 2, d), lambda bb, i: (bb, 0, 0)),
                pl.BlockSpec((d, n), lambda bb, i: (0, 0))]
    args = [x, g.reshape(1, d), scale2, shift2, w_bf16]
    if rope:
        cos_t, sin_t = rope_tabs
        spec = pl.BlockSpec((2, tm, HEAD_W), lambda bb, i: (0, i, 0))
        in_specs += [spec, spec]
        args += [cos_t, sin_t]
    return pl.pallas_call(
        functools.partial(_inproj_kernel, tm=tm, tn=tn, n_lat=n_lat, rope=rope),
        grid=(b, r // tm),
        in_specs=in_specs,
        out_specs=pl.BlockSpec((1, tm, n), lambda bb, i: (bb, i, 0)),
        out_shape=jax.ShapeDtypeStruct((b, r, n), out_dtype),
        compiler_params=_params(("arbitrary", "arbitrary")),
        name="inproj_rope" if rope else "inproj",
    )(*args)


def _gla_consts(reverse):
    c = GLA_CHUNK
    nl = len(GLA_LEVELS)
    t = np.arange(c)
    p = c - 1 - t if reverse else t
    tri = (p[None, :] <= p[:, None]).astype(np.float32)
    rowsel = np.zeros((nl, c, HEAD_W), np.float32)
    masks = np.zeros((nl + 1, c, c), np.float32)
    for li, s in enumerate(GLA_LEVELS):
        blk = p // (2 * s)
        second = (p // s) % 2 == 1
        rowsel[li] = second[:, None]
        masks[li] = (blk[:, None] == blk[None, :]) & second[:, None] & (~second)[None, :]
    masks[nl] = np.eye(c)
    return tri, rowsel, masks


def _level_ref(b, s, reverse):
    c = b.shape[0]
    at = s if reverse else s - 1
    if 2 * s >= 8:
        pieces = [jnp.broadcast_to(b[blk * 2 * s + at:blk * 2 * s + at + 1], (2 * s, HEAD_W))
                  for blk in range(c // (2 * s))]
        return pieces[0] if len(pieces) == 1 else jnp.concatenate(pieces, axis=0)
    sub = lax.broadcasted_iota(I32, (8, HEAD_W), 0)
    outs = []
    for v in range(c // 8):
        acc = None
        for blk in range(8 // (2 * s)):
            row = 8 * v + blk * 2 * s + at
            cand = jnp.broadcast_to(b[row:row + 1], (8, HEAD_W))
            acc = cand if acc is None else jnp.where(sub >= blk * 2 * s, cand, acc)
        outs.append(acc)
    return jnp.concatenate(outs, axis=0)


def _gla_kernel(q_ref, f_ref, v_ref, lb_ref, tri_ref, rowsel_ref, masks_ref, o_ref, st_ref, *, tb, hg, reverse):
    c = GLA_CHUNK
    nl = len(GLA_LEVELS)
    nc = tb // c

    @pl.when(pl.program_id(2) == 0)
    def _():
        st_ref[...] = jnp.zeros_like(st_ref)

    tri = tri_ref[...]
    last = 0 if reverse else c - 1
    starts = [(nc - 1 - ci if reverse else ci) * c for ci in range(nc)]
    units = [(hd, r0) for hd in range(hg) for r0 in starts]

    qf, kk, vb, b3 = [], [], [], []
    for hd, r0 in units:
        lanes = slice(hd * HEAD_W, (hd + 1) * HEAD_W)
        lb = lb_ref[0, :, lanes]
        q = q_ref[0, r0:r0 + c, lanes]
        qf.append(q * _sigmoid(q))
        f = lb + (1.0 - lb) * _sigmoid(f_ref[0, r0:r0 + c, lanes])
        lf = jnp.log(f)
        kk.append(1.0 - f)
        vb.append(v_ref[0, r0:r0 + c, lanes].astype(BF16))
        hi = lf.astype(BF16)
        r1 = lf - hi.astype(F32)
        mid = r1.astype(BF16)
        lo = (r1 - mid.astype(F32)).astype(BF16)
        b3.append(jnp.dot(tri, jnp.concatenate([hi, mid, lo], axis=1), preferred_element_type=F32))
    bc = [x[:, :HEAD_W] + x[:, HEAD_W:2 * HEAD_W] + x[:, 2 * HEAD_W:] for x in b3]

    a = []
    for i in range(len(units)):
        ai = masks_ref[nl] * lax.dot_general(qf[i].astype(BF16), kk[i].astype(BF16), _NT,
                                             preferred_element_type=F32)
        for li, s in enumerate(GLA_LEVELS):
            e = jnp.exp(-jnp.abs(bc[i] - _level_ref(bc[i], s, reverse)))
            xb = (jnp.where(rowsel_ref[li] > 0, qf[i], kk[i]) * e).astype(BF16)
            ai = ai + masks_ref[li] * lax.dot_general(xb, xb, _NT, preferred_element_type=F32)
        a.append(ai.astype(BF16))

    o_intra = [jnp.dot(a[i], vb[i], preferred_element_type=F32) for i in range(len(units))]
    upd = []
    for i in range(len(units)):
        kdec = (kk[i] * jnp.exp(bc[i][last:last + 1] - bc[i])).astype(BF16)
        upd.append(lax.dot_general(vb[i], kdec, _TN, preferred_element_type=F32))

    st = [st_ref[hd] for hd in range(hg)]
    for i, (hd, r0) in enumerate(units):
        qd = (qf[i] * jnp.exp(bc[i])).astype(BF16)
        o_ref[0, r0:r0 + c, hd * HEAD_W:(hd + 1) * HEAD_W] = o_intra[i] + lax.dot_general(
            qd, st[hd].astype(BF16), _NT, preferred_element_type=F32)
        st[hd] = st[hd] * jnp.exp(bc[i][last:last + 1]) + upd[i]
    for hd in range(hg):
        st_ref[hd] = st[hd]


def _gla(proj, lb, n_lat, reverse):
    b, r, _ = proj.shape
    tb = TOKEN_BLOCK
    assert n_lat % tb == 0 and r % tb == 0
    nblk = r // tb
    nlb = n_lat // tb
    ncb = nblk - nlb
    tri, rowsel, masks = _gla_consts(reverse)
    nl = len(GLA_LEVELS)
    c = GLA_CHUNK
    hg = 8
    wblk = hg * HEAD_W
    fcol = N_HEADS // hg * (2 if reverse else 1)

    def blk(n):
        if reverse:
            return jnp.where(n < ncb, nblk - 1 - n, nlb - 1 - (n - ncb))
        return jnp.where(n < ncb, nlb + n, n - ncb)

    return pl.pallas_call(
        functools.partial(_gla_kernel, tb=tb, hg=hg, reverse=reverse),
        grid=(b, N_HEADS // hg, nblk),
        in_specs=[pl.BlockSpec((1, tb, wblk), lambda bb, h, n: (bb, blk(n), h)),
                  pl.BlockSpec((1, tb, wblk), lambda bb, h, n: (bb, blk(n), fcol + h)),
                  pl.BlockSpec((1, tb, wblk), lambda bb, h, n: (bb, blk(n), 3 * N_HEADS // hg + h)),
                  pl.BlockSpec((1, 1, wblk), lambda bb, h, n: (h, 0, 0)),
                  pl.BlockSpec((c, c), lambda bb, h, n: (0, 0)),
                  pl.BlockSpec((nl, c, HEAD_W), lambda bb, h, n: (0, 0, 0)),
                  pl.BlockSpec((nl + 1, c, c), lambda bb, h, n: (0, 0, 0))],
        out_specs=pl.BlockSpec((1, tb, wblk), lambda bb, h, n: (bb, blk(n), h)),
        out_shape=jax.ShapeDtypeStruct((b, r, N_HEADS * HEAD_W), F32),
        scratch_shapes=[pltpu.VMEM((hg, HEAD_W, HEAD_W), F32)],
        compiler_params=_params(("arbitrary", "arbitrary", "arbitrary")),
        name="gla_bwd" if reverse else "gla_fwd",
    )(proj, proj, proj, lb.reshape(N_HEADS // hg, 1, wblk), jnp.asarray(tri, BF16),
      jnp.asarray(rowsel), jnp.asarray(masks))


def _hgrn_out_kernel(of_ref, ob_ref, g_ref, ng_ref, w_ref, x_ref, gate_ref, o_ref, y_ref, *, tm, n_lat):
    i = pl.program_id(1)
    o = of_ref[0] + ob_ref[0]
    for h in range(N_HEADS):
        sl = slice(h * HEAD_W, (h + 1) * HEAD_W)
        oh = o[:, sl]
        ms = jnp.mean(oh * oh, axis=-1, keepdims=True)
        g = g_ref[0, :, sl]
        y_ref[:, sl] = (oh * lax.rsqrt(ms + NORM_EPS) * ng_ref[:, sl] * (g * _sigmoid(g))).astype(BF16)
    y = jnp.dot(y_ref[...], w_ref[...], preferred_element_type=F32)
    rows = i * tm + lax.broadcasted_iota(I32, (tm, 1), 0)
    gate = jnp.where(rows >= n_lat, gate_ref[0, 0:1], gate_ref[0, 1:2])
    o_ref[0] = x_ref[0] + gate * y


def _hgrn_out(o_f, o_b, proj, norm_g, w_bf16, x, gate2, n_lat):
    b, r, d = x.shape
    tm = _pick(r, (384, 256, 128))
    return pl.pallas_call(
        functools.partial(_hgrn_out_kernel, tm=tm, n_lat=n_lat),
        grid=(b, r // tm),
        in_specs=[pl.BlockSpec((1, tm, d), lambda bb, i: (bb, i, 0)),
                  pl.BlockSpec((1, tm, d), lambda bb, i: (bb, i, 0)),
                  pl.BlockSpec((1, tm, d), lambda bb, i: (bb, i, 4)),
                  pl.BlockSpec((1, d), lambda bb, i: (0, 0)),
                  pl.BlockSpec((d, d), lambda bb, i: (0, 0)),
                  pl.BlockSpec((1, tm, d), lambda bb, i: (bb, i, 0)),
                  pl.BlockSpec((1, 2, d), lambda bb, i: (bb, 0, 0))],
        out_specs=pl.BlockSpec((1, tm, d), lambda bb, i: (bb, i, 0)),
        out_shape=jax.ShapeDtypeStruct((b, r, d), F32),
        scratch_shapes=[pltpu.VMEM((tm, d), BF16)],
        compiler_params=_params(("arbitrary", "arbitrary")),
        name="hgrn_out",
    )(o_f, o_b, proj, jnp.tile(norm_g, N_HEADS).reshape(1, d), w_bf16, x, gate2)


def _attn_kernel(q_ref, k_ref, v_ref, lam_ref, g_ref, o_ref, *, tq, tk, lam_init):
    nk = k_ref.shape[1] // tk
    q = q_ref[0]
    lane = lax.broadcasted_iota(I32, (1, HEAD_W), 1)
    zero = jnp.zeros_like(q)
    qc = [jnp.where(lane < 64, q, zero), jnp.where(lane >= 64, q, zero)]

    def scores(u):
        comp, ki = u
        return lax.dot_general(qc[comp], k_ref[0, ki * tk:(ki + 1) * tk, :], _NT, preferred_element_type=F32)

    units = [(comp, ki) for ki in range(nk) for comp in range(2)]
    m = [jnp.full((tq, LANES), -jnp.inf, F32)] * 2
    acc = [jnp.zeros((tq, 2 * HEAD_W), F32)] * 2
    s_next = scores(units[0])
    for idx, (comp, ki) in enumerate(units):
        s = s_next
        if idx + 1 < len(units):
            s_next = scores(units[idx + 1])
        m_new = jnp.maximum(m[comp], jnp.max(s, axis=-1, keepdims=True))
        p = jnp.exp2(s - jnp.tile(m_new, (1, tk // LANES)))
        alpha = jnp.exp2(m[comp] - m_new)
        v1 = v_ref[0, ki * tk:(ki + 1) * tk, :]
        acc[comp] = jnp.tile(alpha, (1, 2)) * acc[comp] + jnp.dot(p.astype(BF16), v1, preferred_element_type=F32)
        m[comp] = m_new
    out = [a[:, :HEAD_W] / a[:, HEAD_W:] for a in acc]

    lv = lam_ref[...]
    s01 = jnp.sum(lv[0:1] * lv[1:2], axis=-1, keepdims=True)
    s23 = jnp.sum(lv[2:3] * lv[3:4], axis=-1, keepdims=True)
    lam = jnp.exp(s01) - jnp.exp(s23) + lam_init
    o = out[0] - lam * out[1]
    ms = jnp.mean(o * o, axis=-1, keepdims=True)
    o_ref[0] = (o * lax.rsqrt(ms + NORM_EPS) * g_ref[...] * (1.0 - lam_init)).astype(o_ref.dtype)


def _diff_attn(qkv, lam_vecs, subln_g, t, lam_init):
    b, r, _ = qkv.shape
    tq = _pick(t, (512, 256, 128))
    tk = _pick(r, (2816, 1280, 512, 256))
    return pl.pallas_call(
        functools.partial(_attn_kernel, tq=tq, tk=tk, lam_init=lam_init),
        grid=(b, N_HEADS, t // tq),
        in_specs=[pl.BlockSpec((1, tq, HEAD_W), lambda bb, h, i: (bb, i, h)),
                  pl.BlockSpec((1, r, HEAD_W), lambda bb, h, i: (bb, 0, N_HEADS + h)),
                  pl.BlockSpec((1, r, 2 * HEAD_W), lambda bb, h, i: (bb, 0, N_HEADS + h)),
                  pl.BlockSpec(lam_vecs.shape, lambda bb, h, i: (0, 0)),
                  pl.BlockSpec((1, HEAD_W), lambda bb, h, i: (0, 0))],
        out_specs=pl.BlockSpec((1, tq, HEAD_W), lambda bb, h, i: (bb, i, h)),
        out_shape=jax.ShapeDtypeStruct((b, t, N_HEADS * HEAD_W), BF16),
        compiler_params=_params(("arbitrary", "arbitrary", "arbitrary")),
        name="diff_attn",
    )(qkv, qkv, qkv, lam_vecs.astype(F32), subln_g.reshape(1, HEAD_W).astype(F32))


def _outproj_kernel(a_ref, w_ref, x_ref, gate_ref, o_ref):
    y = jnp.dot(a_ref[0], w_ref[...], preferred_element_type=F32)
    o_ref[0] = x_ref[0] + gate_ref[0] * y


def _outproj(a, w_bf16, x_cat, gate):
    b, t, d = a.shape
    tm = _pick(t, (512, 256, 128))
    return pl.pallas_call(
        _outproj_kernel,
        grid=(b, t // tm),
        in_specs=[pl.BlockSpec((1, tm, d), lambda bb, i: (bb, i, 0)),
                  pl.BlockSpec((d, d), lambda bb, i: (0, 0)),
                  pl.BlockSpec((1, tm, d), lambda bb, i: (bb, i, 0)),
                  pl.BlockSpec((1, 1, d), lambda bb, i: (bb, 0, 0))],
        out_specs=pl.BlockSpec((1, tm, d), lambda bb, i: (bb, i, 0)),
        out_shape=jax.ShapeDtypeStruct((b, t, d), F32),
        compiler_params=_params(("arbitrary", "arbitrary")),
        name="outproj",
    )(a, w_bf16, x_cat, gate)


def _moe_pre_kernel(x_ref, g_ref, sc_ref, sh_ref, wr_ref, h_ref, aff_ref, *, tm, n_lat):
    i = pl.program_id(1)
    h = _norm_mod(x_ref[0], g_ref[...], sc_ref[0], sh_ref[0], i * tm, n_lat)
    h_ref[0] = h.astype(BF16)
    logits = lax.dot_general(wr_ref[...], h, _NT, precision=lax.Precision.HIGHEST,
                             preferred_element_type=F32)
    mx = jnp.max(logits, axis=0, keepdims=True)
    ex = jnp.exp(logits - mx)
    aff_ref[0] = ex / jnp.sum(ex, axis=0, keepdims=True)


def _moe_pre(x, g, scale2, shift2, w_router, n_lat):
    b, r, d = x.shape
    ne = w_router.shape[1]
    tm = _pick(r, (768, 512, 384, 256, 128))
    return pl.pallas_call(
        functools.partial(_moe_pre_kernel, tm=tm, n_lat=n_lat),
        grid=(b, r // tm),
        in_specs=[pl.BlockSpec((1, tm, d), lambda bb, i: (bb, i, 0)),
                  pl.BlockSpec((1, d), lambda bb, i: (0, 0)),
                  pl.BlockSpec((1, 2, d), lambda bb, i: (bb, 0, 0)),
                  pl.BlockSpec((1, 2, d), lambda bb, i: (bb, 0, 0)),
                  pl.BlockSpec((ne, d), lambda bb, i: (0, 0))],
        out_specs=[pl.BlockSpec((1, tm, d), lambda bb, i: (bb, i, 0)),
                   pl.BlockSpec((1, ne, tm), lambda bb, i: (bb, 0, i))],
        out_shape=[jax.ShapeDtypeStruct((b, r, d), BF16),
                   jax.ShapeDtypeStruct((b, ne, r), F32)],
        compiler_params=_params(("arbitrary", "arbitrary")),
        name="moe_pre",
    )(x, g.reshape(1, d), scale2, shift2, w_router.T)


def _topk_kernel(aff_ref, tri_ref, pos_ref, gsel_ref, r0_ref, *, cap):
    ne, t = aff_ref.shape[1], aff_ref.shape[2]
    tbk = TOKEN_BLOCK
    nblk = t // tbk
    bits = pltpu.bitcast(aff_ref[0], I32)
    thr = jnp.zeros((ne, 1), I32)
    for bit in range(30, -1, -1):
        cand = thr | (1 << bit)
        cnt = jnp.sum(jnp.where(bits >= cand, 1.0, 0.0), axis=1, keepdims=True)
        thr = jnp.where(cnt >= cap, cand, thr)
    n_gt = jnp.sum(jnp.where(bits > thr, 1.0, 0.0), axis=1, keepdims=True)
    need = cap - n_gt
    tri = tri_ref[...]
    carry_eq = jnp.zeros((ne, 1), F32)
    carry_sel = jnp.zeros((ne, 1), F32)
    r0_ref[0] = jnp.full((ne, LANES), cap, I32)
    for j in range(nblk):
        sl = slice(j * tbk, (j + 1) * tbk)
        a = aff_ref[0, :, sl]
        bj = pltpu.bitcast(a, I32)
        eq = jnp.where(bj == thr, 1.0, 0.0)
        gt = jnp.where(bj > thr, 1.0, 0.0)
        pe = jnp.dot(eq.astype(BF16), tri, preferred_element_type=F32) + carry_eq
        sel = gt + eq * jnp.where(pe - eq < need, 1.0, 0.0)
        ps = jnp.dot(sel.astype(BF16), tri, preferred_element_type=F32) + carry_sel
        pos_ref[0, :, sl] = jnp.where(sel > 0, ps - 1.0, -1.0).astype(I32)
        gsel_ref[0, :, sl] = a * sel
        r0_ref[0, :, j:j + 1] = carry_sel.astype(I32)
        carry_eq = pe[:, tbk - 1:tbk]
        carry_sel = ps[:, tbk - 1:tbk]


def _topk(aff, cap, t, tok_off):
    b, ne, _ = aff.shape
    tbk = TOKEN_BLOCK
    assert t % tbk == 0 and t // tbk < LANES and tok_off % t == 0
    tri = jnp.asarray(np.triu(np.ones((tbk, tbk), np.float32)), BF16)
    return pl.pallas_call(
        functools.partial(_topk_kernel, cap=cap),
        grid=(b,),
        in_specs=[pl.BlockSpec((1, ne, t), lambda bb: (bb, 0, tok_off // t)),
                  pl.BlockSpec((tbk, tbk), lambda bb: (0, 0))],
        out_specs=[pl.BlockSpec((1, ne, t), lambda bb: (bb, 0, 0)),
                   pl.BlockSpec((1, ne, t), lambda bb: (bb, 0, 0)),
                   pl.BlockSpec((1, ne, LANES), lambda bb: (bb, 0, 0))],
        out_shape=[jax.ShapeDtypeStruct((b, ne, t), I32),
                   jax.ShapeDtypeStruct((b, ne, t), F32),
                   jax.ShapeDtypeStruct((b, ne, LANES), I32)],
        compiler_params=_params(("arbitrary",)),
        name="topk",
    )(aff, tri)


def _windows(r0_ref, row, j, w, cap, align):
    lo = r0_ref[row, j]
    hi = r0_ref[row, j + 1]
    off = pl.multiple_of(jnp.minimum((lo // align) * align, cap - w), align)
    return off, (jnp.maximum(hi - (off + w), 0) + w - 1) // w


def _gather_kernel(r0_ref, pos_ref, g_ref, h_ref, xs_ref, gs_ref, acc_ref, gacc_ref, *, w, nblk, ne, cap):
    b, e = pl.program_id(0), pl.program_id(1)
    row = b * ne + e
    tbk = TOKEN_BLOCK
    acc_ref[...] = jnp.zeros_like(acc_ref)
    gacc_ref[...] = jnp.zeros_like(gacc_ref)
    sub = lax.broadcasted_iota(I32, (w, tbk), 0)

    def window(j, off, first_rank):
        rank = off + sub
        if first_rank is not None:
            rank = jnp.where(rank >= first_rank, rank, -2)
        hit = pos_ref[0, 0, j:j + 1, :] == rank
        oh = jnp.where(hit, 1.0, 0.0).astype(BF16)
        acc_ref[pl.ds(off, w), :] += jnp.dot(oh, h_ref[0, j * tbk:(j + 1) * tbk, :], preferred_element_type=F32)
        gsum = jnp.sum(jnp.where(hit, g_ref[0, 0, j:j + 1, :], 0.0), axis=1, keepdims=True)
        gacc_ref[pl.ds(off, w), :] += jnp.broadcast_to(gsum, (w, LANES))

    for j in range(nblk):
        off, n_more = _windows(r0_ref, row, j, w, cap, 8)
        window(j, off, None)

        def more(i, _, j=j, off=off):
            start = off + i * w
            window(j, pl.multiple_of(jnp.minimum(start, cap - w), 8), start)
            return 0

        lax.fori_loop(1, n_more + 1, more, 0)
    xs_ref[0, 0] = acc_ref[...].astype(BF16)
    gs_ref[0, 0] = gacc_ref[...]


def _gather(r0, pos, gate, hb, cap, tok_off):
    b, ne, t = pos.shape
    d = hb.shape[2]
    assert tok_off % t == 0
    tbk = TOKEN_BLOCK
    nblk = t // tbk
    w = min(cap, 64)
    pos4 = pos.reshape(b, ne, nblk, tbk)
    g4 = gate.reshape(b, ne, nblk, tbk)
    grid_spec = pltpu.PrefetchScalarGridSpec(
        num_scalar_prefetch=1,
        grid=(b, ne),
        in_specs=[pl.BlockSpec((1, 1, nblk, tbk), lambda bb, e, r: (bb, e, 0, 0)),
                  pl.BlockSpec((1, 1, nblk, tbk), lambda bb, e, r: (bb, e, 0, 0)),
                  pl.BlockSpec((1, t, d), lambda bb, e, r: (bb, tok_off // t, 0))],
        out_specs=[pl.BlockSpec((1, 1, cap, d), lambda bb, e, r: (e, bb, 0, 0)),
                   pl.BlockSpec((1, 1, cap, LANES), lambda bb, e, r: (e, bb, 0, 0))],
        scratch_shapes=[pltpu.VMEM((cap, d), F32), pltpu.VMEM((cap, LANES), F32)],
    )
    return pl.pallas_call(
        functools.partial(_gather_kernel, w=w, nblk=nblk, ne=ne, cap=cap),
        grid_spec=grid_spec,
        out_shape=[jax.ShapeDtypeStruct((ne, b, cap, d), BF16),
                   jax.ShapeDtypeStruct((ne, b, cap, LANES), F32)],
        compiler_params=_params(("arbitrary", "arbitrary")),
        name="moe_gather",
    )(r0.reshape(b * ne, LANES), pos4, g4, hb)


def _ffn_kernel(*refs, nb, caps, rcs):
    ng = len(caps)
    ins, (wg_ref, wu_ref, wd_ref) = refs[:3 * ng], refs[3 * ng:3 * ng + 3]
    y_refs, acc_refs = refs[3 * ng + 3:4 * ng + 3], refs[4 * ng + 3:]
    f = pl.program_id(1)
    nf = pl.num_programs(1)

    @pl.when(f == 0)
    def _():
        for acc_ref in acc_refs:
            acc_ref[...] = jnp.zeros_like(acc_ref)

    wg = wg_ref[0, 0].astype(BF16)
    wu = wu_ref[0, 0].astype(BF16)
    wd = wd_ref[0, 0].astype(BF16)
    chunks = [(g, c0, rcs[g]) for g in range(ng) for c0 in range(0, nb * caps[g], rcs[g])]

    def gate_up(chunk):
        g, c0, rc = chunk
        x = ins[3 * g][0, c0:c0 + rc, :]
        return jnp.dot(x, wg, preferred_element_type=F32), jnp.dot(x, wu, preferred_element_type=F32)

    nxt = gate_up(chunks[0])
    for i, (g, c0, rc) in enumerate(chunks):
        a, u = nxt
        if i + 1 < len(chunks):
            nxt = gate_up(chunks[i + 1])
        hm = (a * _sigmoid(a) * u).astype(BF16)
        acc_refs[g][c0:c0 + rc, :] += jnp.dot(hm, wd, preferred_element_type=F32)

    @pl.when(f == nf - 1)
    def _():
        for g in range(ng):
            gs_ref, gt_ref = ins[3 * g + 1], ins[3 * g + 2]
            for bb in range(nb):
                sl = slice(bb * caps[g], (bb + 1) * caps[g])
                y_refs[g][0, sl, :] = (acc_refs[g][sl, :] * gs_ref[0, sl, 0:1] * gt_ref[bb]).astype(BF16)


def _ffn(groups, w_gate, w_up, w_down, layer):
    ne, b, _, d = groups[0][0].shape
    fdim = w_gate.shape[3]
    tf = _pick(fdim, (256, 128))
    caps = [xs.shape[2] for xs, _, _ in groups]
    rcs = [_pick(b * cap, (1024, 512, 256, 128, 64)) for cap in caps]
    in_specs, args = [], []
    for (xs, gs, gate_f), cap in zip(groups, caps):
        rows = b * cap
        in_specs += [pl.BlockSpec((1, rows, d), lambda e, f: (e, 0, 0)),
                     pl.BlockSpec((1, rows, LANES), lambda e, f: (e, 0, 0)),
                     pl.BlockSpec((b, 1, d), lambda e, f: (0, 0, 0))]
        args += [xs.reshape(ne, rows, d), gs.reshape(ne, rows, LANES), gate_f]
    in_specs += [pl.BlockSpec((1, 1, d, tf), lambda e, f: (layer, e, 0, f)),
                 pl.BlockSpec((1, 1, d, tf), lambda e, f: (layer, e, 0, f)),
                 pl.BlockSpec((1, 1, tf, d), lambda e, f: (layer, e, f, 0))]
    ys = pl.pallas_call(
        functools.partial(_ffn_kernel, nb=b, caps=caps, rcs=rcs),
        grid=(ne, fdim // tf),
        in_specs=in_specs,
        out_specs=[pl.BlockSpec((1, b * cap, d), lambda e, f: (e, 0, 0)) for cap in caps],
        out_shape=[jax.ShapeDtypeStruct((ne, b * cap, d), BF16) for cap in caps],
        scratch_shapes=[pltpu.VMEM((b * cap, d), F32) for cap in caps],
        compiler_params=_params(("arbitrary", "arbitrary")),
        name="moe_ffn",
    )(*args, w_gate, w_up, w_down)
    return [y.reshape(ne, b, cap, d) for y, cap in zip(ys, caps)]


def _combine_kernel(r0_ref, pos_ref, y_ref, x_ref, o_ref, acc_ref, *, w, ne, cap):
    b, j = pl.program_id(0), pl.program_id(2)
    tbk = TOKEN_BLOCK
    sub = lax.broadcasted_iota(I32, (w, tbk), 0)
    align = 16

    def window(e, off, first_rank):
        rank = off + sub
        if first_rank is not None:
            rank = jnp.where(rank >= first_rank, rank, -2)
        oh = jnp.where(pos_ref[0, 0, e:e + 1, :] == rank, 1.0, 0.0).astype(BF16)
        return lax.dot_general(oh, y_ref[e, 0, pl.ds(off, w), :], _TN, preferred_element_type=F32)

    acc = x_ref[0]
    more = []
    for e in range(ne):
        off, n_more = _windows(r0_ref, b * ne + e, j, w, cap, align)
        acc = acc + window(e, off, None)
        more.append((off, n_more))
    acc_ref[...] = acc
    for e, (off, n_more) in enumerate(more):
        def extra(i, _, e=e, off=off):
            start = off + i * w
            acc_ref[...] += window(e, pl.multiple_of(jnp.minimum(start, cap - w), align), start)
            return 0

        lax.fori_loop(1, n_more + 1, extra, 0)
    o_ref[0] = acc_ref[...]


def _combine(r0, pos, y, x, row_off):
    ne, b, cap, d = y.shape
    t = pos.shape[2]
    tbk = TOKEN_BLOCK
    nblk = t // tbk
    assert row_off % tbk == 0
    boff = row_off // tbk
    w = min(cap, 256)
    dh = d // 2
    pos4 = pos.reshape(b, ne, nblk, tbk).transpose(0, 2, 1, 3)
    grid_spec = pltpu.PrefetchScalarGridSpec(
        num_scalar_prefetch=1,
        grid=(b, 2, nblk),
        in_specs=[pl.BlockSpec((1, 1, ne, tbk), lambda bb, c, j, r: (bb, j, 0, 0)),
                  pl.BlockSpec((ne, 1, cap, dh), lambda bb, c, j, r: (0, bb, 0, c)),
                  pl.BlockSpec((1, tbk, dh), lambda bb, c, j, r: (bb, j + boff, c))],
        out_specs=pl.BlockSpec((1, tbk, dh), lambda bb, c, j, r: (bb, j, c)),
        scratch_shapes=[pltpu.VMEM((tbk, dh), F32)],
    )
    return pl.pallas_call(
        functools.partial(_combine_kernel, w=w, ne=ne, cap=cap),
        grid_spec=grid_spec,
        out_shape=jax.ShapeDtypeStruct((b, t, d), F32),
        compiler_params=_params(("arbitrary", "arbitrary", "arbitrary")),
        name="moe_combine",
    )(r0.reshape(b * ne, LANES), pos4, y, x)


def _final_kernel(x_ref, g_ref, o_ref):
    x = x_ref[0]
    ms = jnp.mean(x * x, axis=-1, keepdims=True)
    o_ref[0] = x * lax.rsqrt(ms + NORM_EPS) * g_ref[...]


def _final(x, g):
    b, t, d = x.shape
    tm = _pick(t, (512, 256, 128))
    return pl.pallas_call(
        _final_kernel,
        grid=(b, t // tm),
        in_specs=[pl.BlockSpec((1, tm, d), lambda bb, i: (bb, i, 0)),
                  pl.BlockSpec((1, d), lambda bb, i: (0, 0))],
        out_specs=pl.BlockSpec((1, tm, d), lambda bb, i: (bb, i, 0)),
        out_shape=jax.ShapeDtypeStruct((b, t, d), F32),
        compiler_params=_params(("arbitrary", "arbitrary")),
        name="final_norm",
    )(x, g.reshape(1, d))


def _moe(aff, hb, x, sets, w_gate, w_up, w_down, layer):
    ne = aff.shape[1]
    routed, groups = [], []
    for off, t, gate_f in sets:
        cap = 2 * t // ne
        pos, gsel, r0 = _topk(aff, cap, t, off)
        xs, gs = _gather(r0, pos, gsel, hb, cap, off)
        routed.append((r0, pos, off))
        groups.append((xs, gs, gate_f))
    ys = _ffn(groups, w_gate, w_up, w_down, layer)
    return [_combine(r0, pos, y, x, off) for (r0, pos, off), y in zip(routed, ys)]


def _rope_tables(t, n_ctx):
    rows = t // GRID_W
    row = jnp.repeat(jnp.arange(rows, dtype=F32), GRID_W)
    col = jnp.tile(jnp.arange(GRID_W, dtype=F32), rows)
    pairs = HEAD_W // 8
    freq = ROPE_BASE ** (-jnp.arange(pairs, dtype=F32) / pairs)
    ang = jnp.concatenate([row[:, None] * freq, col[:, None] * freq], axis=-1)
    cos, sin = jnp.cos(ang), jnp.sin(ang)
    cos = jnp.tile(cos, (1, 4))
    sin = jnp.tile(jnp.concatenate([-sin, sin], axis=-1), (1, 2))
    cos = jnp.concatenate([cos, jnp.ones((n_ctx, HEAD_W), F32)], axis=0)
    sin = jnp.concatenate([sin, jnp.zeros((n_ctx, HEAD_W), F32)], axis=0)
    qs = (HEAD_W // 2) ** -0.5 * math.log2(math.e)
    return jnp.stack([cos * qs, cos]), jnp.stack([sin * qs, sin])


def _diff_w_ext(w):
    d = w.shape[0]
    wv = w[:, 2 * d:].reshape(d, N_HEADS, HEAD_W)
    wv = jnp.concatenate([wv, jnp.zeros_like(wv)], axis=-1).reshape(d, 2 * d)
    return jnp.concatenate([w[:, :2 * d], wv], axis=1).astype(BF16)


def kernel(x, c, ctx, c_ctx, ada_w, ada_b, norm_mix, norm_ffn, norm_final, hgrn_w_in, hgrn_lb_logits, hgrn_norm, hgrn_w_out, diff_w_in, diff_lambda, diff_subln, diff_w_out, moe_router, moe_w_gate, moe_w_up, moe_w_down):
    b, t, d = x.shape
    n_ctx = ctx.shape[1]
    depth = ada_w.shape[0]
    ne = moe_router.shape[2]
    assert depth == 2 and d == N_HEADS * HEAD_W

    cvec = jnp.concatenate([c, c_ctx[None, :], jnp.zeros((8 - b - 1, d), F32)], axis=0)
    mod = _ada(cvec, ada_w, ada_b)
    lower_bounds = jnp.cumsum(jax.nn.softmax(hgrn_lb_logits.astype(F32), axis=0), axis=0)

    def kinds(layer, k, plus_one=False):
        m = mod[layer, :, k * d:(k + 1) * d]
        v = jnp.stack([jnp.broadcast_to(m[b], (b, d)), m[:b]], axis=1)
        return 1.0 + v if plus_one else v

    x_cat = jnp.concatenate([x, ctx], axis=1)

    proj = _inproj(x_cat, norm_mix[0], kinds(0, 1, True), kinds(0, 0), hgrn_w_in[0].astype(BF16), t, F32)
    o_f = _gla(proj, lower_bounds[0], t, False)
    o_b = _gla(proj, lower_bounds[0], t, True)
    x_cat = _hgrn_out(o_f, o_b, proj, hgrn_norm[0], hgrn_w_out[0].astype(BF16), x_cat, kinds(0, 2), t)
    hb, aff = _moe_pre(x_cat, norm_ffn[0], kinds(0, 4, True), kinds(0, 3), moe_router[0], t)
    gate_f = kinds(0, 5)
    x_lat, x_ctx = _moe(aff, hb, x_cat, [(0, t, gate_f[:, 1:2]), (t, n_ctx, gate_f[:, 0:1])],
                        moe_w_gate, moe_w_up, moe_w_down, 0)
    x_cat = jnp.concatenate([x_lat, x_ctx], axis=1)

    lam_init = 0.8 - 0.6 * math.exp(-0.3 * 1)
    qkv = _inproj(x_cat, norm_mix[1], kinds(1, 1, True), kinds(1, 0), _diff_w_ext(diff_w_in[0]), t, BF16,
                  rope_tabs=_rope_tables(t, n_ctx))
    att = _diff_attn(qkv, diff_lambda[0], diff_subln[0], t, lam_init)
    x_lat = _outproj(att, diff_w_out[0].astype(BF16), x_cat, kinds(1, 2)[:, 1:2])
    hb, aff = _moe_pre(x_lat, norm_ffn[1], kinds(1, 4, True), kinds(1, 3), moe_router[1], t)
    (x_lat,) = _moe(aff, hb, x_lat, [(0, t, kinds(1, 5)[:, 1:2])], moe_w_gate, moe_w_up, moe_w_down, 1)
    return _final(x_lat, norm_final)
```

```python
import functools
import math

import jax
import jax.numpy as jnp
import numpy as np
from jax import lax
from jax.experimental import pallas as pl
from jax.experimental.pallas import tpu as pltpu

F32 = jnp.float32
BF16 = jnp.bfloat16
I32 = jnp.int32

NORM_EPS = 1e-6
LANES = 128
HEAD_W = 128
N_HEADS = 8
GLA_CHUNK = 64
GLA_LEVELS = (32, 16, 8, 4, 2, 1)
ROPE_BASE = 10000.0
GRID_W = 64
TOKEN_BLOCK = 256
VMEM_LIMIT = 56 * 1024 * 1024

_NT = (((1,), (1,)), ((), ()))
_TN = (((0,), (0,)), ((), ()))


def _pick(n, cands):
    for c in cands:
        if n % c == 0:
            return c
    raise ValueError(f"no tile for {n} in {cands}")


def _params(sem):
    return pltpu.CompilerParams(dimension_semantics=sem, vmem_limit_bytes=VMEM_LIMIT)


def _sigmoid(x):
    return 1.0 / (1.0 + jnp.exp(-x))


def _norm_mod(x, g, scale2, shift2, row0, n_lat):
    ms = jnp.mean(x * x, axis=-1, keepdims=True)
    y = x * lax.rsqrt(ms + NORM_EPS) * g
    rows = row0 + lax.broadcasted_iota(I32, (x.shape[0], 1), 0)
    is_ctx = rows >= n_lat
    sc = jnp.where(is_ctx, scale2[0:1], scale2[1:2])
    sh = jnp.where(is_ctx, shift2[0:1], shift2[1:2])
    return y * sc + sh


def _ada_kernel(c_ref, w_ref, b_ref, o_ref):
    c = c_ref[...]
    s = c * _sigmoid(c)
    o_ref[0] = jnp.dot(s, w_ref[0], precision=lax.Precision.HIGHEST,
                       preferred_element_type=F32) + b_ref[0]


def _ada(cvec, ada_w, ada_b):
    depth, d, n = ada_w.shape
    rows = cvec.shape[0]
    tn = _pick(n, (1024, 512, 256, 128))
    return pl.pallas_call(
        _ada_kernel,
        grid=(depth, n // tn),
        in_specs=[pl.BlockSpec((rows, d), lambda l, j: (0, 0)),
                  pl.BlockSpec((1, d, tn), lambda l, j: (l, 0, j)),
                  pl.BlockSpec((1, 1, tn), lambda l, j: (l, 0, j))],
        out_specs=pl.BlockSpec((1, rows, tn), lambda l, j: (l, 0, j)),
        out_shape=jax.ShapeDtypeStruct((depth, rows, n), F32),
        compiler_params=_params(("arbitrary", "arbitrary")),
        name="adaln",
    )(cvec, ada_w, ada_b.reshape(depth, 1, n))


def _inproj_kernel(x_ref, g_ref, sc_ref, sh_ref, w_ref, *rest, tm, tn, n_lat, rope):
    if rope:
        cos_ref, sin_ref, o_ref = rest
    else:
        (o_ref,) = rest
    i = pl.program_id(1)
    h = _norm_mod(x_ref[0], g_ref[...], sc_ref[0], sh_ref[0], i * tm, n_lat).astype(BF16)
    for n in range(w_ref.shape[1] // tn):
        cols = slice(n * tn, (n + 1) * tn)
        acc = jnp.dot(h, w_ref[:, cols], preferred_element_type=F32)
        if not rope:
            o_ref[0, :, cols] = acc.astype(o_ref.dtype)
        elif n < 2:
            cos = cos_ref[n]
            sin = sin_ref[n]
            lane = lax.broadcasted_iota(I32, (1, HEAD_W), 1)
            first = (lane % 64) < 32
            for hd in range(tn // HEAD_W):
                a = acc[:, hd * HEAD_W:(hd + 1) * HEAD_W]
                rot = jnp.where(first, pltpu.roll(a, HEAD_W - 32, 1), pltpu.roll(a, 32, 1))
                o_ref[0, :, n * tn + hd * HEAD_W:n * tn + (hd + 1) * HEAD_W] = (
                    a * cos + rot * sin).astype(o_ref.dtype)
        else:
            col = lax.broadcasted_iota(I32, (1, tn), 1)
            o_ref[0, :, cols] = jnp.where(col % (2 * HEAD_W) >= HEAD_W, 1.0, acc).astype(o_ref.dtype)


def _inproj(x, g, scale2, shift2, w_bf16, n_lat, out_dtype, rope_tabs=None):
    b, r, d = x.shape
    n = w_bf16.shape[1]
    tm = _pick(r, (384, 256, 128))
    tn = 1024
    rope = rope_tabs is not None
    in_specs = [pl.BlockSpec((1, tm, d), lambda bb, i: (bb, i, 0)),
                pl.BlockSpec((1, d), lambda bb, i: (0, 0)),
                pl.BlockSpec((1, 2, d), lambda bb, i: (bb, 0, 0)),
                pl.BlockSpec((1, 2, d), lambda bb, i: (bb, 0, 0)),
                pl.BlockSpec((d, n), lambda bb, i: (0, 0))]
    args = [x, g.reshape(1, d), scale2, shift2, w_bf16]
    if rope:
        cos_t, sin_t = rope_tabs
        spec = pl.BlockSpec((2, tm, HEAD_W), lambda bb, i: (0, i, 0))
        in_specs += [spec, spec]
        args += [cos_t, sin_t]
    return pl.pallas_call(
        functools.partial(_inproj_kernel, tm=tm, tn=tn, n_lat=n_lat, rope=rope),
        grid=(b, r // tm),
        in_specs=in_specs,
        out_specs=pl.BlockSpec((1, tm, n), lambda bb, i: (bb, i, 0)),
        out_shape=jax.ShapeDtypeStruct((b, r, n), out_dtype),
        compiler_params=_params(("arbitrary", "arbitrary")),
        name="inproj_rope" if rope else "inproj",
    )(*args)


def _gla_consts(reverse):
    c = GLA_CHUNK
    nl = len(GLA_LEVELS)
    t = np.arange(c)
    p = c - 1 - t if reverse else t
    tri = (p[None, :] <= p[:, None]).astype(np.float32)
    rowsel = np.zeros((nl, c, HEAD_W), np.float32)
    masks = np.zeros((nl + 1, c, c), np.float32)
    for li, s in enumerate(GLA_LEVELS):
        blk = p // (2 * s)
        second = (p // s) % 2 == 1
        rowsel[li] = second[:, None]
        masks[li] = (blk[:, None] == blk[None, :]) & second[:, None] & (~second)[None, :]
    masks[nl] = np.eye(c)
    return tri, rowsel, masks


def _level_ref(b, s, reverse):
    c = b.shape[0]
    at = s if reverse else s - 1
    if 2 * s >= 8:
        pieces = [jnp.broadcast_to(b[blk * 2 * s + at:blk * 2 * s + at + 1], (2 * s, HEAD_W))
                  for blk in range(c // (2 * s))]
        return pieces[0] if len(pieces) == 1 else jnp.concatenate(pieces, axis=0)
    sub = lax.broadcasted_iota(I32, (8, HEAD_W), 0)
    outs = []
    for v in range(c // 8):
        acc = None
        for blk in range(8 // (2 * s)):
            row = 8 * v + blk * 2 * s + at
            cand = jnp.broadcast_to(b[row:row + 1], (8, HEAD_W))
            acc = cand if acc is None else jnp.where(sub >= blk * 2 * s, cand, acc)
        outs.append(acc)
    return jnp.concatenate(outs, axis=0)


def _gla_kernel(q_ref, f_ref, v_ref, lb_ref, tri_ref, rowsel_ref, masks_ref, o_ref, st_ref, *, tb, hg, reverse):
    c = GLA_CHUNK
    nl = len(GLA_LEVELS)
    nc = tb // c

    @pl.when(pl.program_id(2) == 0)
    def _():
        st_ref[...] = jnp.zeros_like(st_ref)

    tri = tri_ref[...]
    last = 0 if reverse else c - 1
    starts = [(nc - 1 - ci if reverse else ci) * c for ci in range(nc)]
    units = [(hd, r0) for hd in range(hg) for r0 in starts]

    qf, kk, vb, b3 = [], [], [], []
    for hd, r0 in units:
        lanes = slice(hd * HEAD_W, (hd + 1) * HEAD_W)
        lb = lb_ref[0, :, lanes]
        q = q_ref[0, r0:r0 + c, lanes]
        qf.append(q * _sigmoid(q))
        f = lb + (1.0 - lb) * _sigmoid(f_ref[0, r0:r0 + c, lanes])
        lf = jnp.log(f)
        kk.append(1.0 - f)
        vb.append(v_ref[0, r0:r0 + c, lanes].astype(BF16))
        hi = lf.astype(BF16)
        r1 = lf - hi.astype(F32)
        mid = r1.astype(BF16)
        lo = (r1 - mid.astype(F32)).astype(BF16)
        b3.append(jnp.dot(tri, jnp.concatenate([hi, mid, lo], axis=1), preferred_element_type=F32))
    bc = [x[:, :HEAD_W] + x[:, HEAD_W:2 * HEAD_W] + x[:, 2 * HEAD_W:] for x in b3]

    a = []
    for i in range(len(units)):
        ai = masks_ref[nl] * lax.dot_general(qf[i].astype(BF16), kk[i].astype(BF16), _NT,
                                             preferred_element_type=F32)
        for li, s in enumerate(GLA_LEVELS):
            e = jnp.exp(-jnp.abs(bc[i] - _level_ref(bc[i], s, reverse)))
            xb = (jnp.where(rowsel_ref[li] > 0, qf[i], kk[i]) * e).astype(BF16)
            ai = ai + masks_ref[li] * lax.dot_general(xb, xb, _NT, preferred_element_type=F32)
        a.append(ai.astype(BF16))

    o_intra = [jnp.dot(a[i], vb[i], preferred_element_type=F32) for i in range(len(units))]
    upd = []
    for i in range(len(units)):
        kdec = (kk[i] * jnp.exp(bc[i][last:last + 1] - bc[i])).astype(BF16)
        upd.append(lax.dot_general(vb[i], kdec, _TN, preferred_element_type=F32))

    st = [st_ref[hd] for hd in range(hg)]
    for i, (hd, r0) in enumerate(units):
        qd = (qf[i] * jnp.exp(bc[i])).astype(BF16)
        o_ref[0, r0:r0 + c, hd * HEAD_W:(hd + 1) * HEAD_W] = o_intra[i] + lax.dot_general(
            qd, st[hd].astype(BF16), _NT, preferred_element_type=F32)
        st[hd] = st[hd] * jnp.exp(bc[i][last:last + 1]) + upd[i]
    for hd in range(hg):
        st_ref[hd] = st[hd]


def _gla(proj, lb, n_lat, reverse):
    b, r, _ = proj.shape
    tb = TOKEN_BLOCK
    assert n_lat % tb == 0 and r % tb == 0
    nblk = r // tb
    nlb = n_lat // tb
    ncb = nblk - nlb
    tri, rowsel, masks = _gla_consts(reverse)
    nl = len(GLA_LEVELS)
    c = GLA_CHUNK
    hg = 8
    wblk = hg * HEAD_W
    fcol = N_HEADS // hg * (2 if reverse else 1)

    def blk(n):
        if reverse:
            return jnp.where(n < ncb, nblk - 1 - n, nlb - 1 - (n - ncb))
        return jnp.where(n < ncb, nlb + n, n - ncb)

    return pl.pallas_call(
        functools.partial(_gla_kernel, tb=tb, hg=hg, reverse=reverse),
        grid=(b, N_HEADS // hg, nblk),
        in_specs=[pl.BlockSpec((1, tb, wblk), lambda bb, h, n: (bb, blk(n), h)),
                  pl.BlockSpec((1, tb, wblk), lambda bb, h, n: (bb, blk(n), fcol + h)),
                  pl.BlockSpec((1, tb, wblk), lambda bb, h, n: (bb, blk(n), 3 * N_HEADS // hg + h)),
                  pl.BlockSpec((1, 1, wblk), lambda bb, h, n: (h, 0, 0)),
                  pl.BlockSpec((c, c), lambda bb, h, n: (0, 0)),
                  pl.BlockSpec((nl, c, HEAD_W), lambda bb, h, n: (0, 0, 0)),
                  pl.BlockSpec((nl + 1, c, c), lambda bb, h, n: (0, 0, 0))],
        out_specs=pl.BlockSpec((1, tb, wblk), lambda bb, h, n: (bb, blk(n), h)),
        out_shape=jax.ShapeDtypeStruct((b, r, N_HEADS * HEAD_W), F32),
        scratch_shapes=[pltpu.VMEM((hg, HEAD_W, HEAD_W), F32)],
        compiler_params=_params(("arbitrary", "arbitrary", "arbitrary")),
        name="gla_bwd" if reverse else "gla_fwd",
    )(proj, proj, proj, lb.reshape(N_HEADS // hg, 1, wblk), jnp.asarray(tri, BF16),
      jnp.asarray(rowsel), jnp.asarray(masks))


def _hgrn_out_kernel(of_ref, ob_ref, g_ref, ng_ref, w_ref, x_ref, gate_ref, o_ref, y_ref, *, tm, n_lat):
    i = pl.program_id(1)
    o = of_ref[0] + ob_ref[0]
    for h in range(N_HEADS):
        sl = slice(h * HEAD_W, (h + 1) * HEAD_W)
        oh = o[:, sl]
        ms = jnp.mean(oh * oh, axis=-1, keepdims=True)
        g = g_ref[0, :, sl]
        y_ref[:, sl] = (oh * lax.rsqrt(ms + NORM_EPS) * ng_ref[:, sl] * (g * _sigmoid(g))).astype(BF16)
    y = jnp.dot(y_ref[...], w_ref[...], preferred_element_type=F32)
    rows = i * tm + lax.broadcasted_iota(I32, (tm, 1), 0)
    gate = jnp.where(rows >= n_lat, gate_ref[0, 0:1], gate_ref[0, 1:2])
    o_ref[0] = x_ref[0] + gate * y


def _hgrn_out(o_f, o_b, proj, norm_g, w_bf16, x, gate2, n_lat):
    b, r, d = x.shape
    tm = _pick(r, (384, 256, 128))
    return pl.pallas_call(
        functools.partial(_hgrn_out_kernel, tm=tm, n_lat=n_lat),
        grid=(b, r // tm),
        in_specs=[pl.BlockSpec((1, tm, d), lambda bb, i: (bb, i, 0)),
                  pl.BlockSpec((1, tm, d), lambda bb, i: (bb, i, 0)),
                  pl.BlockSpec((1, tm, d), lambda bb, i: (bb, i, 4)),
                  pl.BlockSpec((1, d), lambda bb, i: (0, 0)),
                  pl.BlockSpec((d, d), lambda bb, i: (0, 0)),
                  pl.BlockSpec((1, tm, d), lambda bb, i: (bb, i, 0)),
                  pl.BlockSpec((1, 2, d), lambda bb, i: (bb, 0, 0))],
        out_specs=pl.BlockSpec((1, tm, d), lambda bb, i: (bb, i, 0)),
        out_shape=jax.ShapeDtypeStruct((b, r, d), F32),
        scratch_shapes=[pltpu.VMEM((tm, d), BF16)],
        compiler_params=_params(("arbitrary", "arbitrary")),
        name="hgrn_out",
    )(o_f, o_b, proj, jnp.tile(norm_g, N_HEADS).reshape(1, d), w_bf16, x, gate2)


def _attn_kernel(q_ref, k_ref, v_ref, lam_ref, g_ref, o_ref, *, tq, tk, lam_init):
    nk = k_ref.shape[1] // tk
    q = q_ref[0]
    lane = lax.broadcasted_iota(I32, (1, HEAD_W), 1)
    zero = jnp.zeros_like(q)
    qc = [jnp.where(lane < 64, q, zero), jnp.where(lane >= 64, q, zero)]

    def scores(u):
        comp, ki = u
        return lax.dot_general(qc[comp], k_ref[0, ki * tk:(ki + 1) * tk, :], _NT, preferred_element_type=F32)

    units = [(comp, ki) for ki in range(nk) for comp in range(2)]
    m = [jnp.full((tq, LANES), -jnp.inf, F32)] * 2
    acc = [jnp.zeros((tq, 2 * HEAD_W), F32)] * 2
    s_next = scores(units[0])
    for idx, (comp, ki) in enumerate(units):
        s = s_next
        if idx + 1 < len(units):
            s_next = scores(units[idx + 1])
        m_new = jnp.maximum(m[comp], jnp.max(s, axis=-1, keepdims=True))
        p = jnp.exp2(s - jnp.tile(m_new, (1, tk // LANES)))
        alpha = jnp.exp2(m[comp] - m_new)
        v1 = v_ref[0, ki * tk:(ki + 1) * tk, :]
        acc[comp] = jnp.tile(alpha, (1, 2)) * acc[comp] + jnp.dot(p.astype(BF16), v1, preferred_element_type=F32)
        m[comp] = m_new
    out = [a[:, :HEAD_W] / a[:, HEAD_W:] for a in acc]

    lv = lam_ref[...]
    s01 = jnp.sum(lv[0:1] * lv[1:2], axis=-1, keepdims=True)
    s23 = jnp.sum(lv[2:3] * lv[3:4], axis=-1, keepdims=True)
    lam = jnp.exp(s01) - jnp.exp(s23) + lam_init
    o = out[0] - lam * out[1]
    ms = jnp.mean(o * o, axis=-1, keepdims=True)
    o_ref[0] = (o * lax.rsqrt(ms + NORM_EPS) * g_ref[...] * (1.0 - lam_init)).astype(o_ref.dtype)


def _diff_attn(qkv, lam_vecs, subln_g, t, lam_init):
    b, r, _ = qkv.shape
    tq = _pick(t, (512, 256, 128))
    tk = _pick(r, (2816, 1280, 512, 256))
    return pl.pallas_call(
        functools.partial(_attn_kernel, tq=tq, tk=tk, lam_init=lam_init),
        grid=(b, N_HEADS, t // tq),
        in_specs=[pl.BlockSpec((1, tq, HEAD_W), lambda bb, h, i: (bb, i, h)),
                  pl.BlockSpec((1, r, HEAD_W), lambda bb, h, i: (bb, 0, N_HEADS + h)),
                  pl.BlockSpec((1, r, 2 * HEAD_W), lambda bb, h, i: (bb, 0, N_HEADS + h)),
                  pl.BlockSpec(lam_vecs.shape, lambda bb, h, i: (0, 0)),
                  pl.BlockSpec((1, HEAD_W), lambda bb, h, i: (0, 0))],
        out_specs=pl.BlockSpec((1, tq, HEAD_W), lambda bb, h, i: (bb, i, h)),
        out_shape=jax.ShapeDtypeStruct((b, t, N_HEADS * HEAD_W), BF16),
        compiler_params=_params(("arbitrary", "arbitrary", "arbitrary")),
        name="diff_attn",
    )(qkv, qkv, qkv, lam_vecs.astype(F32), subln_g.reshape(1, HEAD_W).astype(F32))


def _outproj_kernel(a_ref, w_ref, x_ref, gate_ref, o_ref):
    y = jnp.dot(a_ref[0], w_ref[...], preferred_element_type=F32)
    o_ref[0] = x_ref[0] + gate_ref[0] * y


def _outproj(a, w_bf16, x_cat, gate):
    b, t, d = a.shape
    tm = _pick(t, (512, 256, 128))
    return pl.pallas_call(
        _outproj_kernel,
        grid=(b, t // tm),
        in_specs=[pl.BlockSpec((1, tm, d), lambda bb, i: (bb, i, 0)),
                  pl.BlockSpec((d, d), lambda bb, i: (0, 0)),
                  pl.BlockSpec((1, tm, d), lambda bb, i: (bb, i, 0)),
                  pl.BlockSpec((1, 1, d), lambda bb, i: (bb, 0, 0))],
        out_specs=pl.BlockSpec((1, tm, d), lambda bb, i: (bb, i, 0)),
        out_shape=jax.ShapeDtypeStruct((b, t, d), F32),
        compiler_params=_params(("arbitrary", "arbitrary")),
        name="outproj",
    )(a, w_bf16, x_cat, gate)


def _moe_pre_kernel(x_ref, g_ref, sc_ref, sh_ref, wr_ref, h_ref, aff_ref, *, tm, n_lat):
    i = pl.program_id(1)
    h = _norm_mod(x_ref[0], g_ref[...], sc_ref[0], sh_ref[0], i * tm, n_lat)
    h_ref[0] = h.astype(BF16)
    logits = lax.dot_general(wr_ref[...], h, _NT, precision=lax.Precision.HIGHEST,
                             preferred_element_type=F32)
    mx = jnp.max(logits, axis=0, keepdims=True)
    ex = jnp.exp(logits - mx)
    aff_ref[0] = ex / jnp.sum(ex, axis=0, keepdims=True)


def _moe_pre(x, g, scale2, shift2, w_router, n_lat):
    b, r, d = x.shape
    ne = w_router.shape[1]
    tm = _pick(r, (768, 512, 384, 256, 128))
    return pl.pallas_call(
        functools.partial(_moe_pre_kernel, tm=tm, n_lat=n_lat),
        grid=(b, r // tm),
        in_specs=[pl.BlockSpec((1, tm, d), lambda bb, i: (bb, i, 0)),
                  pl.BlockSpec((1, d), lambda bb, i: (0, 0)),
                  pl.BlockSpec((1, 2, d), lambda bb, i: (bb, 0, 0)),
                  pl.BlockSpec((1, 2, d), lambda bb, i: (bb, 0, 0)),
                  pl.BlockSpec((ne, d), lambda bb, i: (0, 0))],
        out_specs=[pl.BlockSpec((1, tm, d), lambda bb, i: (bb, i, 0)),
                   pl.BlockSpec((1, ne, tm), lambda bb, i: (bb, 0, i))],
        out_shape=[jax.ShapeDtypeStruct((b, r, d), BF16),
                   jax.ShapeDtypeStruct((b, ne, r), F32)],
        compiler_params=_params(("arbitrary", "arbitrary")),
        name="moe_pre",
    )(x, g.reshape(1, d), scale2, shift2, w_router.T)


def _topk_kernel(aff_ref, tri_ref, pos_ref, gsel_ref, r0_ref, *, cap):
    ne, t = aff_ref.shape[1], aff_ref.shape[2]
    tbk = TOKEN_BLOCK
    nblk = t // tbk
    bits = pltpu.bitcast(aff_ref[0], I32)
    thr = jnp.zeros((ne, 1), I32)
    for bit in range(30, -1, -1):
        cand = thr | (1 << bit)
        cnt = jnp.sum(jnp.where(bits >= cand, 1.0, 0.0), axis=1, keepdims=True)
        thr = jnp.where(cnt >= cap, cand, thr)
    n_gt = jnp.sum(jnp.where(bits > thr, 1.0, 0.0), axis=1, keepdims=True)
    need = cap - n_gt
    tri = tri_ref[...]
    carry_eq = jnp.zeros((ne, 1), F32)
    carry_sel = jnp.zeros((ne, 1), F32)
    r0_ref[0] = jnp.full((ne, LANES), cap, I32)
    for j in range(nblk):
        sl = slice(j * tbk, (j + 1) * tbk)
        a = aff_ref[0, :, sl]
        bj = pltpu.bitcast(a, I32)
        eq = jnp.where(bj == thr, 1.0, 0.0)
        gt = jnp.where(bj > thr, 1.0, 0.0)
        pe = jnp.dot(eq.astype(BF16), tri, preferred_element_type=F32) + carry_eq
        sel = gt + eq * jnp.where(pe - eq < need, 1.0, 0.0)
        ps = jnp.dot(sel.astype(BF16), tri, preferred_element_type=F32) + carry_sel
        pos_ref[0, :, sl] = jnp.where(sel > 0, ps - 1.0, -1.0).astype(I32)
        gsel_ref[0, :, sl] = a * sel
        r0_ref[0, :, j:j + 1] = carry_sel.astype(I32)
        carry_eq = pe[:, tbk - 1:tbk]
        carry_sel = ps[:, tbk - 1:tbk]


def _topk(aff, cap, t, tok_off):
    b, ne, _ = aff.shape
    tbk = TOKEN_BLOCK
    assert t % tbk == 0 and t // tbk < LANES and tok_off % t == 0
    tri = jnp.asarray(np.triu(np.ones((tbk, tbk), np.float32)), BF16)
    return pl.pallas_call(
        functools.partial(_topk_kernel, cap=cap),
        grid=(b,),
        in_specs=[pl.BlockSpec((1, ne, t), lambda bb: (bb, 0, tok_off // t)),
                  pl.BlockSpec((tbk, tbk), lambda bb: (0, 0))],
        out_specs=[pl.BlockSpec((1, ne, t), lambda bb: (bb, 0, 0)),
                   pl.BlockSpec((1, ne, t), lambda bb: (bb, 0, 0)),
                   pl.BlockSpec((1, ne, LANES), lambda bb: (bb, 0, 0))],
        out_shape=[jax.ShapeDtypeStruct((b, ne, t), I32),
                   jax.ShapeDtypeStruct((b, ne, t), F32),
                   jax.ShapeDtypeStruct((b, ne, LANES), I32)],
        compiler_params=_params(("arbitrary",)),
        name="topk",
    )(aff, tri)


def _windows(r0_ref, row, j, w, cap, align):
    lo = r0_ref[row, j]
    hi = r0_ref[row, j + 1]
    off = pl.multiple_of(jnp.minimum((lo // align) * align, cap - w), align)
    return off, (jnp.maximum(hi - (off + w), 0) + w - 1) // w


def _gather_kernel(r0_ref, pos_ref, g_ref, h_ref, xs_ref, gs_ref, acc_ref, gacc_ref, *, w, nblk, ne, cap):
    b, e = pl.program_id(0), pl.program_id(1)
    row = b * ne + e
    tbk = TOKEN_BLOCK
    acc_ref[...] = jnp.zeros_like(acc_ref)
    gacc_ref[...] = jnp.zeros_like(gacc_ref)
    sub = lax.broadcasted_iota(I32, (w, tbk), 0)

    def window(j, off, first_rank):
        rank = off + sub
        if first_rank is not None:
            rank = jnp.where(rank >= first_rank, rank, -2)
        hit = pos_ref[0, 0, j:j + 1, :] == rank
        oh = jnp.where(hit, 1.0, 0.0).astype(BF16)
        acc_ref[pl.ds(off, w), :] += jnp.dot(oh, h_ref[0, j * tbk:(j + 1) * tbk, :], preferred_element_type=F32)
        gsum = jnp.sum(jnp.where(hit, g_ref[0, 0, j:j + 1, :], 0.0), axis=1, keepdims=True)
        gacc_ref[pl.ds(off, w), :] += jnp.broadcast_to(gsum, (w, LANES))

    for j in range(nblk):
        off, n_more = _windows(r0_ref, row, j, w, cap, 8)
        window(j, off, None)

        def more(i, _, j=j, off=off):
            start = off + i * w
            window(j, pl.multiple_of(jnp.minimum(start, cap - w), 8), start)
            return 0

        lax.fori_loop(1, n_more + 1, more, 0)
    xs_ref[0, 0] = acc_ref[...].astype(BF16)
    gs_ref[0, 0] = gacc_ref[...]


def _gather(r0, pos, gate, hb, cap, tok_off):
    b, ne, t = pos.shape
    d = hb.shape[2]
    assert tok_off % t == 0
    tbk = TOKEN_BLOCK
    nblk = t // tbk
    w = min(cap, 64)
    pos4 = pos.reshape(b, ne, nblk, tbk)
    g4 = gate.reshape(b, ne, nblk, tbk)
    grid_spec = pltpu.PrefetchScalarGridSpec(
        num_scalar_prefetch=1,
        grid=(b, ne),
        in_specs=[pl.BlockSpec((1, 1, nblk, tbk), lambda bb, e, r: (bb, e, 0, 0)),
                  pl.BlockSpec((1, 1, nblk, tbk), lambda bb, e, r: (bb, e, 0, 0)),
                  pl.BlockSpec((1, t, d), lambda bb, e, r: (bb, tok_off // t, 0))],
        out_specs=[pl.BlockSpec((1, 1, cap, d), lambda bb, e, r: (e, bb, 0, 0)),
                   pl.BlockSpec((1, 1, cap, LANES), lambda bb, e, r: (e, bb, 0, 0))],
        scratch_shapes=[pltpu.VMEM((cap, d), F32), pltpu.VMEM((cap, LANES), F32)],
    )
    return pl.pallas_call(
        functools.partial(_gather_kernel, w=w, nblk=nblk, ne=ne, cap=cap),
        grid_spec=grid_spec,
        out_shape=[jax.ShapeDtypeStruct((ne, b, cap, d), BF16),
                   jax.ShapeDtypeStruct((ne, b, cap, LANES), F32)],
        compiler_params=_params(("arbitrary", "arbitrary")),
        name="moe_gather",
    )(r0.reshape(b * ne, LANES), pos4, g4, hb)


def _ffn_kernel(*refs, nb, caps, rcs):
    ng = len(caps)
    ins, (wg_ref, wu_ref, wd_ref) = refs[:3 * ng], refs[3 * ng:3 * ng + 3]
    y_refs, acc_refs = refs[3 * ng + 3:4 * ng + 3], refs[4 * ng + 3:]
    f = pl.program_id(1)
    nf = pl.num_programs(1)

    @pl.when(f == 0)
    def _():
        for acc_ref in acc_refs:
            acc_ref[...] = jnp.zeros_like(acc_ref)

    wg = wg_ref[0, 0].astype(BF16)
    wu = wu_ref[0, 0].astype(BF16)
    wd = wd_ref[0, 0].astype(BF16)
    chunks = [(g, c0, rcs[g]) for g in range(ng) for c0 in range(0, nb * caps[g], rcs[g])]

    def gate_up(chunk):
        g, c0, rc = chunk
        x = ins[3 * g][0, c0:c0 + rc, :]
        return jnp.dot(x, wg, preferred_element_type=F32), jnp.dot(x, wu, preferred_element_type=F32)

    nxt = gate_up(chunks[0])
    for i, (g, c0, rc) in enumerate(chunks):
        a, u = nxt
        if i + 1 < len(chunks):
            nxt = gate_up(chunks[i + 1])
        hm = (a * _sigmoid(a) * u).astype(BF16)
        acc_refs[g][c0:c0 + rc, :] += jnp.dot(hm, wd, preferred_element_type=F32)

    @pl.when(f == nf - 1)
    def _():
        for g in range(ng):
            gs_ref, gt_ref = ins[3 * g + 1], ins[3 * g + 2]
            for bb in range(nb):
                sl = slice(bb * caps[g], (bb + 1) * caps[g])
                y_refs[g][0, sl, :] = (acc_refs[g][sl, :] * gs_ref[0, sl, 0:1] * gt_ref[bb]).astype(BF16)


def _ffn(groups, w_gate, w_up, w_down, layer):
    ne, b, _, d = groups[0][0].shape
    fdim = w_gate.shape[3]
    tf = _pick(fdim, (256, 128))
    caps = [xs.shape[2] for xs, _, _ in groups]
    rcs = [_pick(b * cap, (1024, 512, 256, 128, 64)) for cap in caps]
    in_specs, args = [], []
    for (xs, gs, gate_f), cap in zip(groups, caps):
        rows = b * cap
        in_specs += [pl.BlockSpec((1, rows, d), lambda e, f: (e, 0, 0)),
                     pl.BlockSpec((1, rows, LANES), lambda e, f: (e, 0, 0)),
                     pl.BlockSpec((b, 1, d), lambda e, f: (0, 0, 0))]
        args += [xs.reshape(ne, rows, d), gs.reshape(ne, rows, LANES), gate_f]
    in_specs += [pl.BlockSpec((1, 1, d, tf), lambda e, f: (layer, e, 0, f)),
                 pl.BlockSpec((1, 1, d, tf), lambda e, f: (layer, e, 0, f)),
                 pl.BlockSpec((1, 1, tf, d), lambda e, f: (layer, e, f, 0))]
    ys = pl.pallas_call(
        functools.partial(_ffn_kernel, nb=b, caps=caps, rcs=rcs),
        grid=(ne, fdim // tf),
        in_specs=in_specs,
        out_specs=[pl.BlockSpec((1, b * cap, d), lambda e, f: (e, 0, 0)) for cap in caps],
        out_shape=[jax.ShapeDtypeStruct((ne, b * cap, d), BF16) for cap in caps],
        scratch_shapes=[pltpu.VMEM((b * cap, d), F32) for cap in caps],
        compiler_params=_params(("arbitrary", "arbitrary")),
        name="moe_ffn",
    )(*args, w_gate, w_up, w_down)
    return [y.reshape(ne, b, cap, d) for y, cap in zip(ys, caps)]


def _combine_kernel(r0_ref, pos_ref, y_ref, x_ref, o_ref, acc_ref, *, w, ne, cap):
    b, j = pl.program_id(0), pl.program_id(2)
    tbk = TOKEN_BLOCK
    sub = lax.broadcasted_iota(I32, (w, tbk), 0)
    align = 16

    def window(e, off, first_rank):
        rank = off + sub
        if first_rank is not None:
            rank = jnp.where(rank >= first_rank, rank, -2)
        oh = jnp.where(pos_ref[0, 0, e:e + 1, :] == rank, 1.0, 0.0).astype(BF16)
        return lax.dot_general(oh, y_ref[e, 0, pl.ds(off, w), :], _TN, preferred_element_type=F32)

    acc = x_ref[0]
    more = []
    for e in range(ne):
        off, n_more = _windows(r0_ref, b * ne + e, j, w, cap, align)
        acc = acc + window(e, off, None)
        more.append((off, n_more))
    acc_ref[...] = acc
    for e, (off, n_more) in enumerate(more):
        def extra(i, _, e=e, off=off):
            start = off + i * w
            acc_ref[...] += window(e, pl.multiple_of(jnp.minimum(start, cap - w), align), start)
            return 0

        lax.fori_loop(1, n_more + 1, extra, 0)
    o_ref[0] = acc_ref[...]


def _combine(r0, pos, y, x, row_off):
    ne, b, cap, d = y.shape
    t = pos.shape[2]
    tbk = TOKEN_BLOCK
    nblk = t // tbk
    assert row_off % tbk == 0
    boff = row_off // tbk
    w = min(cap, 128)
    dh = d // 2
    pos4 = pos.reshape(b, ne, nblk, tbk).transpose(0, 2, 1, 3)
    grid_spec = pltpu.PrefetchScalarGridSpec(
        num_scalar_prefetch=1,
        grid=(b, 2, nblk),
        in_specs=[pl.BlockSpec((1, 1, ne, tbk), lambda bb, c, j, r: (bb, j, 0, 0)),
                  pl.BlockSpec((ne, 1, cap, dh), lambda bb, c, j, r: (0, bb, 0, c)),
                  pl.BlockSpec((1, tbk, dh), lambda bb, c, j, r: (bb, j + boff, c))],
        out_specs=pl.BlockSpec((1, tbk, dh), lambda bb, c, j, r: (bb, j, c)),
        scratch_shapes=[pltpu.VMEM((tbk, dh), F32)],
    )
    return pl.pallas_call(
        functools.partial(_combine_kernel, w=w, ne=ne, cap=cap),
        grid_spec=grid_spec,
        out_shape=jax.ShapeDtypeStruct((b, t, d), F32),
        compiler_params=_params(("arbitrary", "arbitrary", "arbitrary")),
        name="moe_combine",
    )(r0.reshape(b * ne, LANES), pos4, y, x)


def _final_kernel(x_ref, g_ref, o_ref):
    x = x_ref[0]
    ms = jnp.mean(x * x, axis=-1, keepdims=True)
    o_ref[0] = x * lax.rsqrt(ms + NORM_EPS) * g_ref[...]


def _final(x, g):
    b, t, d = x.shape
    tm = _pick(t, (512, 256, 128))
    return pl.pallas_call(
        _final_kernel,
        grid=(b, t // tm),
        in_specs=[pl.BlockSpec((1, tm, d), lambda bb, i: (bb, i, 0)),
                  pl.BlockSpec((1, d), lambda bb, i: (0, 0))],
        out_specs=pl.BlockSpec((1, tm, d), lambda bb, i: (bb, i, 0)),
        out_shape=jax.ShapeDtypeStruct((b, t, d), F32),
        compiler_params=_params(("arbitrary", "arbitrary")),
        name="final_norm",
    )(x, g.reshape(1, d))


def _moe(aff, hb, x, sets, w_gate, w_up, w_down, layer):
    ne = aff.shape[1]
    routed, groups = [], []
    for off, t, gate_f in sets:
        cap = 2 * t // ne
        pos, gsel, r0 = _topk(aff, cap, t, off)
        xs, gs = _gather(r0, pos, gsel, hb, cap, off)
        routed.append((r0, pos, off))
        groups.append((xs, gs, gate_f))
    ys = _ffn(groups, w_gate, w_up, w_down, layer)
    return [_combine(r0, pos, y, x, off) for (r0, pos, off), y in zip(routed, ys)]


def _rope_tables(t, n_ctx):
    rows = t // GRID_W
    row = jnp.repeat(jnp.arange(rows, dtype=F32), GRID_W)
    col = jnp.tile(jnp.arange(GRID_W, dtype=F32), rows)
    pairs = HEAD_W // 8
    freq = ROPE_BASE ** (-jnp.arange(pairs, dtype=F32) / pairs)
    ang = jnp.concatenate([row[:, None] * freq, col[:, None] * freq], axis=-1)
    cos, sin = jnp.cos(ang), jnp.sin(ang)
    cos = jnp.tile(cos, (1, 4))
    sin = jnp.tile(jnp.concatenate([-sin, sin], axis=-1), (1, 2))
    cos = jnp.concatenate([cos, jnp.ones((n_ctx, HEAD_W), F32)], axis=0)
    sin = jnp.concatenate([sin, jnp.zeros((n_ctx, HEAD_W), F32)], axis=0)
    qs = (HEAD_W // 2) ** -0.5 * math.log2(math.e)
    return jnp.stack([cos * qs, cos]), jnp.stack([sin * qs, sin])


def _diff_w_ext(w):
    d = w.shape[0]
    wv = w[:, 2 * d:].reshape(d, N_HEADS, HEAD_W)
    wv = jnp.concatenate([wv, jnp.zeros_like(wv)], axis=-1).reshape(d, 2 * d)
    return jnp.concatenate([w[:, :2 * d], wv], axis=1).astype(BF16)


def kernel(x, c, ctx, c_ctx, ada_w, ada_b, norm_mix, norm_ffn, norm_final, hgrn_w_in, hgrn_lb_logits, hgrn_norm, hgrn_w_out, diff_w_in, diff_lambda, diff_subln, diff_w_out, moe_router, moe_w_gate, moe_w_up, moe_w_down):
    b, t, d = x.shape
    n_ctx = ctx.shape[1]
    depth = ada_w.shape[0]
    ne = moe_router.shape[2]
    assert depth == 2 and d == N_HEADS * HEAD_W

    cvec = jnp.concatenate([c, c_ctx[None, :], jnp.zeros((8 - b - 1, d), F32)], axis=0)
    mod = _ada(cvec, ada_w, ada_b)
    lower_bounds = jnp.cumsum(jax.nn.softmax(hgrn_lb_logits.astype(F32), axis=0), axis=0)

    def kinds(layer, k, plus_one=False):
        m = mod[layer, :, k * d:(k + 1) * d]
        v = jnp.stack([jnp.broadcast_to(m[b], (b, d)), m[:b]], axis=1)
        return 1.0 + v if plus_one else v

    x_cat = jnp.concatenate([x, ctx], axis=1)

    proj = _inproj(x_cat, norm_mix[0], kinds(0, 1, True), kinds(0, 0), hgrn_w_in[0].astype(BF16), t, F32)
    o_f = _gla(proj, lower_bounds[0], t, False)
    o_b = _gla(proj, lower_bounds[0], t, True)
    x_cat = _hgrn_out(o_f, o_b, proj, hgrn_norm[0], hgrn_w_out[0].astype(BF16), x_cat, kinds(0, 2), t)
    hb, aff = _moe_pre(x_cat, norm_ffn[0], kinds(0, 4, True), kinds(0, 3), moe_router[0], t)
    gate_f = kinds(0, 5)
    x_lat, x_ctx = _moe(aff, hb, x_cat, [(0, t, gate_f[:, 1:2]), (t, n_ctx, gate_f[:, 0:1])],
                        moe_w_gate, moe_w_up, moe_w_down, 0)
    x_cat = jnp.concatenate([x_lat, x_ctx], axis=1)

    lam_init = 0.8 - 0.6 * math.exp(-0.3 * 1)
    qkv = _inproj(x_cat, norm_mix[1], kinds(1, 1, True), kinds(1, 0), _diff_w_ext(diff_w_in[0]), t, BF16,
                  rope_tabs=_rope_tables(t, n_ctx))
    att = _diff_attn(qkv, diff_lambda[0], diff_subln[0], t, lam_init)
    x_lat = _outproj(att, diff_w_out[0].astype(BF16), x_cat, kinds(1, 2)[:, 1:2])
    hb, aff = _moe_pre(x_lat, norm_ffn[1], kinds(1, 4, True), kinds(1, 3), moe_router[1], t)
    (x_lat,) = _moe(aff, hb, x_lat, [(0, t, kinds(1, 5)[:, 1:2])], moe_w_gate, moe_w_up, moe_w_down, 1)
    return _final(x_lat, norm_final)
```

```python
import functools
import math

import jax
import jax.numpy as jnp
import numpy as np
from jax import lax
from jax.experimental import pallas as pl
from jax.experimental.pallas import tpu as pltpu

F32 = jnp.float32
BF16 = jnp.bfloat16
I32 = jnp.int32

NORM_EPS = 1e-6
LANES = 128
HEAD_W = 128
N_HEADS = 8
GLA_CHUNK = 64
GLA_LEVELS = (32, 16, 8, 4, 2, 1)
ROPE_BASE = 10000.0
GRID_W = 64
TOKEN_BLOCK = 256
VMEM_LIMIT = 56 * 1024 * 1024

_NT = (((1,), (1,)), ((), ()))
_TN = (((0,), (0,)), ((), ()))


def _pick(n, cands):
    for c in cands:
        if n % c == 0:
            return c
    raise ValueError(f"no tile for {n} in {cands}")


def _params(sem):
    return pltpu.CompilerParams(dimension_semantics=sem, vmem_limit_bytes=VMEM_LIMIT)


def _sigmoid(x):
    return 1.0 / (1.0 + jnp.exp(-x))


def _norm_mod(x, g, scale2, shift2, row0, n_lat):
    ms = jnp.mean(x * x, axis=-1, keepdims=True)
    y = x * lax.rsqrt(ms + NORM_EPS) * g
    rows = row0 + lax.broadcasted_iota(I32, (x.shape[0], 1), 0)
    is_ctx = rows >= n_lat
    sc = jnp.where(is_ctx, scale2[0:1], scale2[1:2])
    sh = jnp.where(is_ctx, shift2[0:1], shift2[1:2])
    return y * sc + sh


def _ada_kernel(c_ref, w_ref, b_ref, o_ref):
    c = c_ref[...]
    s = c * _sigmoid(c)
    o_ref[0] = jnp.dot(s, w_ref[0], precision=lax.Precision.HIGHEST,
                       preferred_element_type=F32) + b_ref[0]


def _ada(cvec, ada_w, ada_b):
    depth, d, n = ada_w.shape
    rows = cvec.shape[0]
    tn = _pick(n, (1024, 512, 256, 128))
    return pl.pallas_call(
        _ada_kernel,
        grid=(depth, n // tn),
        in_specs=[pl.BlockSpec((rows, d), lambda l, j: (0, 0)),
                  pl.BlockSpec((1, d, tn), lambda l, j: (l, 0, j)),
                  pl.BlockSpec((1, 1, tn), lambda l, j: (l, 0, j))],
        out_specs=pl.BlockSpec((1, rows, tn), lambda l, j: (l, 0, j)),
        out_shape=jax.ShapeDtypeStruct((depth, rows, n), F32),
        compiler_params=_params(("arbitrary", "arbitrary")),
        name="adaln",
    )(cvec, ada_w, ada_b.reshape(depth, 1, n))


def _inproj_kernel(x_ref, g_ref, sc_ref, sh_ref, w_ref, *rest, tm, tn, n_lat, rope):
    if rope:
        cos_ref, sin_ref, o_ref = rest
    else:
        (o_ref,) = rest
    i = pl.program_id(1)
    h = _norm_mod(x_ref[0], g_ref[...], sc_ref[0], sh_ref[0], i * tm, n_lat).astype(BF16)
    for n in range(w_ref.shape[1] // tn):
        cols = slice(n * tn, (n + 1) * tn)
        acc = jnp.dot(h, w_ref[:, cols], preferred_element_type=F32)
        if not rope:
            o_ref[0, :, cols] = acc.astype(o_ref.dtype)
        elif n < 2:
            cos = cos_ref[n]
            sin = sin_ref[n]
            lane = lax.broadcasted_iota(I32, (1, HEAD_W), 1)
            first = (lane % 64) < 32
            for hd in range(tn // HEAD_W):
                a = acc[:, hd * HEAD_W:(hd + 1) * HEAD_W]
                rot = jnp.where(first, pltpu.roll(a, HEAD_W - 32, 1), pltpu.roll(a, 32, 1))
                o_ref[0, :, n * tn + hd * HEAD_W:n * tn + (hd + 1) * HEAD_W] = (
                    a * cos + rot * sin).astype(o_ref.dtype)
        else:
            col = lax.broadcasted_iota(I32, (1, tn), 1)
            o_ref[0, :, cols] = jnp.where(col % (2 * HEAD_W) >= HEAD_W, 1.0, acc).astype(o_ref.dtype)


def _inproj(x, g, scale2, shift2, w_bf16, n_lat, out_dtype, rope_tabs=None):
    b, r, d = x.shape
    n = w_bf16.shape[1]
    tm = _pick(r, (384, 256, 128))
    tn = 1024
    rope = rope_tabs is not None
    in_specs = [pl.BlockSpec((1, tm, d), lambda bb, i: (bb, i, 0)),
                pl.BlockSpec((1, d), lambda bb, i: (0, 0)),
                pl.BlockSpec((1, 2, d), lambda bb, i: (bb, 0, 0)),
                pl.BlockSpec((1, 2, d), lambda bb, i: (bb, 0, 0)),
                pl.BlockSpec((d, n), lambda bb, i: (0, 0))]
    args = [x, g.reshape(1, d), scale2, shift2, w_bf16]
    if rope:
        cos_t, sin_t = rope_tabs
        spec = pl.BlockSpec((2, tm, HEAD_W), lambda bb, i: (0, i, 0))
        in_specs += [spec, spec]
        args += [cos_t, sin_t]
    return pl.pallas_call(
        functools.partial(_inproj_kernel, tm=tm, tn=tn, n_lat=n_lat, rope=rope),
        grid=(b, r // tm),
        in_specs=in_specs,
        out_specs=pl.BlockSpec((1, tm, n), lambda bb, i: (bb, i, 0)),
        out_shape=jax.ShapeDtypeStruct((b, r, n), out_dtype),
        compiler_params=_params(("arbitrary", "arbitrary")),
        name="inproj_rope" if rope else "inproj",
    )(*args)


def _gla_consts(reverse):
    c = GLA_CHUNK
    nl = len(GLA_LEVELS)
    t = np.arange(c)
    p = c - 1 - t if reverse else t
    tri = (p[None, :] <= p[:, None]).astype(np.float32)
    rowsel = np.zeros((nl, c, HEAD_W), np.float32)
    masks = np.zeros((nl + 1, c, c), np.float32)
    for li, s in enumerate(GLA_LEVELS):
        blk = p // (2 * s)
        second = (p // s) % 2 == 1
        rowsel[li] = second[:, None]
        masks[li] = (blk[:, None] == blk[None, :]) & second[:, None] & (~second)[None, :]
    masks[nl] = np.eye(c)
    return tri, rowsel, masks


def _level_ref(b, s, reverse):
    c = b.shape[0]
    at = s if reverse else s - 1
    if 2 * s >= 8:
        pieces = [jnp.broadcast_to(b[blk * 2 * s + at:blk * 2 * s + at + 1], (2 * s, HEAD_W))
                  for blk in range(c // (2 * s))]
        return pieces[0] if len(pieces) == 1 else jnp.concatenate(pieces, axis=0)
    sub = lax.broadcasted_iota(I32, (8, HEAD_W), 0)
    outs = []
    for v in range(c // 8):
        acc = None
        for blk in range(8 // (2 * s)):
            row = 8 * v + blk * 2 * s + at
            cand = jnp.broadcast_to(b[row:row + 1], (8, HEAD_W))
            acc = cand if acc is None else jnp.where(sub >= blk * 2 * s, cand, acc)
        outs.append(acc)
    return jnp.concatenate(outs, axis=0)


def _gla_kernel(q_ref, f_ref, v_ref, lb_ref, tri_ref, rowsel_ref, masks_ref, o_ref, st_ref, *, tb, hg, reverse):
    c = GLA_CHUNK
    nl = len(GLA_LEVELS)
    nc = tb // c

    @pl.when(pl.program_id(2) == 0)
    def _():
        st_ref[...] = jnp.zeros_like(st_ref)

    tri = tri_ref[...]
    last = 0 if reverse else c - 1
    starts = [(nc - 1 - ci if reverse else ci) * c for ci in range(nc)]
    units = [(hd, r0) for hd in range(hg) for r0 in starts]

    qf, kk, vb, b3 = [], [], [], []
    for hd, r0 in units:
        lanes = slice(hd * HEAD_W, (hd + 1) * HEAD_W)
        lb = lb_ref[0, :, lanes]
        q = q_ref[0, r0:r0 + c, lanes]
        qf.append(q * _sigmoid(q))
        f = lb + (1.0 - lb) * _sigmoid(f_ref[0, r0:r0 + c, lanes])
        lf = jnp.log(f)
        kk.append(1.0 - f)
        vb.append(v_ref[0, r0:r0 + c, lanes].astype(BF16))
        hi = lf.astype(BF16)
        r1 = lf - hi.astype(F32)
        mid = r1.astype(BF16)
        lo = (r1 - mid.astype(F32)).astype(BF16)
        b3.append(jnp.dot(tri, jnp.concatenate([hi, mid, lo], axis=1), preferred_element_type=F32))
    bc = [x[:, :HEAD_W] + x[:, HEAD_W:2 * HEAD_W] + x[:, 2 * HEAD_W:] for x in b3]

    a = []
    for i in range(len(units)):
        ai = masks_ref[nl] * lax.dot_general(qf[i].astype(BF16), kk[i].astype(BF16), _NT,
                                             preferred_element_type=F32)
        for li, s in enumerate(GLA_LEVELS):
            e = jnp.exp(-jnp.abs(bc[i] - _level_ref(bc[i], s, reverse)))
            xb = (jnp.where(rowsel_ref[li] > 0, qf[i], kk[i]) * e).astype(BF16)
            ai = ai + masks_ref[li] * lax.dot_general(xb, xb, _NT, preferred_element_type=F32)
        a.append(ai.astype(BF16))

    o_intra = [jnp.dot(a[i], vb[i], preferred_element_type=F32) for i in range(len(units))]
    upd = []
    for i in range(len(units)):
        kdec = (kk[i] * jnp.exp(bc[i][last:last + 1] - bc[i])).astype(BF16)
        upd.append(lax.dot_general(vb[i], kdec, _TN, preferred_element_type=F32))

    st = [st_ref[hd] for hd in range(hg)]
    for i, (hd, r0) in enumerate(units):
        qd = (qf[i] * jnp.exp(bc[i])).astype(BF16)
        o_ref[0, r0:r0 + c, hd * HEAD_W:(hd + 1) * HEAD_W] = o_intra[i] + lax.dot_general(
            qd, st[hd].astype(BF16), _NT, preferred_element_type=F32)
        st[hd] = st[hd] * jnp.exp(bc[i][last:last + 1]) + upd[i]
    for hd in range(hg):
        st_ref[hd] = st[hd]


def _gla(proj, lb, n_lat, reverse):
    b, r, _ = proj.shape
    tb = TOKEN_BLOCK
    assert n_lat % tb == 0 and r % tb == 0
    nblk = r // tb
    nlb = n_lat // tb
    ncb = nblk - nlb
    tri, rowsel, masks = _gla_consts(reverse)
    nl = len(GLA_LEVELS)
    c = GLA_CHUNK
    hg = 8
    wblk = hg * HEAD_W
    fcol = N_HEADS // hg * (2 if reverse else 1)

    def blk(n):
        if reverse:
            return jnp.where(n < ncb, nblk - 1 - n, nlb - 1 - (n - ncb))
        return jnp.where(n < ncb, nlb + n, n - ncb)

    return pl.pallas_call(
        functools.partial(_gla_kernel, tb=tb, hg=hg, reverse=reverse),
        grid=(b, N_HEADS // hg, nblk),
        in_specs=[pl.BlockSpec((1, tb, wblk), lambda bb, h, n: (bb, blk(n), h)),
                  pl.BlockSpec((1, tb, wblk), lambda bb, h, n: (bb, blk(n), fcol + h)),
                  pl.BlockSpec((1, tb, wblk), lambda bb, h, n: (bb, blk(n), 3 * N_HEADS // hg + h)),
                  pl.BlockSpec((1, 1, wblk), lambda bb, h, n: (h, 0, 0)),
                  pl.BlockSpec((c, c), lambda bb, h, n: (0, 0)),
                  pl.BlockSpec((nl, c, HEAD_W), lambda bb, h, n: (0, 0, 0)),
                  pl.BlockSpec((nl + 1, c, c), lambda bb, h, n: (0, 0, 0))],
        out_specs=pl.BlockSpec((1, tb, wblk), lambda bb, h, n: (bb, blk(n), h)),
        out_shape=jax.ShapeDtypeStruct((b, r, N_HEADS * HEAD_W), F32),
        scratch_shapes=[pltpu.VMEM((hg, HEAD_W, HEAD_W), F32)],
        compiler_params=_params(("arbitrary", "arbitrary", "arbitrary")),
        name="gla_bwd" if reverse else "gla_fwd",
    )(proj, proj, proj, lb.reshape(N_HEADS // hg, 1, wblk), jnp.asarray(tri, BF16),
      jnp.asarray(rowsel), jnp.asarray(masks))


def _hgrn_out_kernel(of_ref, ob_ref, g_ref, ng_ref, w_ref, x_ref, gate_ref, o_ref, y_ref, *, tm, n_lat):
    i = pl.program_id(1)
    o = of_ref[0] + ob_ref[0]
    for h in range(N_HEADS):
        sl = slice(h * HEAD_W, (h + 1) * HEAD_W)
        oh = o[:, sl]
        ms = jnp.mean(oh * oh, axis=-1, keepdims=True)
        g = g_ref[0, :, sl]
        y_ref[:, sl] = (oh * lax.rsqrt(ms + NORM_EPS) * ng_ref[:, sl] * (g * _sigmoid(g))).astype(BF16)
    y = jnp.dot(y_ref[...], w_ref[...], preferred_element_type=F32)
    rows = i * tm + lax.broadcasted_iota(I32, (tm, 1), 0)
    gate = jnp.where(rows >= n_lat, gate_ref[0, 0:1], gate_ref[0, 1:2])
    o_ref[0] = x_ref[0] + gate * y


def _hgrn_out(o_f, o_b, proj, norm_g, w_bf16, x, gate2, n_lat):
    b, r, d = x.shape
    tm = _pick(r, (384, 256, 128))
    return pl.pallas_call(
        functools.partial(_hgrn_out_kernel, tm=tm, n_lat=n_lat),
        grid=(b, r // tm),
        in_specs=[pl.BlockSpec((1, tm, d), lambda bb, i: (bb, i, 0)),
                  pl.BlockSpec((1, tm, d), lambda bb, i: (bb, i, 0)),
                  pl.BlockSpec((1, tm, d), lambda bb, i: (bb, i, 4)),
                  pl.BlockSpec((1, d), lambda bb, i: (0, 0)),
                  pl.BlockSpec((d, d), lambda bb, i: (0, 0)),
                  pl.BlockSpec((1, tm, d), lambda bb, i: (bb, i, 0)),
                  pl.BlockSpec((1, 2, d), lambda bb, i: (bb, 0, 0))],
        out_specs=pl.BlockSpec((1, tm, d), lambda bb, i: (bb, i, 0)),
        out_shape=jax.ShapeDtypeStruct((b, r, d), F32),
        scratch_shapes=[pltpu.VMEM((tm, d), BF16)],
        compiler_params=_params(("arbitrary", "arbitrary")),
        name="hgrn_out",
    )(o_f, o_b, proj, jnp.tile(norm_g, N_HEADS).reshape(1, d), w_bf16, x, gate2)


def _attn_kernel(q_ref, k_ref, v_ref, lam_ref, g_ref, o_ref, *, tq, tk, lam_init):
    nk = k_ref.shape[1] // tk
    q = q_ref[0]
    lane = lax.broadcasted_iota(I32, (1, HEAD_W), 1)
    zero = jnp.zeros_like(q)
    qc = [jnp.where(lane < 64, q, zero), jnp.where(lane >= 64, q, zero)]

    r = k_ref.shape[1]
    bounds = list(range(0, r, tk)) + [r]
    if bounds[-1] - bounds[-2] > 256:
        bounds.insert(-1, r - 256)
    tiles = list(zip(bounds[:-1], bounds[1:]))

    def scores(u):
        comp, (k0, k1) = u
        return lax.dot_general(qc[comp], k_ref[0, k0:k1, :], _NT, preferred_element_type=F32)

    units = [(comp, kt) for kt in tiles for comp in range(2)]
    m = [jnp.full((tq, LANES), -jnp.inf, F32)] * 2
    acc = [jnp.zeros((tq, 2 * HEAD_W), F32)] * 2
    s_next = scores(units[0])
    for idx, (comp, (k0, k1)) in enumerate(units):
        s = s_next
        if idx + 1 < len(units):
            s_next = scores(units[idx + 1])
        m_new = jnp.maximum(m[comp], jnp.max(s, axis=-1, keepdims=True))
        p = jnp.exp2(s - jnp.tile(m_new, (1, (k1 - k0) // LANES)))
        alpha = jnp.exp2(m[comp] - m_new)
        v1 = v_ref[0, k0:k1, :]
        acc[comp] = jnp.tile(alpha, (1, 2)) * acc[comp] + jnp.dot(p.astype(BF16), v1, preferred_element_type=F32)
        m[comp] = m_new
    out = [a[:, :HEAD_W] / a[:, HEAD_W:] for a in acc]

    lv = lam_ref[...]
    s01 = jnp.sum(lv[0:1] * lv[1:2], axis=-1, keepdims=True)
    s23 = jnp.sum(lv[2:3] * lv[3:4], axis=-1, keepdims=True)
    lam = jnp.exp(s01) - jnp.exp(s23) + lam_init
    o = out[0] - lam * out[1]
    ms = jnp.mean(o * o, axis=-1, keepdims=True)
    o_ref[0] = (o * lax.rsqrt(ms + NORM_EPS) * g_ref[...] * (1.0 - lam_init)).astype(o_ref.dtype)


def _diff_attn(qkv, lam_vecs, subln_g, t, lam_init):
    b, r, _ = qkv.shape
    tq = _pick(t, (512, 256, 128))
    tk = _pick(r, (2816, 1280, 512, 256))
    return pl.pallas_call(
        functools.partial(_attn_kernel, tq=tq, tk=tk, lam_init=lam_init),
        grid=(b, N_HEADS, t // tq),
        in_specs=[pl.BlockSpec((1, tq, HEAD_W), lambda bb, h, i: (bb, i, h)),
                  pl.BlockSpec((1, r, HEAD_W), lambda bb, h, i: (bb, 0, N_HEADS + h)),
                  pl.BlockSpec((1, r, 2 * HEAD_W), lambda bb, h, i: (bb, 0, N_HEADS + h)),
                  pl.BlockSpec(lam_vecs.shape, lambda bb, h, i: (0, 0)),
                  pl.BlockSpec((1, HEAD_W), lambda bb, h, i: (0, 0))],
        out_specs=pl.BlockSpec((1, tq, HEAD_W), lambda bb, h, i: (bb, i, h)),
        out_shape=jax.ShapeDtypeStruct((b, t, N_HEADS * HEAD_W), BF16),
        compiler_params=_params(("arbitrary", "arbitrary", "arbitrary")),
        name="diff_attn",
    )(qkv, qkv, qkv, lam_vecs.astype(F32), subln_g.reshape(1, HEAD_W).astype(F32))


def _outproj_kernel(a_ref, w_ref, x_ref, gate_ref, o_ref):
    y = jnp.dot(a_ref[0], w_ref[...], preferred_element_type=F32)
    o_ref[0] = x_ref[0] + gate_ref[0] * y


def _outproj(a, w_bf16, x_cat, gate):
    b, t, d = a.shape
    tm = _pick(t, (512, 256, 128))
    return pl.pallas_call(
        _outproj_kernel,
        grid=(b, t // tm),
        in_specs=[pl.BlockSpec((1, tm, d), lambda bb, i: (bb, i, 0)),
                  pl.BlockSpec((d, d), lambda bb, i: (0, 0)),
                  pl.BlockSpec((1, tm, d), lambda bb, i: (bb, i, 0)),
                  pl.BlockSpec((1, 1, d), lambda bb, i: (bb, 0, 0))],
        out_specs=pl.BlockSpec((1, tm, d), lambda bb, i: (bb, i, 0)),
        out_shape=jax.ShapeDtypeStruct((b, t, d), F32),
        compiler_params=_params(("arbitrary", "arbitrary")),
        name="outproj",
    )(a, w_bf16, x_cat, gate)


def _moe_pre_kernel(x_ref, g_ref, sc_ref, sh_ref, wr_ref, h_ref, aff_ref, *, tm, n_lat):
    i = pl.program_id(1)
    h = _norm_mod(x_ref[0], g_ref[...], sc_ref[0], sh_ref[0], i * tm, n_lat)
    h_ref[0] = h.astype(BF16)
    logits = lax.dot_general(wr_ref[...], h, _NT, precision=lax.Precision.HIGHEST,
                             preferred_element_type=F32)
    mx = jnp.max(logits, axis=0, keepdims=True)
    ex = jnp.exp(logits - mx)
    aff_ref[0] = ex / jnp.sum(ex, axis=0, keepdims=True)


def _moe_pre(x, g, scale2, shift2, w_router, n_lat):
    b, r, d = x.shape
    ne = w_router.shape[1]
    tm = _pick(r, (768, 512, 384, 256, 128))
    return pl.pallas_call(
        functools.partial(_moe_pre_kernel, tm=tm, n_lat=n_lat),
        grid=(b, r // tm),
        in_specs=[pl.BlockSpec((1, tm, d), lambda bb, i: (bb, i, 0)),
                  pl.BlockSpec((1, d), lambda bb, i: (0, 0)),
                  pl.BlockSpec((1, 2, d), lambda bb, i: (bb, 0, 0)),
                  pl.BlockSpec((1, 2, d), lambda bb, i: (bb, 0, 0)),
                  pl.BlockSpec((ne, d), lambda bb, i: (0, 0))],
        out_specs=[pl.BlockSpec((1, tm, d), lambda bb, i: (bb, i, 0)),
                   pl.BlockSpec((1, ne, tm), lambda bb, i: (bb, 0, i))],
        out_shape=[jax.ShapeDtypeStruct((b, r, d), BF16),
                   jax.ShapeDtypeStruct((b, ne, r), F32)],
        compiler_params=_params(("arbitrary", "arbitrary")),
        name="moe_pre",
    )(x, g.reshape(1, d), scale2, shift2, w_router.T)


def _topk_kernel(aff_ref, tri_ref, pos_ref, gsel_ref, r0_ref, *, cap):
    ne, t = aff_ref.shape[1], aff_ref.shape[2]
    tbk = TOKEN_BLOCK
    nblk = t // tbk
    bits = pltpu.bitcast(aff_ref[0], I32)
    thr = jnp.zeros((ne, 1), I32)
    for bit in range(30, -1, -1):
        cand = thr | (1 << bit)
        cnt = jnp.sum(jnp.where(bits >= cand, 1.0, 0.0), axis=1, keepdims=True)
        thr = jnp.where(cnt >= cap, cand, thr)
    n_gt = jnp.sum(jnp.where(bits > thr, 1.0, 0.0), axis=1, keepdims=True)
    need = cap - n_gt
    tri = tri_ref[...]
    carry_eq = jnp.zeros((ne, 1), F32)
    carry_sel = jnp.zeros((ne, 1), F32)
    r0_ref[0] = jnp.full((ne, LANES), cap, I32)
    for j in range(nblk):
        sl = slice(j * tbk, (j + 1) * tbk)
        a = aff_ref[0, :, sl]
        bj = pltpu.bitcast(a, I32)
        eq = jnp.where(bj == thr, 1.0, 0.0)
        gt = jnp.where(bj > thr, 1.0, 0.0)
        pe = jnp.dot(eq.astype(BF16), tri, preferred_element_type=F32) + carry_eq
        sel = gt + eq * jnp.where(pe - eq < need, 1.0, 0.0)
        ps = jnp.dot(sel.astype(BF16), tri, preferred_element_type=F32) + carry_sel
        pos_ref[0, :, sl] = jnp.where(sel > 0, ps - 1.0, -1.0).astype(I32)
        gsel_ref[0, :, sl] = a * sel
        r0_ref[0, :, j:j + 1] = carry_sel.astype(I32)
        carry_eq = pe[:, tbk - 1:tbk]
        carry_sel = ps[:, tbk - 1:tbk]


def _topk(aff, cap, t, tok_off):
    b, ne, _ = aff.shape
    tbk = TOKEN_BLOCK
    assert t % tbk == 0 and t // tbk < LANES and tok_off % t == 0
    tri = jnp.asarray(np.triu(np.ones((tbk, tbk), np.float32)), BF16)
    return pl.pallas_call(
        functools.partial(_topk_kernel, cap=cap),
        grid=(b,),
        in_specs=[pl.BlockSpec((1, ne, t), lambda bb: (bb, 0, tok_off // t)),
                  pl.BlockSpec((tbk, tbk), lambda bb: (0, 0))],
        out_specs=[pl.BlockSpec((1, ne, t), lambda bb: (bb, 0, 0)),
                   pl.BlockSpec((1, ne, t), lambda bb: (bb, 0, 0)),
                   pl.BlockSpec((1, ne, LANES), lambda bb: (bb, 0, 0))],
        out_shape=[jax.ShapeDtypeStruct((b, ne, t), I32),
                   jax.ShapeDtypeStruct((b, ne, t), F32),
                   jax.ShapeDtypeStruct((b, ne, LANES), I32)],
        compiler_params=_params(("arbitrary",)),
        name="topk",
    )(aff, tri)


def _windows(r0_ref, row, j, w, cap, align):
    lo = r0_ref[row, j]
    hi = r0_ref[row, j + 1]
    off = pl.multiple_of(jnp.minimum((lo // align) * align, cap - w), align)
    return off, (jnp.maximum(hi - (off + w), 0) + w - 1) // w


def _gather_kernel(r0_ref, pos_ref, g_ref, h_ref, xs_ref, gs_ref, acc_ref, gacc_ref, *, w, nblk, ne, cap):
    b, e = pl.program_id(0), pl.program_id(1)
    row = b * ne + e
    tbk = TOKEN_BLOCK
    acc_ref[...] = jnp.zeros_like(acc_ref)
    gacc_ref[...] = jnp.zeros_like(gacc_ref)
    sub = lax.broadcasted_iota(I32, (w, tbk), 0)

    def window(j, off, first_rank):
        rank = off + sub
        if first_rank is not None:
            rank = jnp.where(rank >= first_rank, rank, -2)
        hit = pos_ref[0, 0, j:j + 1, :] == rank
        oh = jnp.where(hit, 1.0, 0.0).astype(BF16)
        acc_ref[pl.ds(off, w), :] += jnp.dot(oh, h_ref[0, j * tbk:(j + 1) * tbk, :], preferred_element_type=F32)
        gsum = jnp.sum(jnp.where(hit, g_ref[0, 0, j:j + 1, :], 0.0), axis=1, keepdims=True)
        gacc_ref[pl.ds(off, w), :] += jnp.broadcast_to(gsum, (w, LANES))

    for j in range(nblk):
        off, n_more = _windows(r0_ref, row, j, w, cap, 8)
        window(j, off, None)

        def more(i, _, j=j, off=off):
            start = off + i * w
            window(j, pl.multiple_of(jnp.minimum(start, cap - w), 8), start)
            return 0

        lax.fori_loop(1, n_more + 1, more, 0)
    xs_ref[0, 0] = acc_ref[...].astype(BF16)
    gs_ref[0, 0] = gacc_ref[...]


def _gather(r0, pos, gate, hb, cap, tok_off):
    b, ne, t = pos.shape
    d = hb.shape[2]
    assert tok_off % t == 0
    tbk = TOKEN_BLOCK
    nblk = t // tbk
    w = min(cap, 64)
    pos4 = pos.reshape(b, ne, nblk, tbk)
    g4 = gate.reshape(b, ne, nblk, tbk)
    grid_spec = pltpu.PrefetchScalarGridSpec(
        num_scalar_prefetch=1,
        grid=(b, ne),
        in_specs=[pl.BlockSpec((1, 1, nblk, tbk), lambda bb, e, r: (bb, e, 0, 0)),
                  pl.BlockSpec((1, 1, nblk, tbk), lambda bb, e, r: (bb, e, 0, 0)),
                  pl.BlockSpec((1, t, d), lambda bb, e, r: (bb, tok_off // t, 0))],
        out_specs=[pl.BlockSpec((1, 1, cap, d), lambda bb, e, r: (e, bb, 0, 0)),
                   pl.BlockSpec((1, 1, cap, LANES), lambda bb, e, r: (e, bb, 0, 0))],
        scratch_shapes=[pltpu.VMEM((cap, d), F32), pltpu.VMEM((cap, LANES), F32)],
    )
    return pl.pallas_call(
        functools.partial(_gather_kernel, w=w, nblk=nblk, ne=ne, cap=cap),
        grid_spec=grid_spec,
        out_shape=[jax.ShapeDtypeStruct((ne, b, cap, d), BF16),
                   jax.ShapeDtypeStruct((ne, b, cap, LANES), F32)],
        compiler_params=_params(("arbitrary", "arbitrary")),
        name="moe_gather",
    )(r0.reshape(b * ne, LANES), pos4, g4, hb)


def _ffn_kernel(*refs, nb, caps, rcs):
    ng = len(caps)
    ins, (wg_ref, wu_ref, wd_ref) = refs[:3 * ng], refs[3 * ng:3 * ng + 3]
    y_refs, acc_refs = refs[3 * ng + 3:4 * ng + 3], refs[4 * ng + 3:]
    f = pl.program_id(1)
    nf = pl.num_programs(1)

    @pl.when(f == 0)
    def _():
        for acc_ref in acc_refs:
            acc_ref[...] = jnp.zeros_like(acc_ref)

    wg = wg_ref[0, 0].astype(BF16)
    wu = wu_ref[0, 0].astype(BF16)
    wd = wd_ref[0, 0].astype(BF16)
    chunks = [(g, c0, rcs[g]) for g in range(ng) for c0 in range(0, nb * caps[g], rcs[g])]

    def gate_up(chunk):
        g, c0, rc = chunk
        x = ins[3 * g][0, c0:c0 + rc, :]
        return jnp.dot(x, wg, preferred_element_type=F32), jnp.dot(x, wu, preferred_element_type=F32)

    nxt = gate_up(chunks[0])
    for i, (g, c0, rc) in enumerate(chunks):
        a, u = nxt
        if i + 1 < len(chunks):
            nxt = gate_up(chunks[i + 1])
        hm = (a * _sigmoid(a) * u).astype(BF16)
        acc_refs[g][c0:c0 + rc, :] += jnp.dot(hm, wd, preferred_element_type=F32)

    @pl.when(f == nf - 1)
    def _():
        for g in range(ng):
            gs_ref, gt_ref = ins[3 * g + 1], ins[3 * g + 2]
            for bb in range(nb):
                sl = slice(bb * caps[g], (bb + 1) * caps[g])
                y_refs[g][0, sl, :] = (acc_refs[g][sl, :] * gs_ref[0, sl, 0:1] * gt_ref[bb]).astype(BF16)


def _ffn(groups, w_gate, w_up, w_down, layer):
    ne, b, _, d = groups[0][0].shape
    fdim = w_gate.shape[3]
    tf = _pick(fdim, (256, 128))
    caps = [xs.shape[2] for xs, _, _ in groups]
    rcs = [_pick(b * cap, (1024, 512, 256, 128, 64)) for cap in caps]
    in_specs, args = [], []
    for (xs, gs, gate_f), cap in zip(groups, caps):
        rows = b * cap
        in_specs += [pl.BlockSpec((1, rows, d), lambda e, f: (e, 0, 0)),
                     pl.BlockSpec((1, rows, LANES), lambda e, f: (e, 0, 0)),
                     pl.BlockSpec((b, 1, d), lambda e, f: (0, 0, 0))]
        args += [xs.reshape(ne, rows, d), gs.reshape(ne, rows, LANES), gate_f]
    in_specs += [pl.BlockSpec((1, 1, d, tf), lambda e, f: (layer, e, 0, f)),
                 pl.BlockSpec((1, 1, d, tf), lambda e, f: (layer, e, 0, f)),
                 pl.BlockSpec((1, 1, tf, d), lambda e, f: (layer, e, f, 0))]
    ys = pl.pallas_call(
        functools.partial(_ffn_kernel, nb=b, caps=caps, rcs=rcs),
        grid=(ne, fdim // tf),
        in_specs=in_specs,
        out_specs=[pl.BlockSpec((1, b * cap, d), lambda e, f: (e, 0, 0)) for cap in caps],
        out_shape=[jax.ShapeDtypeStruct((ne, b * cap, d), BF16) for cap in caps],
        scratch_shapes=[pltpu.VMEM((b * cap, d), F32) for cap in caps],
        compiler_params=_params(("arbitrary", "arbitrary")),
        name="moe_ffn",
    )(*args, w_gate, w_up, w_down)
    return [y.reshape(ne, b, cap, d) for y, cap in zip(ys, caps)]


def _combine_kernel(r0_ref, pos_ref, y_ref, x_ref, o_ref, acc_ref, *, w, ne, cap):
    b, j = pl.program_id(0), pl.program_id(2)
    tbk = TOKEN_BLOCK
    sub = lax.broadcasted_iota(I32, (w, tbk), 0)
    align = 16

    def window(e, off, first_rank):
        rank = off + sub
        if first_rank is not None:
            rank = jnp.where(rank >= first_rank, rank, -2)
        oh = jnp.where(pos_ref[0, 0, e:e + 1, :] == rank, 1.0, 0.0).astype(BF16)
        return lax.dot_general(oh, y_ref[e, 0, pl.ds(off, w), :], _TN, preferred_element_type=F32)

    acc = x_ref[0]
    more = []
    for e in range(ne):
        off, n_more = _windows(r0_ref, b * ne + e, j, w, cap, align)
        acc = acc + window(e, off, None)
        more.append((off, n_more))
    acc_ref[...] = acc
    for e, (off, n_more) in enumerate(more):
        def extra(i, _, e=e, off=off):
            start = off + i * w
            acc_ref[...] += window(e, pl.multiple_of(jnp.minimum(start, cap - w), align), start)
            return 0

        lax.fori_loop(1, n_more + 1, extra, 0)
    o_ref[0] = acc_ref[...]


def _combine(r0, pos, y, x, row_off):
    ne, b, cap, d = y.shape
    t = pos.shape[2]
    tbk = TOKEN_BLOCK
    nblk = t // tbk
    assert row_off % tbk == 0
    boff = row_off // tbk
    w = min(cap, 256)
    dh = d // 2
    pos4 = pos.reshape(b, ne, nblk, tbk).transpose(0, 2, 1, 3)
    grid_spec = pltpu.PrefetchScalarGridSpec(
        num_scalar_prefetch=1,
        grid=(b, 2, nblk),
        in_specs=[pl.BlockSpec((1, 1, ne, tbk), lambda bb, c, j, r: (bb, j, 0, 0)),
                  pl.BlockSpec((ne, 1, cap, dh), lambda bb, c, j, r: (0, bb, 0, c)),
                  pl.BlockSpec((1, tbk, dh), lambda bb, c, j, r: (bb, j + boff, c))],
        out_specs=pl.BlockSpec((1, tbk, dh), lambda bb, c, j, r: (bb, j, c)),
        scratch_shapes=[pltpu.VMEM((tbk, dh), F32)],
    )
    return pl.pallas_call(
        functools.partial(_combine_kernel, w=w, ne=ne, cap=cap),
        grid_spec=grid_spec,
        out_shape=jax.ShapeDtypeStruct((b, t, d), F32),
        compiler_params=_params(("arbitrary", "arbitrary", "arbitrary")),
        name="moe_combine",
    )(r0.reshape(b * ne, LANES), pos4, y, x)


def _final_kernel(x_ref, g_ref, o_ref):
    x = x_ref[0]
    ms = jnp.mean(x * x, axis=-1, keepdims=True)
    o_ref[0] = x * lax.rsqrt(ms + NORM_EPS) * g_ref[...]


def _final(x, g):
    b, t, d = x.shape
    tm = _pick(t, (512, 256, 128))
    return pl.pallas_call(
        _final_kernel,
        grid=(b, t // tm),
        in_specs=[pl.BlockSpec((1, tm, d), lambda bb, i: (bb, i, 0)),
                  pl.BlockSpec((1, d), lambda bb, i: (0, 0))],
        out_specs=pl.BlockSpec((1, tm, d), lambda bb, i: (bb, i, 0)),
        out_shape=jax.ShapeDtypeStruct((b, t, d), F32),
        compiler_params=_params(("arbitrary", "arbitrary")),
        name="final_norm",
    )(x, g.reshape(1, d))


def _moe(aff, hb, x, sets, w_gate, w_up, w_down, layer):
    ne = aff.shape[1]
    routed, groups = [], []
    for off, t, gate_f in sets:
        cap = 2 * t // ne
        pos, gsel, r0 = _topk(aff, cap, t, off)
        xs, gs = _gather(r0, pos, gsel, hb, cap, off)
        routed.append((r0, pos, off))
        groups.append((xs, gs, gate_f))
    ys = _ffn(groups, w_gate, w_up, w_down, layer)
    return [_combine(r0, pos, y, x, off) for (r0, pos, off), y in zip(routed, ys)]


def _rope_tables(t, n_ctx):
    rows = t // GRID_W
    row = jnp.repeat(jnp.arange(rows, dtype=F32), GRID_W)
    col = jnp.tile(jnp.arange(GRID_W, dtype=F32), rows)
    pairs = HEAD_W // 8
    freq = ROPE_BASE ** (-jnp.arange(pairs, dtype=F32) / pairs)
    ang = jnp.concatenate([row[:, None] * freq, col[:, None] * freq], axis=-1)
    cos, sin = jnp.cos(ang), jnp.sin(ang)
    cos = jnp.tile(cos, (1, 4))
    sin = jnp.tile(jnp.concatenate([-sin, sin], axis=-1), (1, 2))
    cos = jnp.concatenate([cos, jnp.ones((n_ctx, HEAD_W), F32)], axis=0)
    sin = jnp.concatenate([sin, jnp.zeros((n_ctx, HEAD_W), F32)], axis=0)
    qs = (HEAD_W // 2) ** -0.5 * math.log2(math.e)
    return jnp.stack([cos * qs, cos]), jnp.stack([sin * qs, sin])


def _diff_w_ext(w):
    d = w.shape[0]
    wv = w[:, 2 * d:].reshape(d, N_HEADS, HEAD_W)
    wv = jnp.concatenate([wv, jnp.zeros_like(wv)], axis=-1).reshape(d, 2 * d)
    return jnp.concatenate([w[:, :2 * d], wv], axis=1).astype(BF16)


def kernel(x, c, ctx, c_ctx, ada_w, ada_b, norm_mix, norm_ffn, norm_final, hgrn_w_in, hgrn_lb_logits, hgrn_norm, hgrn_w_out, diff_w_in, diff_lambda, diff_subln, diff_w_out, moe_router, moe_w_gate, moe_w_up, moe_w_down):
    b, t, d = x.shape
    n_ctx = ctx.shape[1]
    depth = ada_w.shape[0]
    ne = moe_router.shape[2]
    assert depth == 2 and d == N_HEADS * HEAD_W

    cvec = jnp.concatenate([c, c_ctx[None, :], jnp.zeros((8 - b - 1, d), F32)], axis=0)
    mod = _ada(cvec, ada_w, ada_b)
    lower_bounds = jnp.cumsum(jax.nn.softmax(hgrn_lb_logits.astype(F32), axis=0), axis=0)

    def kinds(layer, k, plus_one=False):
        m = mod[layer, :, k * d:(k + 1) * d]
        v = jnp.stack([jnp.broadcast_to(m[b], (b, d)), m[:b]], axis=1)
        return 1.0 + v if plus_one else v

    x_cat = jnp.concatenate([x, ctx], axis=1)

    proj = _inproj(x_cat, norm_mix[0], kinds(0, 1, True), kinds(0, 0), hgrn_w_in[0].astype(BF16), t, F32)
    o_f = _gla(proj, lower_bounds[0], t, False)
    o_b = _gla(proj, lower_bounds[0], t, True)
    x_cat = _hgrn_out(o_f, o_b, proj, hgrn_norm[0], hgrn_w_out[0].astype(BF16), x_cat, kinds(0, 2), t)
    hb, aff = _moe_pre(x_cat, norm_ffn[0], kinds(0, 4, True), kinds(0, 3), moe_router[0], t)
    gate_f = kinds(0, 5)
    x_lat, x_ctx = _moe(aff, hb, x_cat, [(0, t, gate_f[:, 1:2]), (t, n_ctx, gate_f[:, 0:1])],
                        moe_w_gate, moe_w_up, moe_w_down, 0)
    x_cat = jnp.concatenate([x_lat, x_ctx], axis=1)

    lam_init = 0.8 - 0.6 * math.exp(-0.3 * 1)
    qkv = _inproj(x_cat, norm_mix[1], kinds(1, 1, True), kinds(1, 0), _diff_w_ext(diff_w_in[0]), t, BF16,
                  rope_tabs=_rope_tables(t, n_ctx))
    att = _diff_attn(qkv, diff_lambda[0], diff_subln[0], t, lam_init)
    x_lat = _outproj(att, diff_w_out[0].astype(BF16), x_cat, kinds(1, 2)[:, 1:2])
    hb, aff = _moe_pre(x_lat, norm_ffn[1], kinds(1, 4, True), kinds(1, 3), moe_router[1], t)
    (x_lat,) = _moe(aff, hb, x_lat, [(0, t, kinds(1, 5)[:, 1:2])], moe_w_gate, moe_w_up, moe_w_down, 1)
    return _final(x_lat, norm_final)
```

```python
import functools
import math

import jax
import jax.numpy as jnp
import numpy as np
from jax import lax
from jax.experimental import pallas as pl
from jax.experimental.pallas import tpu as pltpu

F32 = jnp.float32
BF16 = jnp.bfloat16
I32 = jnp.int32

NORM_EPS = 1e-6
LANES = 128
HEAD_W = 128
N_HEADS = 8
GLA_CHUNK = 64
GLA_LEVELS = (32, 16, 8, 4, 2, 1)
ROPE_BASE = 10000.0
GRID_W = 64
TOKEN_BLOCK = 256
VMEM_LIMIT = 56 * 1024 * 1024

_NT = (((1,), (1,)), ((), ()))
_TN = (((0,), (0,)), ((), ()))


def _pick(n, cands):
    for c in cands:
        if n % c == 0:
            return c
    raise ValueError(f"no tile for {n} in {cands}")


def _params(sem):
    return pltpu.CompilerParams(dimension_semantics=sem, vmem_limit_bytes=VMEM_LIMIT)


def _sigmoid(x):
    return 1.0 / (1.0 + jnp.exp(-x))


def _norm_mod(x, g, scale2, shift2, row0, n_lat):
    ms = jnp.mean(x * x, axis=-1, keepdims=True)
    y = x * lax.rsqrt(ms + NORM_EPS) * g
    rows = row0 + lax.broadcasted_iota(I32, (x.shape[0], 1), 0)
    is_ctx = rows >= n_lat
    sc = jnp.where(is_ctx, scale2[0:1], scale2[1:2])
    sh = jnp.where(is_ctx, shift2[0:1], shift2[1:2])
    return y * sc + sh


def _ada_kernel(c_ref, w_ref, b_ref, o_ref):
    c = c_ref[...]
    s = c * _sigmoid(c)
    o_ref[0] = jnp.dot(s, w_ref[0], precision=lax.Precision.HIGHEST,
                       preferred_element_type=F32) + b_ref[0]


def _ada(cvec, ada_w, ada_b):
    depth, d, n = ada_w.shape
    rows = cvec.shape[0]
    tn = _pick(n, (1024, 512, 256, 128))
    return pl.pallas_call(
        _ada_kernel,
        grid=(depth, n // tn),
        in_specs=[pl.BlockSpec((rows, d), lambda l, j: (0, 0)),
                  pl.BlockSpec((1, d, tn), lambda l, j: (l, 0, j)),
                  pl.BlockSpec((1, 1, tn), lambda l, j: (l, 0, j))],
        out_specs=pl.BlockSpec((1, rows, tn), lambda l, j: (l, 0, j)),
        out_shape=jax.ShapeDtypeStruct((depth, rows, n), F32),
        compiler_params=_params(("arbitrary", "arbitrary")),
        name="adaln",
    )(cvec, ada_w, ada_b.reshape(depth, 1, n))


def _inproj_kernel(x_ref, g_ref, sc_ref, sh_ref, w_ref, *rest, tm, tn, n_lat, rope):
    if rope:
        cos_ref, sin_ref, o_ref = rest
    else:
        (o_ref,) = rest
    i = pl.program_id(1)
    h = _norm_mod(x_ref[0], g_ref[...], sc_ref[0], sh_ref[0], i * tm, n_lat).astype(BF16)
    for n in range(w_ref.shape[1] // tn):
        cols = slice(n * tn, (n + 1) * tn)
        acc = jnp.dot(h, w_ref[:, cols], preferred_element_type=F32)
        if not rope:
            o_ref[0, :, cols] = acc.astype(o_ref.dtype)
        elif n < 2:
            cos = cos_ref[n]
            sin = sin_ref[n]
            lane = lax.broadcasted_iota(I32, (1, HEAD_W), 1)
            first = (lane % 64) < 32
            for hd in range(tn // HEAD_W):
                a = acc[:, hd * HEAD_W:(hd + 1) * HEAD_W]
                rot = jnp.where(first, pltpu.roll(a, HEAD_W - 32, 1), pltpu.roll(a, 32, 1))
                o_ref[0, :, n * tn + hd * HEAD_W:n * tn + (hd + 1) * HEAD_W] = (
                    a * cos + rot * sin).astype(o_ref.dtype)
        else:
            col = lax.broadcasted_iota(I32, (1, tn), 1)
            o_ref[0, :, cols] = jnp.where(col % (2 * HEAD_W) >= HEAD_W, 1.0, acc).astype(o_ref.dtype)


def _inproj(x, g, scale2, shift2, w_bf16, n_lat, out_dtype, rope_tabs=None):
    b, r, d = x.shape
    n = w_bf16.shape[1]
    tm = _pick(r, (384, 256, 128))
    tn = 1024
    rope = rope_tabs is not None
    in_specs = [pl.BlockSpec((1, tm, d), lambda bb, i: (bb, i, 0)),
                pl.BlockSpec((1, d), lambda bb, i: (0, 0)),
                pl.BlockSpec((1, 2, d), lambda bb, i: (bb, 0, 0)),
                pl.BlockSpec((1, 2, d), lambda bb, i: (bb, 0, 0)),
                pl.BlockSpec((d, n), lambda bb, i: (0, 0))]
    args = [x, g.reshape(1, d), scale2, shift2, w_bf16]
    if rope:
        cos_t, sin_t = rope_tabs
        spec = pl.BlockSpec((2, tm, HEAD_W), lambda bb, i: (0, i, 0))
        in_specs += [spec, spec]
        args += [cos_t, sin_t]
    return pl.pallas_call(
        functools.partial(_inproj_kernel, tm=tm, tn=tn, n_lat=n_lat, rope=rope),
        grid=(b, r // tm),
        in_specs=in_specs,
        out_specs=pl.BlockSpec((1, tm, n), lambda bb, i: (bb, i, 0)),
        out_shape=jax.ShapeDtypeStruct((b, r, n), out_dtype),
        compiler_params=_params(("arbitrary", "arbitrary")),
        name="inproj_rope" if rope else "inproj",
    )(*args)


def _gla_consts(reverse):
    c = GLA_CHUNK
    nl = len(GLA_LEVELS)
    t = np.arange(c)
    p = c - 1 - t if reverse else t
    tri = (p[None, :] <= p[:, None]).astype(np.float32)
    rowsel = np.zeros((nl, c, HEAD_W), np.float32)
    masks = np.zeros((nl + 1, c, c), np.float32)
    for li, s in enumerate(GLA_LEVELS):
        blk = p // (2 * s)
        second = (p // s) % 2 == 1
        rowsel[li] = second[:, None]
        masks[li] = (blk[:, None] == blk[None, :]) & second[:, None] & (~second)[None, :]
    masks[nl] = np.eye(c)
    return tri, rowsel, masks


def _level_ref(b, s, reverse):
    c = b.shape[0]
    at = s if reverse else s - 1
    if 2 * s >= 8:
        pieces = [jnp.broadcast_to(b[blk * 2 * s + at:blk * 2 * s + at + 1], (2 * s, HEAD_W))
                  for blk in range(c // (2 * s))]
        return pieces[0] if len(pieces) == 1 else jnp.concatenate(pieces, axis=0)
    sub = lax.broadcasted_iota(I32, (8, HEAD_W), 0)
    outs = []
    for v in range(c // 8):
        acc = None
        for blk in range(8 // (2 * s)):
            row = 8 * v + blk * 2 * s + at
            cand = jnp.broadcast_to(b[row:row + 1], (8, HEAD_W))
            acc = cand if acc is None else jnp.where(sub >= blk * 2 * s, cand, acc)
        outs.append(acc)
    return jnp.concatenate(outs, axis=0)


def _gla_kernel(q_ref, f_ref, v_ref, lb_ref, tri_ref, rowsel_ref, masks_ref, o_ref, st_ref, *, tb, hg, reverse):
    c = GLA_CHUNK
    nl = len(GLA_LEVELS)
    nc = tb // c

    @pl.when(pl.program_id(2) == 0)
    def _():
        st_ref[...] = jnp.zeros_like(st_ref)

    tri = tri_ref[...]
    last = 0 if reverse else c - 1
    starts = [(nc - 1 - ci if reverse else ci) * c for ci in range(nc)]
    units = [(hd, r0) for hd in range(hg) for r0 in starts]

    qf, kk, vb, b3 = [], [], [], []
    for hd, r0 in units:
        lanes = slice(hd * HEAD_W, (hd + 1) * HEAD_W)
        lb = lb_ref[0, :, lanes]
        q = q_ref[0, r0:r0 + c, lanes]
        qf.append(q * _sigmoid(q))
        f = lb + (1.0 - lb) * _sigmoid(f_ref[0, r0:r0 + c, lanes])
        lf = jnp.log(f)
        kk.append(1.0 - f)
        vb.append(v_ref[0, r0:r0 + c, lanes].astype(BF16))
        hi = lf.astype(BF16)
        r1 = lf - hi.astype(F32)
        mid = r1.astype(BF16)
        lo = (r1 - mid.astype(F32)).astype(BF16)
        b3.append(jnp.dot(tri, jnp.concatenate([hi, mid, lo], axis=1), preferred_element_type=F32))
    bc = [x[:, :HEAD_W] + x[:, HEAD_W:2 * HEAD_W] + x[:, 2 * HEAD_W:] for x in b3]

    a = []
    for i in range(len(units)):
        ai = masks_ref[nl] * lax.dot_general(qf[i].astype(BF16), kk[i].astype(BF16), _NT,
                                             preferred_element_type=F32)
        for li, s in enumerate(GLA_LEVELS):
            e = jnp.exp(-jnp.abs(bc[i] - _level_ref(bc[i], s, reverse)))
            xb = (jnp.where(rowsel_ref[li] > 0, qf[i], kk[i]) * e).astype(BF16)
            ai = ai + masks_ref[li] * lax.dot_general(xb, xb, _NT, preferred_element_type=F32)
        a.append(ai.astype(BF16))

    o_intra = [jnp.dot(a[i], vb[i], preferred_element_type=F32) for i in range(len(units))]
    upd = []
    for i in range(len(units)):
        kdec = (kk[i] * jnp.exp(bc[i][last:last + 1] - bc[i])).astype(BF16)
        upd.append(lax.dot_general(vb[i], kdec, _TN, preferred_element_type=F32))

    st = [st_ref[hd] for hd in range(hg)]
    for i, (hd, r0) in enumerate(units):
        qd = (qf[i] * jnp.exp(bc[i])).astype(BF16)
        o_ref[0, r0:r0 + c, hd * HEAD_W:(hd + 1) * HEAD_W] = o_intra[i] + lax.dot_general(
            qd, st[hd].astype(BF16), _NT, preferred_element_type=F32)
        st[hd] = st[hd] * jnp.exp(bc[i][last:last + 1]) + upd[i]
    for hd in range(hg):
        st_ref[hd] = st[hd]


def _gla(proj, lb, n_lat, reverse):
    b, r, _ = proj.shape
    tb = TOKEN_BLOCK
    assert n_lat % tb == 0 and r % tb == 0
    nblk = r // tb
    nlb = n_lat // tb
    ncb = nblk - nlb
    tri, rowsel, masks = _gla_consts(reverse)
    nl = len(GLA_LEVELS)
    c = GLA_CHUNK
    hg = 8
    wblk = hg * HEAD_W
    fcol = N_HEADS // hg * (2 if reverse else 1)

    def blk(n):
        if reverse:
            return jnp.where(n < ncb, nblk - 1 - n, nlb - 1 - (n - ncb))
        return jnp.where(n < ncb, nlb + n, n - ncb)

    return pl.pallas_call(
        functools.partial(_gla_kernel, tb=tb, hg=hg, reverse=reverse),
        grid=(b, N_HEADS // hg, nblk),
        in_specs=[pl.BlockSpec((1, tb, wblk), lambda bb, h, n: (bb, blk(n), h)),
                  pl.BlockSpec((1, tb, wblk), lambda bb, h, n: (bb, blk(n), fcol + h)),
                  pl.BlockSpec((1, tb, wblk), lambda bb, h, n: (bb, blk(n), 3 * N_HEADS // hg + h)),
                  pl.BlockSpec((1, 1, wblk), lambda bb, h, n: (h, 0, 0)),
                  pl.BlockSpec((c, c), lambda bb, h, n: (0, 0)),
                  pl.BlockSpec((nl, c, HEAD_W), lambda bb, h, n: (0, 0, 0)),
                  pl.BlockSpec((nl + 1, c, c), lambda bb, h, n: (0, 0, 0))],
        out_specs=pl.BlockSpec((1, tb, wblk), lambda bb, h, n: (bb, blk(n), h)),
        out_shape=jax.ShapeDtypeStruct((b, r, N_HEADS * HEAD_W), F32),
        scratch_shapes=[pltpu.VMEM((hg, HEAD_W, HEAD_W), F32)],
        compiler_params=_params(("arbitrary", "arbitrary", "arbitrary")),
        name="gla_bwd" if reverse else "gla_fwd",
    )(proj, proj, proj, lb.reshape(N_HEADS // hg, 1, wblk), jnp.asarray(tri, BF16),
      jnp.asarray(rowsel), jnp.asarray(masks))


def _hgrn_out_kernel(of_ref, ob_ref, g_ref, ng_ref, w_ref, x_ref, gate_ref, o_ref, y_ref, *, tm, n_lat):
    i = pl.program_id(1)
    o = of_ref[0] + ob_ref[0]
    for h in range(N_HEADS):
        sl = slice(h * HEAD_W, (h + 1) * HEAD_W)
        oh = o[:, sl]
        ms = jnp.mean(oh * oh, axis=-1, keepdims=True)
        g = g_ref[0, :, sl]
        y_ref[:, sl] = (oh * lax.rsqrt(ms + NORM_EPS) * ng_ref[:, sl] * (g * _sigmoid(g))).astype(BF16)
    y = jnp.dot(y_ref[...], w_ref[...], preferred_element_type=F32)
    rows = i * tm + lax.broadcasted_iota(I32, (tm, 1), 0)
    gate = jnp.where(rows >= n_lat, gate_ref[0, 0:1], gate_ref[0, 1:2])
    o_ref[0] = x_ref[0] + gate * y


def _hgrn_out(o_f, o_b, proj, norm_g, w_bf16, x, gate2, n_lat):
    b, r, d = x.shape
    tm = _pick(r, (384, 256, 128))
    return pl.pallas_call(
        functools.partial(_hgrn_out_kernel, tm=tm, n_lat=n_lat),
        grid=(b, r // tm),
        in_specs=[pl.BlockSpec((1, tm, d), lambda bb, i: (bb, i, 0)),
                  pl.BlockSpec((1, tm, d), lambda bb, i: (bb, i, 0)),
                  pl.BlockSpec((1, tm, d), lambda bb, i: (bb, i, 4)),
                  pl.BlockSpec((1, d), lambda bb, i: (0, 0)),
                  pl.BlockSpec((d, d), lambda bb, i: (0, 0)),
                  pl.BlockSpec((1, tm, d), lambda bb, i: (bb, i, 0)),
                  pl.BlockSpec((1, 2, d), lambda bb, i: (bb, 0, 0))],
        out_specs=pl.BlockSpec((1, tm, d), lambda bb, i: (bb, i, 0)),
        out_shape=jax.ShapeDtypeStruct((b, r, d), F32),
        scratch_shapes=[pltpu.VMEM((tm, d), BF16)],
        compiler_params=_params(("arbitrary", "arbitrary")),
        name="hgrn_out",
    )(o_f, o_b, proj, jnp.tile(norm_g, N_HEADS).reshape(1, d), w_bf16, x, gate2)


def _attn_kernel(q_ref, k_ref, v_ref, lam_ref, g_ref, o_ref, *, tq, tk, lam_init):
    nk = k_ref.shape[1] // tk
    q = q_ref[0]
    lane = lax.broadcasted_iota(I32, (1, HEAD_W), 1)
    zero = jnp.zeros_like(q)
    qc = [jnp.where(lane < 64, q, zero), jnp.where(lane >= 64, q, zero)]

    def scores(u):
        comp, ki = u
        return lax.dot_general(qc[comp], k_ref[0, ki * tk:(ki + 1) * tk, :], _NT, preferred_element_type=F32)

    units = [(comp, ki) for ki in range(nk) for comp in range(2)]
    m = [jnp.full((tq, LANES), -jnp.inf, F32)] * 2
    acc = [jnp.zeros((tq, 2 * HEAD_W), F32)] * 2
    s_next = scores(units[0])
    for idx, (comp, ki) in enumerate(units):
        s = s_next
        if idx + 1 < len(units):
            s_next = scores(units[idx + 1])
        m_new = jnp.maximum(m[comp], jnp.max(s, axis=-1, keepdims=True))
        p = jnp.exp2((s - jnp.tile(m_new, (1, tk // LANES))).astype(BF16))
        alpha = jnp.exp2(m[comp] - m_new)
        v1 = v_ref[0, ki * tk:(ki + 1) * tk, :]
        acc[comp] = jnp.tile(alpha, (1, 2)) * acc[comp] + jnp.dot(p, v1, preferred_element_type=F32)
        m[comp] = m_new
    out = [a[:, :HEAD_W] / a[:, HEAD_W:] for a in acc]

    lv = lam_ref[...]
    s01 = jnp.sum(lv[0:1] * lv[1:2], axis=-1, keepdims=True)
    s23 = jnp.sum(lv[2:3] * lv[3:4], axis=-1, keepdims=True)
    lam = jnp.exp(s01) - jnp.exp(s23) + lam_init
    o = out[0] - lam * out[1]
    ms = jnp.mean(o * o, axis=-1, keepdims=True)
    o_ref[0] = (o * lax.rsqrt(ms + NORM_EPS) * g_ref[...] * (1.0 - lam_init)).astype(o_ref.dtype)


def _diff_attn(qkv, lam_vecs, subln_g, t, lam_init):
    b, r, _ = qkv.shape
    tq = _pick(t, (512, 256, 128))
    tk = _pick(r, (2816, 1280, 512, 256))
    return pl.pallas_call(
        functools.partial(_attn_kernel, tq=tq, tk=tk, lam_init=lam_init),
        grid=(b, N_HEADS, t // tq),
        in_specs=[pl.BlockSpec((1, tq, HEAD_W), lambda bb, h, i: (bb, i, h)),
                  pl.BlockSpec((1, r, HEAD_W), lambda bb, h, i: (bb, 0, N_HEADS + h)),
                  pl.BlockSpec((1, r, 2 * HEAD_W), lambda bb, h, i: (bb, 0, N_HEADS + h)),
                  pl.BlockSpec(lam_vecs.shape, lambda bb, h, i: (0, 0)),
                  pl.BlockSpec((1, HEAD_W), lambda bb, h, i: (0, 0))],
        out_specs=pl.BlockSpec((1, tq, HEAD_W), lambda bb, h, i: (bb, i, h)),
        out_shape=jax.ShapeDtypeStruct((b, t, N_HEADS * HEAD_W), BF16),
        compiler_params=_params(("arbitrary", "arbitrary", "arbitrary")),
        name="diff_attn",
    )(qkv, qkv, qkv, lam_vecs.astype(F32), subln_g.reshape(1, HEAD_W).astype(F32))


def _outproj_kernel(a_ref, w_ref, x_ref, gate_ref, o_ref):
    y = jnp.dot(a_ref[0], w_ref[...], preferred_element_type=F32)
    o_ref[0] = x_ref[0] + gate_ref[0] * y


def _outproj(a, w_bf16, x_cat, gate):
    b, t, d = a.shape
    tm = _pick(t, (512, 256, 128))
    return pl.pallas_call(
        _outproj_kernel,
        grid=(b, t // tm),
        in_specs=[pl.BlockSpec((1, tm, d), lambda bb, i: (bb, i, 0)),
                  pl.BlockSpec((d, d), lambda bb, i: (0, 0)),
                  pl.BlockSpec((1, tm, d), lambda bb, i: (bb, i, 0)),
                  pl.BlockSpec((1, 1, d), lambda bb, i: (bb, 0, 0))],
        out_specs=pl.BlockSpec((1, tm, d), lambda bb, i: (bb, i, 0)),
        out_shape=jax.ShapeDtypeStruct((b, t, d), F32),
        compiler_params=_params(("arbitrary", "arbitrary")),
        name="outproj",
    )(a, w_bf16, x_cat, gate)


def _moe_pre_kernel(x_ref, g_ref, sc_ref, sh_ref, wr_ref, h_ref, aff_ref, *, tm, n_lat):
    i = pl.program_id(1)
    h = _norm_mod(x_ref[0], g_ref[...], sc_ref[0], sh_ref[0], i * tm, n_lat)
    h_ref[0] = h.astype(BF16)
    logits = lax.dot_general(wr_ref[...], h, _NT, precision=lax.Precision.HIGHEST,
                             preferred_element_type=F32)
    mx = jnp.max(logits, axis=0, keepdims=True)
    ex = jnp.exp(logits - mx)
    aff_ref[0] = ex / jnp.sum(ex, axis=0, keepdims=True)


def _moe_pre(x, g, scale2, shift2, w_router, n_lat):
    b, r, d = x.shape
    ne = w_router.shape[1]
    tm = _pick(r, (768, 512, 384, 256, 128))
    return pl.pallas_call(
        functools.partial(_moe_pre_kernel, tm=tm, n_lat=n_lat),
        grid=(b, r // tm),
        in_specs=[pl.BlockSpec((1, tm, d), lambda bb, i: (bb, i, 0)),
                  pl.BlockSpec((1, d), lambda bb, i: (0, 0)),
                  pl.BlockSpec((1, 2, d), lambda bb, i: (bb, 0, 0)),
                  pl.BlockSpec((1, 2, d), lambda bb, i: (bb, 0, 0)),
                  pl.BlockSpec((ne, d), lambda bb, i: (0, 0))],
        out_specs=[pl.BlockSpec((1, tm, d), lambda bb, i: (bb, i, 0)),
                   pl.BlockSpec((1, ne, tm), lambda bb, i: (bb, 0, i))],
        out_shape=[jax.ShapeDtypeStruct((b, r, d), BF16),
                   jax.ShapeDtypeStruct((b, ne, r), F32)],
        compiler_params=_params(("arbitrary", "arbitrary")),
        name="moe_pre",
    )(x, g.reshape(1, d), scale2, shift2, w_router.T)


def _topk_kernel(aff_ref, tri_ref, pos_ref, gsel_ref, r0_ref, *, cap):
    ne, t = aff_ref.shape[1], aff_ref.shape[2]
    tbk = TOKEN_BLOCK
    nblk = t // tbk
    bits = pltpu.bitcast(aff_ref[0], I32)
    thr = jnp.zeros((ne, 1), I32)
    for bit in range(30, -1, -1):
        cand = thr | (1 << bit)
        cnt = jnp.sum(jnp.where(bits >= cand, 1.0, 0.0), axis=1, keepdims=True)
        thr = jnp.where(cnt >= cap, cand, thr)
    n_gt = jnp.sum(jnp.where(bits > thr, 1.0, 0.0), axis=1, keepdims=True)
    need = cap - n_gt
    tri = tri_ref[...]
    carry_eq = jnp.zeros((ne, 1), F32)
    carry_sel = jnp.zeros((ne, 1), F32)
    r0_ref[0] = jnp.full((ne, LANES), cap, I32)
    for j in range(nblk):
        sl = slice(j * tbk, (j + 1) * tbk)
        a = aff_ref[0, :, sl]
        bj = pltpu.bitcast(a, I32)
        eq = jnp.where(bj == thr, 1.0, 0.0)
        gt = jnp.where(bj > thr, 1.0, 0.0)
        pe = jnp.dot(eq.astype(BF16), tri, preferred_element_type=F32) + carry_eq
        sel = gt + eq * jnp.where(pe - eq < need, 1.0, 0.0)
        ps = jnp.dot(sel.astype(BF16), tri, preferred_element_type=F32) + carry_sel
        pos_ref[0, :, sl] = jnp.where(sel > 0, ps - 1.0, -1.0).astype(I32)
        gsel_ref[0, :, sl] = a * sel
        r0_ref[0, :, j:j + 1] = carry_sel.astype(I32)
        carry_eq = pe[:, tbk - 1:tbk]
        carry_sel = ps[:, tbk - 1:tbk]


def _topk(aff, cap, t, tok_off):
    b, ne, _ = aff.shape
    tbk = TOKEN_BLOCK
    assert t % tbk == 0 and t // tbk < LANES and tok_off % t == 0
    tri = jnp.asarray(np.triu(np.ones((tbk, tbk), np.float32)), BF16)
    return pl.pallas_call(
        functools.partial(_topk_kernel, cap=cap),
        grid=(b,),
        in_specs=[pl.BlockSpec((1, ne, t), lambda bb: (bb, 0, tok_off // t)),
                  pl.BlockSpec((tbk, tbk), lambda bb: (0, 0))],
        out_specs=[pl.BlockSpec((1, ne, t), lambda bb: (bb, 0, 0)),
                   pl.BlockSpec((1, ne, t), lambda bb: (bb, 0, 0)),
                   pl.BlockSpec((1, ne, LANES), lambda bb: (bb, 0, 0))],
        out_shape=[jax.ShapeDtypeStruct((b, ne, t), I32),
                   jax.ShapeDtypeStruct((b, ne, t), F32),
                   jax.ShapeDtypeStruct((b, ne, LANES), I32)],
        compiler_params=_params(("arbitrary",)),
        name="topk",
    )(aff, tri)


def _windows(r0_ref, row, j, w, cap, align):
    lo = r0_ref[row, j]
    hi = r0_ref[row, j + 1]
    off = pl.multiple_of(jnp.minimum((lo // align) * align, cap - w), align)
    return off, (jnp.maximum(hi - (off + w), 0) + w - 1) // w


def _gather_kernel(r0_ref, pos_ref, g_ref, h_ref, xs_ref, gs_ref, acc_ref, gacc_ref, *, w, nblk, ne, cap):
    b, e = pl.program_id(0), pl.program_id(1)
    row = b * ne + e
    tbk = TOKEN_BLOCK
    acc_ref[...] = jnp.zeros_like(acc_ref)
    gacc_ref[...] = jnp.zeros_like(gacc_ref)
    sub = lax.broadcasted_iota(I32, (w, tbk), 0)

    def window(j, off, first_rank):
        rank = off + sub
        if first_rank is not None:
            rank = jnp.where(rank >= first_rank, rank, -2)
        hit = pos_ref[0, 0, j:j + 1, :] == rank
        oh = jnp.where(hit, 1.0, 0.0).astype(BF16)
        acc_ref[pl.ds(off, w), :] += jnp.dot(oh, h_ref[0, j * tbk:(j + 1) * tbk, :], preferred_element_type=F32)
        gsum = jnp.sum(jnp.where(hit, g_ref[0, 0, j:j + 1, :], 0.0), axis=1, keepdims=True)
        gacc_ref[pl.ds(off, w), :] += jnp.broadcast_to(gsum, (w, LANES))

    for j in range(nblk):
        off, n_more = _windows(r0_ref, row, j, w, cap, 8)
        window(j, off, None)

        def more(i, _, j=j, off=off):
            start = off + i * w
            window(j, pl.multiple_of(jnp.minimum(start, cap - w), 8), start)
            return 0

        lax.fori_loop(1, n_more + 1, more, 0)
    xs_ref[0, 0] = acc_ref[...].astype(BF16)
    gs_ref[0, 0] = gacc_ref[...]


def _gather(r0, pos, gate, hb, cap, tok_off):
    b, ne, t = pos.shape
    d = hb.shape[2]
    assert tok_off % t == 0
    tbk = TOKEN_BLOCK
    nblk = t // tbk
    w = min(cap, 64)
    pos4 = pos.reshape(b, ne, nblk, tbk)
    g4 = gate.reshape(b, ne, nblk, tbk)
    grid_spec = pltpu.PrefetchScalarGridSpec(
        num_scalar_prefetch=1,
        grid=(b, ne),
        in_specs=[pl.BlockSpec((1, 1, nblk, tbk), lambda bb, e, r: (bb, e, 0, 0)),
                  pl.BlockSpec((1, 1, nblk, tbk), lambda bb, e, r: (bb, e, 0, 0)),
                  pl.BlockSpec((1, t, d), lambda bb, e, r: (bb, tok_off // t, 0))],
        out_specs=[pl.BlockSpec((1, 1, cap, d), lambda bb, e, r: (e, bb, 0, 0)),
                   pl.BlockSpec((1, 1, cap, LANES), lambda bb, e, r: (e, bb, 0, 0))],
        scratch_shapes=[pltpu.VMEM((cap, d), F32), pltpu.VMEM((cap, LANES), F32)],
    )
    return pl.pallas_call(
        functools.partial(_gather_kernel, w=w, nblk=nblk, ne=ne, cap=cap),
        grid_spec=grid_spec,
        out_shape=[jax.ShapeDtypeStruct((ne, b, cap, d), BF16),
                   jax.ShapeDtypeStruct((ne, b, cap, LANES), F32)],
        compiler_params=_params(("arbitrary", "arbitrary")),
        name="moe_gather",
    )(r0.reshape(b * ne, LANES), pos4, g4, hb)


def _ffn_kernel(*refs, nb, caps, rcs):
    ng = len(caps)
    ins, (wg_ref, wu_ref, wd_ref) = refs[:3 * ng], refs[3 * ng:3 * ng + 3]
    y_refs, acc_refs = refs[3 * ng + 3:4 * ng + 3], refs[4 * ng + 3:]
    f = pl.program_id(1)
    nf = pl.num_programs(1)

    @pl.when(f == 0)
    def _():
        for acc_ref in acc_refs:
            acc_ref[...] = jnp.zeros_like(acc_ref)

    wg = wg_ref[0, 0].astype(BF16)
    wu = wu_ref[0, 0].astype(BF16)
    wd = wd_ref[0, 0].astype(BF16)
    chunks = [(g, c0, rcs[g]) for g in range(ng) for c0 in range(0, nb * caps[g], rcs[g])]

    def gate_up(chunk):
        g, c0, rc = chunk
        x = ins[3 * g][0, c0:c0 + rc, :]
        return jnp.dot(x, wg, preferred_element_type=F32), jnp.dot(x, wu, preferred_element_type=F32)

    nxt = gate_up(chunks[0])
    for i, (g, c0, rc) in enumerate(chunks):
        a, u = nxt
        if i + 1 < len(chunks):
            nxt = gate_up(chunks[i + 1])
        hm = (a * _sigmoid(a) * u).astype(BF16)
        acc_refs[g][c0:c0 + rc, :] += jnp.dot(hm, wd, preferred_element_type=F32)

    @pl.when(f == nf - 1)
    def _():
        for g in range(ng):
            gs_ref, gt_ref = ins[3 * g + 1], ins[3 * g + 2]
            for bb in range(nb):
                sl = slice(bb * caps[g], (bb + 1) * caps[g])
                y_refs[g][0, sl, :] = (acc_refs[g][sl, :] * gs_ref[0, sl, 0:1] * gt_ref[bb]).astype(BF16)


def _ffn(groups, w_gate, w_up, w_down, layer):
    ne, b, _, d = groups[0][0].shape
    fdim = w_gate.shape[3]
    tf = _pick(fdim, (256, 128))
    caps = [xs.shape[2] for xs, _, _ in groups]
    rcs = [_pick(b * cap, (1024, 512, 256, 128, 64)) for cap in caps]
    in_specs, args = [], []
    for (xs, gs, gate_f), cap in zip(groups, caps):
        rows = b * cap
        in_specs += [pl.BlockSpec((1, rows, d), lambda e, f: (e, 0, 0)),
                     pl.BlockSpec((1, rows, LANES), lambda e, f: (e, 0, 0)),
                     pl.BlockSpec((b, 1, d), lambda e, f: (0, 0, 0))]
        args += [xs.reshape(ne, rows, d), gs.reshape(ne, rows, LANES), gate_f]
    in_specs += [pl.BlockSpec((1, 1, d, tf), lambda e, f: (layer, e, 0, f)),
                 pl.BlockSpec((1, 1, d, tf), lambda e, f: (layer, e, 0, f)),
                 pl.BlockSpec((1, 1, tf, d), lambda e, f: (layer, e, f, 0))]
    ys = pl.pallas_call(
        functools.partial(_ffn_kernel, nb=b, caps=caps, rcs=rcs),
        grid=(ne, fdim // tf),
        in_specs=in_specs,
        out_specs=[pl.BlockSpec((1, b * cap, d), lambda e, f: (e, 0, 0)) for cap in caps],
        out_shape=[jax.ShapeDtypeStruct((ne, b * cap, d), BF16) for cap in caps],
        scratch_shapes=[pltpu.VMEM((b * cap, d), F32) for cap in caps],
        compiler_params=_params(("arbitrary", "arbitrary")),
        name="moe_ffn",
    )(*args, w_gate, w_up, w_down)
    return [y.reshape(ne, b, cap, d) for y, cap in zip(ys, caps)]


def _combine_kernel(r0_ref, pos_ref, y_ref, x_ref, o_ref, acc_ref, *, w, ne, cap):
    b, j = pl.program_id(0), pl.program_id(2)
    tbk = TOKEN_BLOCK
    sub = lax.broadcasted_iota(I32, (w, tbk), 0)
    align = 16

    def window(e, off, first_rank):
        rank = off + sub
        if first_rank is not None:
            rank = jnp.where(rank >= first_rank, rank, -2)
        oh = jnp.where(pos_ref[0, 0, e:e + 1, :] == rank, 1.0, 0.0).astype(BF16)
        return lax.dot_general(oh, y_ref[e, 0, pl.ds(off, w), :], _TN, preferred_element_type=F32)

    acc = x_ref[0]
    more = []
    for e in range(ne):
        off, n_more = _windows(r0_ref, b * ne + e, j, w, cap, align)
        acc = acc + window(e, off, None)
        more.append((off, n_more))
    acc_ref[...] = acc
    for e, (off, n_more) in enumerate(more):
        def extra(i, _, e=e, off=off):
            start = off + i * w
            acc_ref[...] += window(e, pl.multiple_of(jnp.minimum(start, cap - w), align), start)
            return 0

        lax.fori_loop(1, n_more + 1, extra, 0)
    o_ref[0] = acc_ref[...]


def _combine(r0, pos, y, x, row_off):
    ne, b, cap, d = y.shape
    t = pos.shape[2]
    tbk = TOKEN_BLOCK
    nblk = t // tbk
    assert row_off % tbk == 0
    boff = row_off // tbk
    w = min(cap, 256)
    dh = d // 2
    pos4 = pos.reshape(b, ne, nblk, tbk).transpose(0, 2, 1, 3)
    grid_spec = pltpu.PrefetchScalarGridSpec(
        num_scalar_prefetch=1,
        grid=(b, 2, nblk),
        in_specs=[pl.BlockSpec((1, 1, ne, tbk), lambda bb, c, j, r: (bb, j, 0, 0)),
                  pl.BlockSpec((ne, 1, cap, dh), lambda bb, c, j, r: (0, bb, 0, c)),
                  pl.BlockSpec((1, tbk, dh), lambda bb, c, j, r: (bb, j + boff, c))],
        out_specs=pl.BlockSpec((1, tbk, dh), lambda bb, c, j, r: (bb, j, c)),
        scratch_shapes=[pltpu.VMEM((tbk, dh), F32)],
    )
    return pl.pallas_call(
        functools.partial(_combine_kernel, w=w, ne=ne, cap=cap),
        grid_spec=grid_spec,
        out_shape=jax.ShapeDtypeStruct((b, t, d), F32),
        compiler_params=_params(("arbitrary", "arbitrary", "arbitrary")),
        name="moe_combine",
    )(r0.reshape(b * ne, LANES), pos4, y, x)


def _final_kernel(x_ref, g_ref, o_ref):
    x = x_ref[0]
    ms = jnp.mean(x * x, axis=-1, keepdims=True)
    o_ref[0] = x * lax.rsqrt(ms + NORM_EPS) * g_ref[...]


def _final(x, g):
    b, t, d = x.shape
    tm = _pick(t, (512, 256, 128))
    return pl.pallas_call(
        _final_kernel,
        grid=(b, t // tm),
        in_specs=[pl.BlockSpec((1, tm, d), lambda bb, i: (bb, i, 0)),
                  pl.BlockSpec((1, d), lambda bb, i: (0, 0))],
        out_specs=pl.BlockSpec((1, tm, d), lambda bb, i: (bb, i, 0)),
        out_shape=jax.ShapeDtypeStruct((b, t, d), F32),
        compiler_params=_params(("arbitrary", "arbitrary")),
        name="final_norm",
    )(x, g.reshape(1, d))


def _moe(aff, hb, x, sets, w_gate, w_up, w_down, layer):
    ne = aff.shape[1]
    routed, groups = [], []
    for off, t, gate_f in sets:
        cap = 2 * t // ne
        pos, gsel, r0 = _topk(aff, cap, t, off)
        xs, gs = _gather(r0, pos, gsel, hb, cap, off)
        routed.append((r0, pos, off))
        groups.append((xs, gs, gate_f))
    ys = _ffn(groups, w_gate, w_up, w_down, layer)
    return [_combine(r0, pos, y, x, off) for (r0, pos, off), y in zip(routed, ys)]


def _rope_tables(t, n_ctx):
    rows = t // GRID_W
    row = jnp.repeat(jnp.arange(rows, dtype=F32), GRID_W)
    col = jnp.tile(jnp.arange(GRID_W, dtype=F32), rows)
    pairs = HEAD_W // 8
    freq = ROPE_BASE ** (-jnp.arange(pairs, dtype=F32) / pairs)
    ang = jnp.concatenate([row[:, None] * freq, col[:, None] * freq], axis=-1)
    cos, sin = jnp.cos(ang), jnp.sin(ang)
    cos = jnp.tile(cos, (1, 4))
    sin = jnp.tile(jnp.concatenate([-sin, sin], axis=-1), (1, 2))
    cos = jnp.concatenate([cos, jnp.ones((n_ctx, HEAD_W), F32)], axis=0)
    sin = jnp.concatenate([sin, jnp.zeros((n_ctx, HEAD_W), F32)], axis=0)
    qs = (HEAD_W // 2) ** -0.5 * math.log2(math.e)
    return jnp.stack([cos * qs, cos]), jnp.stack([sin * qs, sin])


def _diff_w_ext(w):
    d = w.shape[0]
    wv = w[:, 2 * d:].reshape(d, N_HEADS, HEAD_W)
    wv = jnp.concatenate([wv, jnp.zeros_like(wv)], axis=-1).reshape(d, 2 * d)
    return jnp.concatenate([w[:, :2 * d], wv], axis=1).astype(BF16)


def kernel(x, c, ctx, c_ctx, ada_w, ada_b, norm_mix, norm_ffn, norm_final, hgrn_w_in, hgrn_lb_logits, hgrn_norm, hgrn_w_out, diff_w_in, diff_lambda, diff_subln, diff_w_out, moe_router, moe_w_gate, moe_w_up, moe_w_down):
    b, t, d = x.shape
    n_ctx = ctx.shape[1]
    depth = ada_w.shape[0]
    ne = moe_router.shape[2]
    assert depth == 2 and d == N_HEADS * HEAD_W

    cvec = jnp.concatenate([c, c_ctx[None, :], jnp.zeros((8 - b - 1, d), F32)], axis=0)
    mod = _ada(cvec, ada_w, ada_b)
    lower_bounds = jnp.cumsum(jax.nn.softmax(hgrn_lb_logits.astype(F32), axis=0), axis=0)

    def kinds(layer, k, plus_one=False):
        m = mod[layer, :, k * d:(k + 1) * d]
        v = jnp.stack([jnp.broadcast_to(m[b], (b, d)), m[:b]], axis=1)
        return 1.0 + v if plus_one else v

    x_cat = jnp.concatenate([x, ctx], axis=1)

    proj = _inproj(x_cat, norm_mix[0], kinds(0, 1, True), kinds(0, 0), hgrn_w_in[0].astype(BF16), t, F32)
    o_f = _gla(proj, lower_bounds[0], t, False)
    o_b = _gla(proj, lower_bounds[0], t, True)
    x_cat = _hgrn_out(o_f, o_b, proj, hgrn_norm[0], hgrn_w_out[0].astype(BF16), x_cat, kinds(0, 2), t)
    hb, aff = _moe_pre(x_cat, norm_ffn[0], kinds(0, 4, True), kinds(0, 3), moe_router[0], t)
    gate_f = kinds(0, 5)
    x_lat, x_ctx = _moe(aff, hb, x_cat, [(0, t, gate_f[:, 1:2]), (t, n_ctx, gate_f[:, 0:1])],
                        moe_w_gate, moe_w_up, moe_w_down, 0)
    x_cat = jnp.concatenate([x_lat, x_ctx], axis=1)

    lam_init = 0.8 - 0.6 * math.exp(-0.3 * 1)
    qkv = _inproj(x_cat, norm_mix[1], kinds(1, 1, True), kinds(1, 0), _diff_w_ext(diff_w_in[0]), t, BF16,
                  rope_tabs=_rope_tables(t, n_ctx))
    att = _diff_attn(qkv, diff_lambda[0], diff_subln[0], t, lam_init)
    x_lat = _outproj(att, diff_w_out[0].astype(BF16), x_cat, kinds(1, 2)[:, 1:2])
    hb, aff = _moe_pre(x_lat, norm_ffn[1], kinds(1, 4, True), kinds(1, 3), moe_router[1], t)
    (x_lat,) = _moe(aff, hb, x_lat, [(0, t, kinds(1, 5)[:, 1:2])], moe_w_gate, moe_w_up, moe_w_down, 1)
    return _final(x_lat, norm_final)
```

```python
import functools
import math

import jax
import jax.numpy as jnp
import numpy as np
from jax import lax
from jax.experimental import pallas as pl
from jax.experimental.pallas import tpu as pltpu

F32 = jnp.float32
BF16 = jnp.bfloat16
I32 = jnp.int32

NORM_EPS = 1e-6
LANES = 128
HEAD_W = 128
N_HEADS = 8
GLA_CHUNK = 64
GLA_LEVELS = (32, 16, 8, 4, 2, 1)
ROPE_BASE = 10000.0
GRID_W = 64
TOKEN_BLOCK = 256
VMEM_LIMIT = 56 * 1024 * 1024

_NT = (((1,), (1,)), ((), ()))
_TN = (((0,), (0,)), ((), ()))


def _pick(n, cands):
    for c in cands:
        if n % c == 0:
            return c
    raise ValueError(f"no tile for {n} in {cands}")


def _params(sem):
    return pltpu.CompilerParams(dimension_semantics=sem, vmem_limit_bytes=VMEM_LIMIT)


def _sigmoid(x):
    return 1.0 / (1.0 + jnp.exp(-x))


def _norm_mod(x, g, scale2, shift2, row0, n_lat):
    ms = jnp.mean(x * x, axis=-1, keepdims=True)
    y = x * lax.rsqrt(ms + NORM_EPS) * g
    rows = row0 + lax.broadcasted_iota(I32, (x.shape[0], 1), 0)
    is_ctx = rows >= n_lat
    sc = jnp.where(is_ctx, scale2[0:1], scale2[1:2])
    sh = jnp.where(is_ctx, shift2[0:1], shift2[1:2])
    return y * sc + sh


def _ada_kernel(c_ref, w_ref, b_ref, o_ref):
    c = c_ref[...]
    s = c * _sigmoid(c)
    o_ref[0] = jnp.dot(s, w_ref[0], precision=lax.Precision.HIGHEST,
                       preferred_element_type=F32) + b_ref[0]


def _ada(cvec, ada_w, ada_b):
    depth, d, n = ada_w.shape
    rows = cvec.shape[0]
    tn = _pick(n, (1024, 512, 256, 128))
    return pl.pallas_call(
        _ada_kernel,
        grid=(depth, n // tn),
        in_specs=[pl.BlockSpec((rows, d), lambda l, j: (0, 0)),
                  pl.BlockSpec((1, d, tn), lambda l, j: (l, 0, j)),
                  pl.BlockSpec((1, 1, tn), lambda l, j: (l, 0, j))],
        out_specs=pl.BlockSpec((1, rows, tn), lambda l, j: (l, 0, j)),
        out_shape=jax.ShapeDtypeStruct((depth, rows, n), F32),
        compiler_params=_params(("arbitrary", "arbitrary")),
        name="adaln",
    )(cvec, ada_w, ada_b.reshape(depth, 1, n))


def _inproj_kernel(x_ref, g_ref, sc_ref, sh_ref, w_ref, *rest, tm, tn, n_lat, rope):
    if rope:
        cos_ref, sin_ref, o_ref = rest
    else:
        (o_ref,) = rest
    i = pl.program_id(1)
    h = _norm_mod(x_ref[0], g_ref[...], sc_ref[0], sh_ref[0], i * tm, n_lat).astype(BF16)
    for n in range(w_ref.shape[1] // tn):
        cols = slice(n * tn, (n + 1) * tn)
        acc = jnp.dot(h, w_ref[:, cols], preferred_element_type=F32)
        if not rope:
            o_ref[0, :, cols] = acc.astype(o_ref.dtype)
        elif n < 2:
            cos = cos_ref[n]
            sin = sin_ref[n]
            lane = lax.broadcasted_iota(I32, (1, HEAD_W), 1)
            first = (lane % 64) < 32
            for hd in range(tn // HEAD_W):
                a = acc[:, hd * HEAD_W:(hd + 1) * HEAD_W]
                rot = jnp.where(first, pltpu.roll(a, HEAD_W - 32, 1), pltpu.roll(a, 32, 1))
                o_ref[0, :, n * tn + hd * HEAD_W:n * tn + (hd + 1) * HEAD_W] = (
                    a * cos + rot * sin).astype(o_ref.dtype)
        else:
            col = lax.broadcasted_iota(I32, (1, tn), 1)
            o_ref[0, :, cols] = jnp.where(col % (2 * HEAD_W) >= HEAD_W, 1.0, acc).astype(o_ref.dtype)


def _inproj(x, g, scale2, shift2, w_bf16, n_lat, out_dtype, rope_tabs=None):
    b, r, d = x.shape
    n = w_bf16.shape[1]
    tm = _pick(r, (384, 256, 128))
    tn = 1024
    rope = rope_tabs is not None
    in_specs = [pl.BlockSpec((1, tm, d), lambda bb, i: (bb, i, 0)),
                pl.BlockSpec((1, d), lambda bb, i: (0, 0)),
                pl.BlockSpec((1, 2, d), lambda bb, i: (bb, 0, 0)),
                pl.BlockSpec((1, 2, d), lambda bb, i: (bb, 0, 0)),
                pl.BlockSpec((d, n), lambda bb, i: (0, 0))]
    args = [x, g.reshape(1, d), scale2, shift2, w_bf16]
    if rope:
        cos_t, sin_t = rope_tabs
        spec = pl.BlockSpec((2, tm, HEAD_W), lambda bb, i: (0, i, 0))
        in_specs += [spec, spec]
        args += [cos_t, sin_t]
    return pl.pallas_call(
        functools.partial(_inproj_kernel, tm=tm, tn=tn, n_lat=n_lat, rope=rope),
        grid=(b, r // tm),
        in_specs=in_specs,
        out_specs=pl.BlockSpec((1, tm, n), lambda bb, i: (bb, i, 0)),
        out_shape=jax.ShapeDtypeStruct((b, r, n), out_dtype),
        compiler_params=_params(("arbitrary", "arbitrary")),
        name="inproj_rope" if rope else "inproj",
    )(*args)


def _gla_consts(reverse):
    c = GLA_CHUNK
    nl = len(GLA_LEVELS)
    t = np.arange(c)
    p = c - 1 - t if reverse else t
    tri = (p[None, :] <= p[:, None]).astype(np.float32)
    rowsel = np.zeros((nl, c, HEAD_W), np.float32)
    masks = np.zeros((nl + 1, c, c), np.float32)
    for li, s in enumerate(GLA_LEVELS):
        blk = p // (2 * s)
        second = (p // s) % 2 == 1
        rowsel[li] = second[:, None]
        masks[li] = (blk[:, None] == blk[None, :]) & second[:, None] & (~second)[None, :]
    masks[nl] = np.eye(c)
    return tri, rowsel, masks


def _level_ref(b, s, reverse):
    c = b.shape[0]
    at = s if reverse else s - 1
    if 2 * s >= 8:
        pieces = [jnp.broadcast_to(b[blk * 2 * s + at:blk * 2 * s + at + 1], (2 * s, HEAD_W))
                  for blk in range(c // (2 * s))]
        return pieces[0] if len(pieces) == 1 else jnp.concatenate(pieces, axis=0)
    sub = lax.broadcasted_iota(I32, (8, HEAD_W), 0)
    outs = []
    for v in range(c // 8):
        acc = None
        for blk in range(8 // (2 * s)):
            row = 8 * v + blk * 2 * s + at
            cand = jnp.broadcast_to(b[row:row + 1], (8, HEAD_W))
            acc = cand if acc is None else jnp.where(sub >= blk * 2 * s, cand, acc)
        outs.append(acc)
    return jnp.concatenate(outs, axis=0)


def _gla_kernel(q_ref, f_ref, v_ref, lb_ref, tri_ref, rowsel_ref, masks_ref, o_ref, st_ref, *, tb, hg, reverse):
    c = GLA_CHUNK
    nl = len(GLA_LEVELS)
    nc = tb // c

    @pl.when(pl.program_id(2) == 0)
    def _():
        st_ref[...] = jnp.zeros_like(st_ref)

    tri = tri_ref[...]
    last = 0 if reverse else c - 1
    starts = [(nc - 1 - ci if reverse else ci) * c for ci in range(nc)]
    units = [(hd, r0) for hd in range(hg) for r0 in starts]

    qf, kk, vb, b3 = [], [], [], []
    for hd, r0 in units:
        lanes = slice(hd * HEAD_W, (hd + 1) * HEAD_W)
        lb = lb_ref[0, :, lanes]
        q = q_ref[0, r0:r0 + c, lanes]
        qf.append(q * _sigmoid(q))
        f = lb + (1.0 - lb) * _sigmoid(f_ref[0, r0:r0 + c, lanes])
        lf = jnp.log(f)
        kk.append(1.0 - f)
        vb.append(v_ref[0, r0:r0 + c, lanes].astype(BF16))
        hi = lf.astype(BF16)
        r1 = lf - hi.astype(F32)
        mid = r1.astype(BF16)
        lo = (r1 - mid.astype(F32)).astype(BF16)
        b3.append(jnp.dot(tri, jnp.concatenate([hi, mid, lo], axis=1), preferred_element_type=F32))
    bc = [x[:, :HEAD_W] + x[:, HEAD_W:2 * HEAD_W] + x[:, 2 * HEAD_W:] for x in b3]

    a = []
    for i in range(len(units)):
        ai = masks_ref[nl] * lax.dot_general(qf[i].astype(BF16), kk[i].astype(BF16), _NT,
                                             preferred_element_type=F32)
        for li, s in enumerate(GLA_LEVELS):
            e = jnp.exp(-jnp.abs(bc[i] - _level_ref(bc[i], s, reverse)))
            xb = (jnp.where(rowsel_ref[li] > 0, qf[i], kk[i]) * e).astype(BF16)
            ai = ai + masks_ref[li] * lax.dot_general(xb, xb, _NT, preferred_element_type=F32)
        a.append(ai.astype(BF16))

    o_intra = [jnp.dot(a[i], vb[i], preferred_element_type=F32) for i in range(len(units))]
    upd = []
    for i in range(len(units)):
        kdec = (kk[i] * jnp.exp(bc[i][last:last + 1] - bc[i])).astype(BF16)
        upd.append(lax.dot_general(vb[i], kdec, _TN, preferred_element_type=F32))

    st = [st_ref[hd] for hd in range(hg)]
    for i, (hd, r0) in enumerate(units):
        qd = (qf[i] * jnp.exp(bc[i])).astype(BF16)
        o_ref[0, r0:r0 + c, hd * HEAD_W:(hd + 1) * HEAD_W] = o_intra[i] + lax.dot_general(
            qd, st[hd].astype(BF16), _NT, preferred_element_type=F32)
        st[hd] = st[hd] * jnp.exp(bc[i][last:last + 1]) + upd[i]
    for hd in range(hg):
        st_ref[hd] = st[hd]


def _gla(proj, lb, n_lat, reverse):
    b, r, _ = proj.shape
    tb = TOKEN_BLOCK
    assert n_lat % tb == 0 and r % tb == 0
    nblk = r // tb
    nlb = n_lat // tb
    ncb = nblk - nlb
    tri, rowsel, masks = _gla_consts(reverse)
    nl = len(GLA_LEVELS)
    c = GLA_CHUNK
    hg = 8
    wblk = hg * HEAD_W
    fcol = N_HEADS // hg * (2 if reverse else 1)

    def blk(n):
        if reverse:
            return jnp.where(n < ncb, nblk - 1 - n, nlb - 1 - (n - ncb))
        return jnp.where(n < ncb, nlb + n, n - ncb)

    return pl.pallas_call(
        functools.partial(_gla_kernel, tb=tb, hg=hg, reverse=reverse),
        grid=(b, N_HEADS // hg, nblk),
        in_specs=[pl.BlockSpec((1, tb, wblk), lambda bb, h, n: (bb, blk(n), h)),
                  pl.BlockSpec((1, tb, wblk), lambda bb, h, n: (bb, blk(n), fcol + h)),
                  pl.BlockSpec((1, tb, wblk), lambda bb, h, n: (bb, blk(n), 3 * N_HEADS // hg + h)),
                  pl.BlockSpec((1, 1, wblk), lambda bb, h, n: (h, 0, 0)),
                  pl.BlockSpec((c, c), lambda bb, h, n: (0, 0)),
                  pl.BlockSpec((nl, c, HEAD_W), lambda bb, h, n: (0, 0, 0)),
                  pl.BlockSpec((nl + 1, c, c), lambda bb, h, n: (0, 0, 0))],
        out_specs=pl.BlockSpec((1, tb, wblk), lambda bb, h, n: (bb, blk(n), h)),
        out_shape=jax.ShapeDtypeStruct((b, r, N_HEADS * HEAD_W), F32),
        scratch_shapes=[pltpu.VMEM((hg, HEAD_W, HEAD_W), F32)],
        compiler_params=_params(("arbitrary", "arbitrary", "arbitrary")),
        name="gla_bwd" if reverse else "gla_fwd",
    )(proj, proj, proj, lb.reshape(N_HEADS // hg, 1, wblk), jnp.asarray(tri, BF16),
      jnp.asarray(rowsel), jnp.asarray(masks))


def _hgrn_out_kernel(of_ref, ob_ref, g_ref, ng_ref, w_ref, x_ref, gate_ref, o_ref, y_ref, *, tm, n_lat):
    i = pl.program_id(1)
    o = of_ref[0] + ob_ref[0]
    for h in range(N_HEADS):
        sl = slice(h * HEAD_W, (h + 1) * HEAD_W)
        oh = o[:, sl]
        ms = jnp.mean(oh * oh, axis=-1, keepdims=True)
        g = g_ref[0, :, sl]
        y_ref[:, sl] = (oh * lax.rsqrt(ms + NORM_EPS) * ng_ref[:, sl] * (g * _sigmoid(g))).astype(BF16)
    y = jnp.dot(y_ref[...], w_ref[...], preferred_element_type=F32)
    rows = i * tm + lax.broadcasted_iota(I32, (tm, 1), 0)
    gate = jnp.where(rows >= n_lat, gate_ref[0, 0:1], gate_ref[0, 1:2])
    o_ref[0] = x_ref[0] + gate * y


def _hgrn_out(o_f, o_b, proj, norm_g, w_bf16, x, gate2, n_lat):
    b, r, d = x.shape
    tm = _pick(r, (384, 256, 128))
    return pl.pallas_call(
        functools.partial(_hgrn_out_kernel, tm=tm, n_lat=n_lat),
        grid=(b, r // tm),
        in_specs=[pl.BlockSpec((1, tm, d), lambda bb, i: (bb, i, 0)),
                  pl.BlockSpec((1, tm, d), lambda bb, i: (bb, i, 0)),
                  pl.BlockSpec((1, tm, d), lambda bb, i: (bb, i, 4)),
                  pl.BlockSpec((1, d), lambda bb, i: (0, 0)),
                  pl.BlockSpec((d, d), lambda bb, i: (0, 0)),
                  pl.BlockSpec((1, tm, d), lambda bb, i: (bb, i, 0)),
                  pl.BlockSpec((1, 2, d), lambda bb, i: (bb, 0, 0))],
        out_specs=pl.BlockSpec((1, tm, d), lambda bb, i: (bb, i, 0)),
        out_shape=jax.ShapeDtypeStruct((b, r, d), F32),
        scratch_shapes=[pltpu.VMEM((tm, d), BF16)],
        compiler_params=_params(("arbitrary", "arbitrary")),
        name="hgrn_out",
    )(o_f, o_b, proj, jnp.tile(norm_g, N_HEADS).reshape(1, d), w_bf16, x, gate2)


def _attn_kernel(q_ref, k_ref, v_ref, lam_ref, g_ref, o_ref, *, tq, tk, lam_init):
    nk = k_ref.shape[1] // tk
    q = q_ref[0]
    lane = lax.broadcasted_iota(I32, (1, HEAD_W), 1)
    zero = jnp.zeros_like(q)
    qc = [jnp.where(lane < 64, q, zero), jnp.where(lane >= 64, q, zero)]

    def scores(u):
        comp, ki = u
        return lax.dot_general(qc[comp], k_ref[0, ki * tk:(ki + 1) * tk, :], _NT, preferred_element_type=F32)

    units = [(comp, ki) for ki in range(nk) for comp in range(2)]
    m = [jnp.full((tq, LANES), -jnp.inf, F32)] * 2
    acc = [jnp.zeros((tq, 2 * HEAD_W), F32)] * 2
    s_next = scores(units[0])
    for idx, (comp, ki) in enumerate(units):
        s = s_next
        if idx + 1 < len(units):
            s_next = scores(units[idx + 1])
        m_new = jnp.maximum(m[comp], jnp.max(s, axis=-1, keepdims=True))
        p = jnp.exp2(s - jnp.tile(m_new, (1, tk // LANES)))
        alpha = jnp.exp2(m[comp] - m_new)
        v1 = v_ref[0, ki * tk:(ki + 1) * tk, :]
        acc[comp] = jnp.tile(alpha, (1, 2)) * acc[comp] + jnp.dot(p.astype(BF16), v1, preferred_element_type=F32)
        m[comp] = m_new
    out = [a[:, :HEAD_W] / a[:, HEAD_W:] for a in acc]

    lv = lam_ref[...]
    s01 = jnp.sum(lv[0:1] * lv[1:2], axis=-1, keepdims=True)
    s23 = jnp.sum(lv[2:3] * lv[3:4], axis=-1, keepdims=True)
    lam = jnp.exp(s01) - jnp.exp(s23) + lam_init
    o = out[0] - lam * out[1]
    ms = jnp.mean(o * o, axis=-1, keepdims=True)
    o_ref[0] = (o * lax.rsqrt(ms + NORM_EPS) * g_ref[...] * (1.0 - lam_init)).astype(o_ref.dtype)


def _diff_attn(qkv, lam_vecs, subln_g, t, lam_init):
    b, r, _ = qkv.shape
    tq = _pick(t, (1024, 512, 256, 128))
    tk = _pick(r, (2816, 1280, 512, 256))
    return pl.pallas_call(
        functools.partial(_attn_kernel, tq=tq, tk=tk, lam_init=lam_init),
        grid=(b, N_HEADS, t // tq),
        in_specs=[pl.BlockSpec((1, tq, HEAD_W), lambda bb, h, i: (bb, i, h)),
                  pl.BlockSpec((1, r, HEAD_W), lambda bb, h, i: (bb, 0, N_HEADS + h)),
                  pl.BlockSpec((1, r, 2 * HEAD_W), lambda bb, h, i: (bb, 0, N_HEADS + h)),
                  pl.BlockSpec(lam_vecs.shape, lambda bb, h, i: (0, 0)),
                  pl.BlockSpec((1, HEAD_W), lambda bb, h, i: (0, 0))],
        out_specs=pl.BlockSpec((1, tq, HEAD_W), lambda bb, h, i: (bb, i, h)),
        out_shape=jax.ShapeDtypeStruct((b, t, N_HEADS * HEAD_W), BF16),
        compiler_params=_params(("arbitrary", "arbitrary", "arbitrary")),
        name="diff_attn",
    )(qkv, qkv, qkv, lam_vecs.astype(F32), subln_g.reshape(1, HEAD_W).astype(F32))


def _outproj_kernel(a_ref, w_ref, x_ref, gate_ref, o_ref):
    y = jnp.dot(a_ref[0], w_ref[...], preferred_element_type=F32)
    o_ref[0] = x_ref[0] + gate_ref[0] * y


def _outproj(a, w_bf16, x_cat, gate):
    b, t, d = a.shape
    tm = _pick(t, (512, 256, 128))
    return pl.pallas_call(
        _outproj_kernel,
        grid=(b, t // tm),
        in_specs=[pl.BlockSpec((1, tm, d), lambda bb, i: (bb, i, 0)),
                  pl.BlockSpec((d, d), lambda bb, i: (0, 0)),
                  pl.BlockSpec((1, tm, d), lambda bb, i: (bb, i, 0)),
                  pl.BlockSpec((1, 1, d), lambda bb, i: (bb, 0, 0))],
        out_specs=pl.BlockSpec((1, tm, d), lambda bb, i: (bb, i, 0)),
        out_shape=jax.ShapeDtypeStruct((b, t, d), F32),
        compiler_params=_params(("arbitrary", "arbitrary")),
        name="outproj",
    )(a, w_bf16, x_cat, gate)


def _moe_pre_kernel(x_ref, g_ref, sc_ref, sh_ref, wr_ref, h_ref, aff_ref, *, tm, n_lat):
    i = pl.program_id(1)
    h = _norm_mod(x_ref[0], g_ref[...], sc_ref[0], sh_ref[0], i * tm, n_lat)
    h_ref[0] = h.astype(BF16)
    logits = lax.dot_general(wr_ref[...], h, _NT, precision=lax.Precision.HIGHEST,
                             preferred_element_type=F32)
    mx = jnp.max(logits, axis=0, keepdims=True)
    ex = jnp.exp(logits - mx)
    aff_ref[0] = ex / jnp.sum(ex, axis=0, keepdims=True)


def _moe_pre(x, g, scale2, shift2, w_router, n_lat):
    b, r, d = x.shape
    ne = w_router.shape[1]
    tm = _pick(r, (768, 512, 384, 256, 128))
    return pl.pallas_call(
        functools.partial(_moe_pre_kernel, tm=tm, n_lat=n_lat),
        grid=(b, r // tm),
        in_specs=[pl.BlockSpec((1, tm, d), lambda bb, i: (bb, i, 0)),
                  pl.BlockSpec((1, d), lambda bb, i: (0, 0)),
                  pl.BlockSpec((1, 2, d), lambda bb, i: (bb, 0, 0)),
                  pl.BlockSpec((1, 2, d), lambda bb, i: (bb, 0, 0)),
                  pl.BlockSpec((ne, d), lambda bb, i: (0, 0))],
        out_specs=[pl.BlockSpec((1, tm, d), lambda bb, i: (bb, i, 0)),
                   pl.BlockSpec((1, ne, tm), lambda bb, i: (bb, 0, i))],
        out_shape=[jax.ShapeDtypeStruct((b, r, d), BF16),
                   jax.ShapeDtypeStruct((b, ne, r), F32)],
        compiler_params=_params(("arbitrary", "arbitrary")),
        name="moe_pre",
    )(x, g.reshape(1, d), scale2, shift2, w_router.T)


def _topk_kernel(aff_ref, tri_ref, pos_ref, gsel_ref, r0_ref, *, cap):
    ne, t = aff_ref.shape[1], aff_ref.shape[2]
    tbk = TOKEN_BLOCK
    nblk = t // tbk
    bits = pltpu.bitcast(aff_ref[0], I32)
    thr = jnp.zeros((ne, 1), I32)
    for bit in range(30, -1, -1):
        cand = thr | (1 << bit)
        cnt = jnp.sum(jnp.where(bits >= cand, 1.0, 0.0), axis=1, keepdims=True)
        thr = jnp.where(cnt >= cap, cand, thr)
    n_gt = jnp.sum(jnp.where(bits > thr, 1.0, 0.0), axis=1, keepdims=True)
    need = cap - n_gt
    tri = tri_ref[...]
    carry_eq = jnp.zeros((ne, 1), F32)
    carry_sel = jnp.zeros((ne, 1), F32)
    r0_ref[0] = jnp.full((ne, LANES), cap, I32)
    for j in range(nblk):
        sl = slice(j * tbk, (j + 1) * tbk)
        a = aff_ref[0, :, sl]
        bj = pltpu.bitcast(a, I32)
        eq = jnp.where(bj == thr, 1.0, 0.0)
        gt = jnp.where(bj > thr, 1.0, 0.0)
        pe = jnp.dot(eq.astype(BF16), tri, preferred_element_type=F32) + carry_eq
        sel = gt + eq * jnp.where(pe - eq < need, 1.0, 0.0)
        ps = jnp.dot(sel.astype(BF16), tri, preferred_element_type=F32) + carry_sel
        pos_ref[0, :, sl] = jnp.where(sel > 0, ps - 1.0, -1.0).astype(I32)
        gsel_ref[0, :, sl] = a * sel
        r0_ref[0, :, j:j + 1] = carry_sel.astype(I32)
        carry_eq = pe[:, tbk - 1:tbk]
        carry_sel = ps[:, tbk - 1:tbk]


def _topk(aff, cap, t, tok_off):
    b, ne, _ = aff.shape
    tbk = TOKEN_BLOCK
    assert t % tbk == 0 and t // tbk < LANES and tok_off % t == 0
    tri = jnp.asarray(np.triu(np.ones((tbk, tbk), np.float32)), BF16)
    return pl.pallas_call(
        functools.partial(_topk_kernel, cap=cap),
        grid=(b,),
        in_specs=[pl.BlockSpec((1, ne, t), lambda bb: (bb, 0, tok_off // t)),
                  pl.BlockSpec((tbk, tbk), lambda bb: (0, 0))],
        out_specs=[pl.BlockSpec((1, ne, t), lambda bb: (bb, 0, 0)),
                   pl.BlockSpec((1, ne, t), lambda bb: (bb, 0, 0)),
                   pl.BlockSpec((1, ne, LANES), lambda bb: (bb, 0, 0))],
        out_shape=[jax.ShapeDtypeStruct((b, ne, t), I32),
                   jax.ShapeDtypeStruct((b, ne, t), F32),
                   jax.ShapeDtypeStruct((b, ne, LANES), I32)],
        compiler_params=_params(("arbitrary",)),
        name="topk",
    )(aff, tri)


def _windows(r0_ref, row, j, w, cap, align):
    lo = r0_ref[row, j]
    hi = r0_ref[row, j + 1]
    off = pl.multiple_of(jnp.minimum((lo // align) * align, cap - w), align)
    return off, (jnp.maximum(hi - (off + w), 0) + w - 1) // w


def _gather_kernel(r0_ref, pos_ref, g_ref, h_ref, xs_ref, gs_ref, acc_ref, gacc_ref, *, w, nblk, ne, cap):
    b, e = pl.program_id(0), pl.program_id(1)
    row = b * ne + e
    tbk = TOKEN_BLOCK
    acc_ref[...] = jnp.zeros_like(acc_ref)
    gacc_ref[...] = jnp.zeros_like(gacc_ref)
    sub = lax.broadcasted_iota(I32, (w, tbk), 0)

    def window(j, off, first_rank):
        rank = off + sub
        if first_rank is not None:
            rank = jnp.where(rank >= first_rank, rank, -2)
        hit = pos_ref[0, 0, j:j + 1, :] == rank
        oh = jnp.where(hit, 1.0, 0.0).astype(BF16)
        acc_ref[pl.ds(off, w), :] += jnp.dot(oh, h_ref[0, j * tbk:(j + 1) * tbk, :], preferred_element_type=F32)
        gsum = jnp.sum(jnp.where(hit, g_ref[0, 0, j:j + 1, :], 0.0), axis=1, keepdims=True)
        gacc_ref[pl.ds(off, w), :] += jnp.broadcast_to(gsum, (w, LANES))

    for j in range(nblk):
        off, n_more = _windows(r0_ref, row, j, w, cap, 8)
        window(j, off, None)

        def more(i, _, j=j, off=off):
            start = off + i * w
            window(j, pl.multiple_of(jnp.minimum(start, cap - w), 8), start)
            return 0

        lax.fori_loop(1, n_more + 1, more, 0)
    xs_ref[0, 0] = acc_ref[...].astype(BF16)
    gs_ref[0, 0] = gacc_ref[...]


def _gather(r0, pos, gate, hb, cap, tok_off):
    b, ne, t = pos.shape
    d = hb.shape[2]
    assert tok_off % t == 0
    tbk = TOKEN_BLOCK
    nblk = t // tbk
    w = min(cap, 64)
    pos4 = pos.reshape(b, ne, nblk, tbk)
    g4 = gate.reshape(b, ne, nblk, tbk)
    grid_spec = pltpu.PrefetchScalarGridSpec(
        num_scalar_prefetch=1,
        grid=(b, ne),
        in_specs=[pl.BlockSpec((1, 1, nblk, tbk), lambda bb, e, r: (bb, e, 0, 0)),
                  pl.BlockSpec((1, 1, nblk, tbk), lambda bb, e, r: (bb, e, 0, 0)),
                  pl.BlockSpec((1, t, d), lambda bb, e, r: (bb, tok_off // t, 0))],
        out_specs=[pl.BlockSpec((1, 1, cap, d), lambda bb, e, r: (e, bb, 0, 0)),
                   pl.BlockSpec((1, 1, cap, LANES), lambda bb, e, r: (e, bb, 0, 0))],
        scratch_shapes=[pltpu.VMEM((cap, d), F32), pltpu.VMEM((cap, LANES), F32)],
    )
    return pl.pallas_call(
        functools.partial(_gather_kernel, w=w, nblk=nblk, ne=ne, cap=cap),
        grid_spec=grid_spec,
        out_shape=[jax.ShapeDtypeStruct((ne, b, cap, d), BF16),
                   jax.ShapeDtypeStruct((ne, b, cap, LANES), F32)],
        compiler_params=_params(("arbitrary", "arbitrary")),
        name="moe_gather",
    )(r0.reshape(b * ne, LANES), pos4, g4, hb)


def _ffn_kernel(*refs, nb, caps, rcs):
    ng = len(caps)
    ins, (wg_ref, wu_ref, wd_ref) = refs[:3 * ng], refs[3 * ng:3 * ng + 3]
    y_refs, acc_refs = refs[3 * ng + 3:4 * ng + 3], refs[4 * ng + 3:]
    f = pl.program_id(1)
    nf = pl.num_programs(1)

    @pl.when(f == 0)
    def _():
        for acc_ref in acc_refs:
            acc_ref[...] = jnp.zeros_like(acc_ref)

    wg = wg_ref[0, 0].astype(BF16)
    wu = wu_ref[0, 0].astype(BF16)
    wd = wd_ref[0, 0].astype(BF16)
    chunks = [(g, c0, rcs[g]) for g in range(ng) for c0 in range(0, nb * caps[g], rcs[g])]

    def gate_up(chunk):
        g, c0, rc = chunk
        x = ins[3 * g][0, c0:c0 + rc, :]
        return jnp.dot(x, wg, preferred_element_type=F32), jnp.dot(x, wu, preferred_element_type=F32)

    nxt = gate_up(chunks[0])
    for i, (g, c0, rc) in enumerate(chunks):
        a, u = nxt
        if i + 1 < len(chunks):
            nxt = gate_up(chunks[i + 1])
        hm = (a * _sigmoid(a) * u).astype(BF16)
        acc_refs[g][c0:c0 + rc, :] += jnp.dot(hm, wd, preferred_element_type=F32)

    @pl.when(f == nf - 1)
    def _():
        for g in range(ng):
            gs_ref, gt_ref = ins[3 * g + 1], ins[3 * g + 2]
            for bb in range(nb):
                sl = slice(bb * caps[g], (bb + 1) * caps[g])
                y_refs[g][0, sl, :] = (acc_refs[g][sl, :] * gs_ref[0, sl, 0:1] * gt_ref[bb]).astype(BF16)


def _ffn(groups, w_gate, w_up, w_down, layer):
    ne, b, _, d = groups[0][0].shape
    fdim = w_gate.shape[3]
    tf = _pick(fdim, (256, 128))
    caps = [xs.shape[2] for xs, _, _ in groups]
    rcs = [_pick(b * cap, (1024, 512, 256, 128, 64)) for cap in caps]
    in_specs, args = [], []
    for (xs, gs, gate_f), cap in zip(groups, caps):
        rows = b * cap
        in_specs += [pl.BlockSpec((1, rows, d), lambda e, f: (e, 0, 0)),
                     pl.BlockSpec((1, rows, LANES), lambda e, f: (e, 0, 0)),
                     pl.BlockSpec((b, 1, d), lambda e, f: (0, 0, 0))]
        args += [xs.reshape(ne, rows, d), gs.reshape(ne, rows, LANES), gate_f]
    in_specs += [pl.BlockSpec((1, 1, d, tf), lambda e, f: (layer, e, 0, f)),
                 pl.BlockSpec((1, 1, d, tf), lambda e, f: (layer, e, 0, f)),
                 pl.BlockSpec((1, 1, tf, d), lambda e, f: (layer, e, f, 0))]
    ys = pl.pallas_call(
        functools.partial(_ffn_kernel, nb=b, caps=caps, rcs=rcs),
        grid=(ne, fdim // tf),
        in_specs=in_specs,
        out_specs=[pl.BlockSpec((1, b * cap, d), lambda e, f: (e, 0, 0)) for cap in caps],
        out_shape=[jax.ShapeDtypeStruct((ne, b * cap, d), BF16) for cap in caps],
        scratch_shapes=[pltpu.VMEM((b * cap, d), F32) for cap in caps],
        compiler_params=_params(("arbitrary", "arbitrary")),
        name="moe_ffn",
    )(*args, w_gate, w_up, w_down)
    return [y.reshape(ne, b, cap, d) for y, cap in zip(ys, caps)]


def _combine_kernel(r0_ref, pos_ref, y_ref, x_ref, o_ref, acc_ref, *, w, ne, cap):
    b, j = pl.program_id(0), pl.program_id(2)
    tbk = TOKEN_BLOCK
    sub = lax.broadcasted_iota(I32, (w, tbk), 0)
    align = 16

    def window(e, off, first_rank):
        rank = off + sub
        if first_rank is not None:
            rank = jnp.where(rank >= first_rank, rank, -2)
        oh = jnp.where(pos_ref[0, 0, e:e + 1, :] == rank, 1.0, 0.0).astype(BF16)
        return lax.dot_general(oh, y_ref[e, 0, pl.ds(off, w), :], _TN, preferred_element_type=F32)

    acc = x_ref[0]
    more = []
    for e in range(ne):
        off, n_more = _windows(r0_ref, b * ne + e, j, w, cap, align)
        acc = acc + window(e, off, None)
        more.append((off, n_more))
    acc_ref[...] = acc
    for e, (off, n_more) in enumerate(more):
        def extra(i, _, e=e, off=off):
            start = off + i * w
            acc_ref[...] += window(e, pl.multiple_of(jnp.minimum(start, cap - w), align), start)
            return 0

        lax.fori_loop(1, n_more + 1, extra, 0)
    o_ref[0] = acc_ref[...]


def _combine(r0, pos, y, x, row_off):
    ne, b, cap, d = y.shape
    t = pos.shape[2]
    tbk = TOKEN_BLOCK
    nblk = t // tbk
    assert row_off % tbk == 0
    boff = row_off // tbk
    w = min(cap, 256)
    dh = d // 2
    pos4 = pos.reshape(b, ne, nblk, tbk).transpose(0, 2, 1, 3)
    grid_spec = pltpu.PrefetchScalarGridSpec(
        num_scalar_prefetch=1,
        grid=(b, 2, nblk),
        in_specs=[pl.BlockSpec((1, 1, ne, tbk), lambda bb, c, j, r: (bb, j, 0, 0)),
                  pl.BlockSpec((ne, 1, cap, dh), lambda bb, c, j, r: (0, bb, 0, c)),
                  pl.BlockSpec((1, tbk, dh), lambda bb, c, j, r: (bb, j + boff, c))],
        out_specs=pl.BlockSpec((1, tbk, dh), lambda bb, c, j, r: (bb, j, c)),
        scratch_shapes=[pltpu.VMEM((tbk, dh), F32)],
    )
    return pl.pallas_call(
        functools.partial(_combine_kernel, w=w, ne=ne, cap=cap),
        grid_spec=grid_spec,
        out_shape=jax.ShapeDtypeStruct((b, t, d), F32),
        compiler_params=_params(("arbitrary", "arbitrary", "arbitrary")),
        name="moe_combine",
    )(r0.reshape(b * ne, LANES), pos4, y, x)


def _final_kernel(x_ref, g_ref, o_ref):
    x = x_ref[0]
    ms = jnp.mean(x * x, axis=-1, keepdims=True)
    o_ref[0] = x * lax.rsqrt(ms + NORM_EPS) * g_ref[...]


def _final(x, g):
    b, t, d = x.shape
    tm = _pick(t, (512, 256, 128))
    return pl.pallas_call(
        _final_kernel,
        grid=(b, t // tm),
        in_specs=[pl.BlockSpec((1, tm, d), lambda bb, i: (bb, i, 0)),
                  pl.BlockSpec((1, d), lambda bb, i: (0, 0))],
        out_specs=pl.BlockSpec((1, tm, d), lambda bb, i: (bb, i, 0)),
        out_shape=jax.ShapeDtypeStruct((b, t, d), F32),
        compiler_params=_params(("arbitrary", "arbitrary")),
        name="final_norm",
    )(x, g.reshape(1, d))


def _moe(aff, hb, x, sets, w_gate, w_up, w_down, layer):
    ne = aff.shape[1]
    routed, groups = [], []
    for off, t, gate_f in sets:
        cap = 2 * t // ne
        pos, gsel, r0 = _topk(aff, cap, t, off)
        xs, gs = _gather(r0, pos, gsel, hb, cap, off)
        routed.append((r0, pos, off))
        groups.append((xs, gs, gate_f))
    ys = _ffn(groups, w_gate, w_up, w_down, layer)
    return [_combine(r0, pos, y, x, off) for (r0, pos, off), y in zip(routed, ys)]


def _rope_tables(t, n_ctx):
    rows = t // GRID_W
    row = jnp.repeat(jnp.arange(rows, dtype=F32), GRID_W)
    col = jnp.tile(jnp.arange(GRID_W, dtype=F32), rows)
    pairs = HEAD_W // 8
    freq = ROPE_BASE ** (-jnp.arange(pairs, dtype=F32) / pairs)
    ang = jnp.concatenate([row[:, None] * freq, col[:, None] * freq], axis=-1)
    cos, sin = jnp.cos(ang), jnp.sin(ang)
    cos = jnp.tile(cos, (1, 4))
    sin = jnp.tile(jnp.concatenate([-sin, sin], axis=-1), (1, 2))
    cos = jnp.concatenate([cos, jnp.ones((n_ctx, HEAD_W), F32)], axis=0)
    sin = jnp.concatenate([sin, jnp.zeros((n_ctx, HEAD_W), F32)], axis=0)
    qs = (HEAD_W // 2) ** -0.5 * math.log2(math.e)
    return jnp.stack([cos * qs, cos]), jnp.stack([sin * qs, sin])


def _diff_w_ext(w):
    d = w.shape[0]
    wv = w[:, 2 * d:].reshape(d, N_HEADS, HEAD_W)
    wv = jnp.concatenate([wv, jnp.zeros_like(wv)], axis=-1).reshape(d, 2 * d)
    return jnp.concatenate([w[:, :2 * d], wv], axis=1).astype(BF16)


def kernel(x, c, ctx, c_ctx, ada_w, ada_b, norm_mix, norm_ffn, norm_final, hgrn_w_in, hgrn_lb_logits, hgrn_norm, hgrn_w_out, diff_w_in, diff_lambda, diff_subln, diff_w_out, moe_router, moe_w_gate, moe_w_up, moe_w_down):
    b, t, d = x.shape
    n_ctx = ctx.shape[1]
    depth = ada_w.shape[0]
    ne = moe_router.shape[2]
    assert depth == 2 and d == N_HEADS * HEAD_W

    cvec = jnp.concatenate([c, c_ctx[None, :], jnp.zeros((8 - b - 1, d), F32)], axis=0)
    mod = _ada(cvec, ada_w, ada_b)
    lower_bounds = jnp.cumsum(jax.nn.softmax(hgrn_lb_logits.astype(F32), axis=0), axis=0)

    def kinds(layer, k, plus_one=False):
        m = mod[layer, :, k * d:(k + 1) * d]
        v = jnp.stack([jnp.broadcast_to(m[b], (b, d)), m[:b]], axis=1)
        return 1.0 + v if plus_one else v

    x_cat = jnp.concatenate([x, ctx], axis=1)

    proj = _inproj(x_cat, norm_mix[0], kinds(0, 1, True), kinds(0, 0), hgrn_w_in[0].astype(BF16), t, F32)
    o_f = _gla(proj, lower_bounds[0], t, False)
    o_b = _gla(proj, lower_bounds[0], t, True)
    x_cat = _hgrn_out(o_f, o_b, proj, hgrn_norm[0], hgrn_w_out[0].astype(BF16), x_cat, kinds(0, 2), t)
    hb, aff = _moe_pre(x_cat, norm_ffn[0], kinds(0, 4, True), kinds(0, 3), moe_router[0], t)
    gate_f = kinds(0, 5)
    x_lat, x_ctx = _moe(aff, hb, x_cat, [(0, t, gate_f[:, 1:2]), (t, n_ctx, gate_f[:, 0:1])],
                        moe_w_gate, moe_w_up, moe_w_down, 0)
    x_cat = jnp.concatenate([x_lat, x_ctx], axis=1)

    lam_init = 0.8 - 0.6 * math.exp(-0.3 * 1)
    qkv = _inproj(x_cat, norm_mix[1], kinds(1, 1, True), kinds(1, 0), _diff_w_ext(diff_w_in[0]), t, BF16,
                  rope_tabs=_rope_tables(t, n_ctx))
    att = _diff_attn(qkv, diff_lambda[0], diff_subln[0], t, lam_init)
    x_lat = _outproj(att, diff_w_out[0].astype(BF16), x_cat, kinds(1, 2)[:, 1:2])
    hb, aff = _moe_pre(x_lat, norm_ffn[1], kinds(1, 4, True), kinds(1, 3), moe_router[1], t)
    (x_lat,) = _moe(aff, hb, x_lat, [(0, t, kinds(1, 5)[:, 1:2])], moe_w_gate, moe_w_up, moe_w_down, 1)
    return _final(x_lat, norm_final)
```
